```python
import jax, jax.numpy as jnp
from jax import lax
import numpy as np

D_MODEL = 2048
BATCH = 1
SEQ = 16384
DEPTH = 2

GRID_W = 64
CTX_LEN = 256
N_DIR = 2
NORM_EPS = 1e-6
RWKV_HEADS = 8
RWKV_HEAD_DIM = 64
RWKV_WIDTH = RWKV_HEADS * RWKV_HEAD_DIM
DECAY_LORA = 32
ICLR_LORA = 32
VRES_LORA = 32
GATE_LORA = 96
DECAY_SCALE = 0.606531
GN_EPS = 64e-5
SHIFT_WIDTH = 3
LRU_BLOCKS = 8
LRU_BLOCK_DIM = 64
LRU_WIDTH = LRU_BLOCKS * LRU_BLOCK_DIM
LRU_CONV_WIDTH = 4
LRU_CONV_LEFT = 2
LRU_C = 8.0
LRU_A_MIN = 0.9
LRU_A_MAX = 0.999
MLA_HEADS = 8
MLA_NOPE = 128
MLA_ROPE = 64
MLA_V = 128
MLA_Q_RANK = 512
MLA_KV_RANK = 256
MLA_WIDTH = MLA_HEADS * MLA_V
MLA_SCALE = (MLA_NOPE + MLA_ROPE) ** -0.5
ROPE_THETA = 10000.0
Q_BLOCK = 128
MIX_WIDTH = RWKV_WIDTH + LRU_WIDTH + MLA_WIDTH
RWKV_COLS = 3 * RWKV_WIDTH + N_DIR * DECAY_LORA + N_DIR * ICLR_LORA + GATE_LORA
LRU_COLS = 2 * LRU_WIDTH
MLA_COLS = MLA_Q_RANK + MLA_KV_RANK + MLA_ROPE
IN_COLS = RWKV_COLS + LRU_COLS + MLA_COLS
D_FF = 5632
FFN_CONV_WIDTH = 3

kernel_name = 'hybrid_rwkv7_rglru_mla_prefix_dit'


def rms_norm(x, g):
    xf = x.astype(jnp.float32)
    y = xf * lax.rsqrt(jnp.mean(xf * xf, axis=-1, keepdims=True) + NORM_EPS)
    return (y * g.astype(jnp.float32)).astype(x.dtype)


def modulate(x, shift, scale):
    return x * (1.0 + scale) + shift


def dwconv(x, w, b, left):
    width, n = w.shape[0], x.shape[1]
    xp = jnp.pad(x, ((0, 0), (left, width - 1 - left), (0, 0)))
    y = b + xp[:, 0:n] * w[0]
    for j in range(1, width):
        y = y + xp[:, j:j + n] * w[j]
    return y


def axial_rope_tables(rows):
    n_freq = MLA_ROPE // 4
    row = jnp.repeat(jnp.arange(rows, dtype=jnp.float32), GRID_W)
    col = jnp.tile(jnp.arange(GRID_W, dtype=jnp.float32), rows)
    inv_freq = ROPE_THETA ** (-jnp.arange(n_freq, dtype=jnp.float32) / n_freq)
    ang = jnp.stack([row[:, None] * inv_freq, col[:, None] * inv_freq], axis=1)
    return jnp.cos(ang), jnp.sin(ang)


def apply_axial_rope(x, cos, sin):
    n_freq = MLA_ROPE // 4
    xs = x.astype(jnp.float32).reshape(x.shape[:-1] + (2, 2, n_freq))
    a, b = xs[..., 0, :], xs[..., 1, :]
    out = jnp.stack([a * cos - b * sin, a * sin + b * cos], axis=-2)
    return out.reshape(x.shape).astype(x.dtype)


def rwkv_prepare(h, cols, v_first, lp):
    B, L, _ = h.shape
    f32 = jnp.float32
    u = dwconv(cols, lp['rw_conv'], lp['rw_conv_b'], SHIFT_WIDTH // 2)
    splits = np.cumsum([RWKV_WIDTH, RWKV_WIDTH, RWKV_WIDTH, N_DIR * DECAY_LORA, N_DIR * ICLR_LORA]).tolist()
    r, k, v, wd, ad, gd = jnp.split(u, splits, axis=-1)
    wd = wd.reshape(B, L, N_DIR, DECAY_LORA)
    ad = ad.reshape(B, L, N_DIR, ICLR_LORA)
    z_w = lp['rw_w0'] + jnp.einsum('blde,dec->bldc', jnp.tanh(wd), lp['rw_w_up'])
    decay = jnp.exp(-DECAY_SCALE * jax.nn.sigmoid(z_w.astype(f32)))
    iclr = jax.nn.sigmoid((lp['rw_a0'] + jnp.einsum('blde,dec->bldc', ad, lp['rw_a_up'])).astype(f32))
    g = jax.nn.sigmoid(gd) @ lp['rw_g_up']
    if v_first is not None:
        mix = jax.nn.sigmoid(lp['rw_v0'] + (h @ lp['rw_v_down']) @ lp['rw_v_up'])
        v = v + (v_first - v) * mix
    heads = lambda t: t.reshape(t.shape[:-1] + (RWKV_HEADS, RWKV_HEAD_DIM))
    kk = heads(k * lp['rw_k_k']).astype(f32)
    kk = kk * lax.rsqrt(jnp.maximum(jnp.sum(kk * kk, axis=-1, keepdims=True), 1e-24))
    k_dir = heads(k[:, :, None].astype(f32) * (1.0 + (iclr - 1.0) * lp['rw_k_a']))
    b_dir = kk[:, :, None] * heads(iclr)
    return heads(r).astype(f32), heads(decay), k_dir, heads(v).astype(f32), -kk, b_dir, g, v


def rwkv_scan(state0, decay, k, v, a_in, b_in, r, reverse):
    def step(S, inp):
        w_t, k_t, v_t, a_t, b_t = inp[:5]
        sa = jnp.einsum('bhvk,bhk->bhv', S, a_t)
        S = S * w_t[:, :, None, :] + sa[..., None] * b_t[:, :, None, :] + v_t[..., None] * k_t[:, :, None, :]
        y = jnp.einsum('bhvk,bhk->bhv', S, inp[5]) if len(inp) == 6 else None
        return S, y
    xs = (decay, k, v, a_in, b_in) + (() if r is None else (r,))
    S, ys = lax.scan(step, state0, tuple(jnp.moveaxis(t, 1, 0) for t in xs), reverse=reverse)
    return S, (None if r is None else jnp.moveaxis(ys, 0, 1))


def rwkv_output(y, r, k_dir, v, g, lp):
    B, L = y.shape[:2]
    mu = jnp.mean(y, axis=-1, keepdims=True)
    var = jnp.mean(jnp.square(y - mu), axis=-1, keepdims=True)
    yn = ((y - mu) * lax.rsqrt(var + GN_EPS)).reshape(B, L, RWKV_WIDTH) * lp['rw_ln_g'] + lp['rw_ln_b']
    coef = jnp.sum(jnp.sum(r[:, :, None] * k_dir * lp['rw_r_k'], axis=-1, keepdims=True), axis=2)
    out = (yn + (coef * v).reshape(B, L, RWKV_WIDTH)) * g
    return out.astype(g.dtype)


def rwkv_mixer(h_ctx, cols_ctx, h_lat, cols_lat, vf_ctx, vf_lat, lp, emit_ctx):
    rc, wc, kc, vc, ac, bc, gc, v_ctx = rwkv_prepare(h_ctx, cols_ctx, vf_ctx, lp)
    rl, wl, kl, vl, al, bl, gl, v_lat = rwkv_prepare(h_lat, cols_lat, vf_lat, lp)
    state0 = jnp.zeros((h_lat.shape[0], RWKV_HEADS, RWKV_HEAD_DIM, RWKV_HEAD_DIM), jnp.float32)
    ys_c, ys_l = [], []
    for d in range(N_DIR):
        rev = d == 1
        S_c, y_c = rwkv_scan(state0, wc[:, :, d], kc[:, :, d], vc, ac, bc[:, :, d], rc if emit_ctx else None, rev)
        _, y_l = rwkv_scan(S_c, wl[:, :, d], kl[:, :, d], vl, al, bl[:, :, d], rl, rev)
        ys_c.append(y_c)
        ys_l.append(y_l)
    out_lat = rwkv_output(ys_l[0] + ys_l[1], rl, kl, vl, gl, lp)
    out_ctx = rwkv_output(ys_c[0] + ys_c[1], rc, kc, vc, gc, lp) if emit_ctx else None
    return out_ctx, out_lat, v_ctx, v_lat


def lru_prepare(x_cols, lp):
    B, L, _ = x_cols.shape
    f32 = jnp.float32
    xb = dwconv(x_cols, lp['lru_conv'], lp['lru_conv_b'], LRU_CONV_LEFT)
    xh = xb.reshape(B, L, LRU_BLOCKS, LRU_BLOCK_DIM)
    def block_diag(w, b):
        return jnp.einsum('blhi,dhij->bldhj', xh, w).reshape(B, L, N_DIR, LRU_WIDTH) + b
    gate_r = jax.nn.sigmoid(block_diag(lp['lru_wa'], lp['lru_ba']).astype(f32))
    gate_i = jax.nn.sigmoid(block_diag(lp['lru_wx'], lp['lru_bx']).astype(f32))
    log_a = -LRU_C * gate_r * jax.nn.softplus(-lp['lru_lambda'].astype(f32))
    a = jnp.exp(log_a)
    b = jnp.sqrt(-jnp.expm1(2.0 * log_a)) * gate_i * xb[:, :, None].astype(f32)
    return a, b


def linear_scan(a, b, h0, reverse):
    def combine(e1, e2):
        return e1[0] * e2[0], e2[0] * e1[1] + e2[1]
    a_cum, b_cum = lax.associative_scan(combine, (a, b), reverse=reverse, axis=1)
    return a_cum * h0[:, None] + b_cum


def lru_mixer(cols_ctx, cols_lat, lp, emit_ctx):
    x_c, gate_c = jnp.split(cols_ctx, 2, axis=-1)
    x_l, gate_l = jnp.split(cols_lat, 2, axis=-1)
    a_c, b_c = lru_prepare(x_c, lp)
    a_l, b_l = lru_prepare(x_l, lp)
    h0 = jnp.zeros((cols_ctx.shape[0], LRU_WIDTH), jnp.float32)
    hs_c, hs_l = [], []
    for d in range(N_DIR):
        rev = d == 1
        h_c = linear_scan(a_c[:, :, d], b_c[:, :, d], h0, rev)
        h_end = h_c[:, 0] if rev else h_c[:, -1]
        hs_l.append(linear_scan(a_l[:, :, d], b_l[:, :, d], h_end, rev))
        hs_c.append(h_c)
    out_lat = ((hs_l[0] + hs_l[1]) * jax.nn.gelu(gate_l.astype(jnp.float32))).astype(cols_lat.dtype)
    out_ctx = None
    if emit_ctx:
        out_ctx = ((hs_c[0] + hs_c[1]) * jax.nn.gelu(gate_c.astype(jnp.float32))).astype(cols_ctx.dtype)
    return out_ctx, out_lat


def mla_attend(q_nope, q_rope, k_nope, k_rope, v):
    s = (jnp.einsum('bqhd,bkhd->bhqk', q_nope, k_nope)
         + jnp.einsum('bqhr,bkr->bhqk', q_rope, k_rope)).astype(jnp.float32) * MLA_SCALE
    p = jax.nn.softmax(s, axis=-1).astype(v.dtype)
    return jnp.einsum('bhqk,bkhd->bqhd', p, v)


def blocked_attention(q_nope, q_rope, k_nope, k_rope, v):
    B, L = q_nope.shape[:2]
    n_blocks = L // Q_BLOCK
    def blocks(t):
        return jnp.moveaxis(t.reshape((B, n_blocks, Q_BLOCK) + t.shape[2:]), 1, 0)
    out = lax.map(lambda qb: mla_attend(qb[0], qb[1], k_nope, k_rope, v), (blocks(q_nope), blocks(q_rope)))
    return jnp.moveaxis(out, 0, 1).reshape(B, L, MLA_WIDTH)


def mla_mixer(cols_ctx, cols_lat, lp, cos, sin, emit_ctx):
    def project(cols, with_q):
        B, L, _ = cols.shape
        p_q, c_kv, k_rope = jnp.split(cols, [MLA_Q_RANK, MLA_Q_RANK + MLA_KV_RANK], axis=-1)
        kv = (rms_norm(c_kv, lp['mla_kv_norm']) @ lp['mla_w_kvb']).reshape(B, L, MLA_HEADS, MLA_NOPE + MLA_V)
        k_nope, v = kv[..., :MLA_NOPE], kv[..., MLA_NOPE:]
        if not with_q:
            return None, None, k_nope, k_rope, v
        q = (rms_norm(p_q, lp['mla_q_norm']) @ lp['mla_w_qb']).reshape(B, L, MLA_HEADS, MLA_NOPE + MLA_ROPE)
        return q[..., :MLA_NOPE], q[..., MLA_NOPE:], k_nope, k_rope, v
    qn_c, qr_c, kn_c, kr_c, v_c = project(cols_ctx, emit_ctx)
    qn_l, qr_l, kn_l, kr_l, v_l = project(cols_lat, True)
    qr_l = apply_axial_rope(qr_l, cos[:, None], sin[:, None])
    kr_l = apply_axial_rope(kr_l, cos, sin)
    out_lat = blocked_attention(qn_l, qr_l,
                                jnp.concatenate([kn_c, kn_l], axis=1),
                                jnp.concatenate([kr_c, kr_l], axis=1),
                                jnp.concatenate([v_c, v_l], axis=1))
    out_ctx = None
    if emit_ctx:
        B, Lc = cols_ctx.shape[:2]
        out_ctx = mla_attend(qn_c, qr_c, kn_c, kr_c, v_c).reshape(B, Lc, MLA_WIDTH)
    return out_ctx, out_lat


def conv_ffn(h, lp):
    gate = dwconv(h @ lp['ffn_w_gate'], lp['ffn_conv'], lp['ffn_conv_b'], FFN_CONV_WIDTH // 2)
    return (jax.nn.silu(gate) * (h @ lp['ffn_w_up'])) @ lp['ffn_w_down']


def setup_inputs(seed: int = 0) -> dict:
    f32 = jnp.float32
    keys = iter(jax.random.split(jax.random.key(seed), 64))
    def normal(shape, scale):
        return jax.random.normal(next(keys), shape, f32) * scale
    def gain(shape):
        return 1.0 + normal(shape, 0.05)
    def uniform(shape, lo, hi):
        return jax.random.uniform(next(keys), shape, f32, lo, hi)
    nv = max(DEPTH - 1, 0)
    lam_u = uniform((DEPTH, N_DIR, LRU_WIDTH), LRU_A_MIN ** (1.0 / LRU_C), LRU_A_MAX ** (1.0 / LRU_C))
    return {
        'x': normal((BATCH, SEQ, D_MODEL), 1.0),
        'c': normal((BATCH, D_MODEL), 1.0),
        'ctx': normal((BATCH, CTX_LEN, D_MODEL), 1.0),
        'c_ctx': normal((D_MODEL,), 1.0),
        'ada_w': normal((DEPTH, D_MODEL, 6 * D_MODEL), 0.5 * D_MODEL ** -0.5),
        'ada_b': normal((DEPTH, 6 * D_MODEL), 0.01),
        'norm1': gain((DEPTH, D_MODEL)),
        'norm2': gain((DEPTH, D_MODEL)),
        'w_in': normal((DEPTH, D_MODEL, IN_COLS), D_MODEL ** -0.5),
        'w_out': normal((DEPTH, MIX_WIDTH, D_MODEL), MIX_WIDTH ** -0.5),
        'rw_conv': normal((DEPTH, SHIFT_WIDTH, RWKV_COLS), SHIFT_WIDTH ** -0.5),
        'rw_conv_b': normal((DEPTH, RWKV_COLS), 0.01),
        'rw_w0': uniform((DEPTH, N_DIR, RWKV_WIDTH), -6.0, 1.0),
        'rw_w_up': normal((DEPTH, N_DIR, DECAY_LORA, RWKV_WIDTH), 0.5 * DECAY_LORA ** -0.5),
        'rw_a0': normal((DEPTH, N_DIR, RWKV_WIDTH), 0.5),
        'rw_a_up': normal((DEPTH, N_DIR, ICLR_LORA, RWKV_WIDTH), 0.5 * ICLR_LORA ** -0.5),
        'rw_g_up': normal((DEPTH, GATE_LORA, RWKV_WIDTH), GATE_LORA ** -0.5),
        'rw_k_k': 0.85 + normal((DEPTH, RWKV_WIDTH), 0.05),
        'rw_k_a': gain((DEPTH, RWKV_WIDTH)),
        'rw_r_k': normal((DEPTH, RWKV_HEADS, RWKV_HEAD_DIM), 0.1),
        'rw_ln_g': gain((DEPTH, RWKV_WIDTH)),
        'rw_ln_b': normal((DEPTH, RWKV_WIDTH), 0.01),
        'rw_v0': normal((nv, RWKV_WIDTH), 0.5),
        'rw_v_down': normal((nv, D_MODEL, VRES_LORA), D_MODEL ** -0.5),
        'rw_v_up': normal((nv, VRES_LORA, RWKV_WIDTH), 0.5 * VRES_LORA ** -0.5),
        'lru_conv': normal((DEPTH, LRU_CONV_WIDTH, LRU_WIDTH), LRU_CONV_WIDTH ** -0.5),
        'lru_conv_b': normal((DEPTH, LRU_WIDTH), 0.01),
        'lru_wa': normal((DEPTH, N_DIR, LRU_BLOCKS, LRU_BLOCK_DIM, LRU_BLOCK_DIM), LRU_BLOCK_DIM ** -0.5),
        'lru_ba': normal((DEPTH, N_DIR, LRU_WIDTH), 0.01),
        'lru_wx': normal((DEPTH, N_DIR, LRU_BLOCKS, LRU_BLOCK_DIM, LRU_BLOCK_DIM), LRU_BLOCK_DIM ** -0.5),
        'lru_bx': normal((DEPTH, N_DIR, LRU_WIDTH), 0.01),
        'lru_lambda': jnp.log(lam_u) - jnp.log1p(-lam_u),
        'mla_q_norm': gain((DEPTH, MLA_Q_RANK)),
        'mla_w_qb': normal((DEPTH, MLA_Q_RANK, MLA_HEADS * (MLA_NOPE + MLA_ROPE)), MLA_Q_RANK ** -0.5),
        'mla_kv_norm': gain((DEPTH, MLA_KV_RANK)),
        'mla_w_kvb': normal((DEPTH, MLA_KV_RANK, MLA_HEADS * (MLA_NOPE + MLA_V)), MLA_KV_RANK ** -0.5),
        'ffn_w_gate': normal((DEPTH, D_MODEL, D_FF), D_MODEL ** -0.5),
        'ffn_w_up': normal((DEPTH, D_MODEL, D_FF), D_MODEL ** -0.5),
        'ffn_conv': normal((DEPTH, FFN_CONV_WIDTH, D_FF), FFN_CONV_WIDTH ** -0.5),
        'ffn_conv_b': normal((DEPTH, D_FF), 0.01),
        'ffn_w_down': normal((DEPTH, D_FF, D_MODEL), D_FF ** -0.5),
        'final_norm': gain((D_MODEL,)),
    }


def reference(x, c, ctx, c_ctx, ada_w, ada_b, norm1, norm2, w_in, w_out,
              rw_conv, rw_conv_b, rw_w0, rw_w_up, rw_a0, rw_a_up, rw_g_up, rw_k_k, rw_k_a, rw_r_k,
              rw_ln_g, rw_ln_b, rw_v0, rw_v_down, rw_v_up,
              lru_conv, lru_conv_b, lru_wa, lru_ba, lru_wx, lru_bx, lru_lambda,
              mla_q_norm, mla_w_qb, mla_kv_norm, mla_w_kvb,
              ffn_w_gate, ffn_w_up, ffn_conv, ffn_conv_b, ffn_w_down, final_norm):
    n_tokens = x.shape[1]
    rows = n_tokens // GRID_W
    cos, sin = axial_rope_tables(rows)
    col_splits = [RWKV_COLS, RWKV_COLS + LRU_COLS]
    vf_ctx, vf_lat = None, None
    for i in range(DEPTH):
        lp = {
            'rw_conv': rw_conv[i], 'rw_conv_b': rw_conv_b[i], 'rw_w0': rw_w0[i], 'rw_w_up': rw_w_up[i],
            'rw_a0': rw_a0[i], 'rw_a_up': rw_a_up[i], 'rw_g_up': rw_g_up[i], 'rw_k_k': rw_k_k[i],
            'rw_k_a': rw_k_a[i], 'rw_r_k': rw_r_k[i], 'rw_ln_g': rw_ln_g[i], 'rw_ln_b': rw_ln_b[i],
            'lru_conv': lru_conv[i], 'lru_conv_b': lru_conv_b[i], 'lru_wa': lru_wa[i], 'lru_ba': lru_ba[i],
            'lru_wx': lru_wx[i], 'lru_bx': lru_bx[i], 'lru_lambda': lru_lambda[i],
            'mla_q_norm': mla_q_norm[i], 'mla_w_qb': mla_w_qb[i], 'mla_kv_norm': mla_kv_norm[i],
            'mla_w_kvb': mla_w_kvb[i],
            'ffn_w_gate': ffn_w_gate[i], 'ffn_w_up': ffn_w_up[i], 'ffn_conv': ffn_conv[i],
            'ffn_conv_b': ffn_conv_b[i], 'ffn_w_down': ffn_w_down[i],
        }
        if i > 0:
            lp['rw_v0'] = rw_v0[i - 1]
            lp['rw_v_down'] = rw_v_down[i - 1]
            lp['rw_v_up'] = rw_v_up[i - 1]
        emit_ctx = i < DEPTH - 1
        mod_lat = jax.nn.silu(c) @ ada_w[i] + ada_b[i]
        mod_ctx = jax.nn.silu(c_ctx) @ ada_w[i] + ada_b[i]
        sh1_l, sc1_l, g1_l, sh2_l, sc2_l, g2_l = [m[:, None] for m in jnp.split(mod_lat, 6, axis=-1)]
        sh1_c, sc1_c, g1_c, sh2_c, sc2_c, g2_c = jnp.split(mod_ctx, 6, axis=-1)
        h_lat = modulate(rms_norm(x, norm1[i]), sh1_l, sc1_l)
        h_ctx = modulate(rms_norm(ctx, norm1[i]), sh1_c, sc1_c)
        rw_l, lru_l, mla_l = jnp.split(h_lat @ w_in[i], col_splits, axis=-1)
        rw_c, lru_c, mla_c = jnp.split(h_ctx @ w_in[i], col_splits, axis=-1)
        o_rw_c, o_rw_l, v_c, v_l = rwkv_mixer(h_ctx, rw_c, h_lat, rw_l, vf_ctx, vf_lat, lp, emit_ctx)
        if i == 0:
            vf_ctx, vf_lat = v_c, v_l
        o_lru_c, o_lru_l = lru_mixer(lru_c, lru_l, lp, emit_ctx)
        o_mla_c, o_mla_l = mla_mixer(mla_c, mla_l, lp, cos, sin, emit_ctx)
        x = x + g1_l * (jnp.concatenate([o_rw_l, o_lru_l, o_mla_l], axis=-1) @ w_out[i])
        x = x + g2_l * conv_ffn(modulate(rms_norm(x, norm2[i]), sh2_l, sc2_l), lp)
        if emit_ctx:
            ctx = ctx + g1_c * (jnp.concatenate([o_rw_c, o_lru_c, o_mla_c], axis=-1) @ w_out[i])
            ctx = ctx + g2_c * conv_ffn(modulate(rms_norm(ctx, norm2[i]), sh2_c, sc2_c), lp)
    return rms_norm(x, final_norm)
```

```python
import functools

import numpy as np
import jax
import jax.numpy as jnp
from jax import lax
from jax.experimental import pallas as pl
from jax.experimental.pallas import tpu as pltpu

F32 = jnp.float32
BF16 = jnp.bfloat16

D_MODEL = 2048
NORM_EPS = 1e-6
GN_EPS = 64e-5
DECAY_SCALE = 0.606531
LRU_C = 8.0
HEADS = 8
HEAD_DIM = 64
WIDTH = HEADS * HEAD_DIM
LORA = 32
GATE_LORA = 96
RW_COLS = 1792
LRU_COLS = 1024
MLA_COLS = 896
Q_RANK = 512
KV_RANK = 256
NOPE = 128
ROPE = 64
V_DIM = 128
QK_DIM = 256
MLA_WIDTH = HEADS * V_DIM
MLA_SCALE = (NOPE + ROPE) ** -0.5
ROPE_THETA = 10000.0
GRID_W = 64
D_FF = 5632

TILE = 256
CHUNK = 64
GROUP = 4
GW = GROUP * HEAD_DIM
HALO = 8
FFN_TILE = 512
FFN_HALO = 16
FFN_CHUNK = 512
Q_TILE = 512
VMEM_LIMIT = 56 * 1024 * 1024


def _params(n_axes, vmem=VMEM_LIMIT):
    return pltpu.CompilerParams(dimension_semantics=("arbitrary",) * n_axes, vmem_limit_bytes=vmem)


def _const_spec(shape):
    nd = len(shape)
    return pl.BlockSpec(shape, lambda *_: (0,) * nd, pipeline_mode=pl.Buffered(1))


def _sigmoid(x):
    return 1.0 / (1.0 + jnp.exp(-x))


def _silu(x):
    return x * _sigmoid(x)


def _gelu_tanh(x):
    return 0.5 * x * (1.0 + jnp.tanh(0.7978845608028654 * (x + 0.044715 * (x * x * x))))


def _rms(x, g):
    return x * lax.rsqrt(jnp.mean(x * x, axis=-1, keepdims=True) + NORM_EPS) * g


def _row_iota(shape):
    return lax.broadcasted_iota(jnp.int32, shape, 0)


def _shift_down(cur, prev_rows, k):
    out = pltpu.roll(cur, k, 0)
    rows = _row_iota(cur.shape)
    for i in range(k):
        out = jnp.where(rows == i, prev_rows[i:i + 1, :], out)
    return out


def _shift_up(cur, next_row):
    n = cur.shape[0]
    out = pltpu.roll(cur, n - 1, 0)
    return jnp.where(_row_iota(cur.shape) == n - 1, next_row, out)


def _halo_valid(j, tile, seg_starts, seg_ends):
    first = j * tile
    last = first + tile
    lvalid = jnp.logical_and(first != seg_starts[0], first != seg_starts[1])
    rvalid = jnp.logical_and(last != seg_ends[0], last != seg_ends[1])
    return lvalid.astype(F32), rvalid.astype(F32)


def _mod_kernel(cc_ref, w_ref, b_ref, o_ref):
    s = _silu(cc_ref[...])
    w = w_ref[0]
    b = b_ref[0]
    o_ref[0, 0:1, :] = jnp.sum(s[:, 0:1] * w, axis=0, keepdims=True) + b
    o_ref[0, 1:2, :] = jnp.sum(s[:, 1:2] * w, axis=0, keepdims=True) + b


def _mod_call(cc, ada_w, ada_b):
    depth, d, n6 = ada_w.shape
    tn = 1024
    return pl.pallas_call(
        _mod_kernel,
        grid=(depth, n6 // tn),
        in_specs=[
            pl.BlockSpec((d, 2), lambda i, j: (0, 0)),
            pl.BlockSpec((1, d, tn), lambda i, j: (i, 0, j)),
            pl.BlockSpec((1, 1, tn), lambda i, j: (i, 0, j)),
        ],
        out_specs=pl.BlockSpec((1, 2, tn), lambda i, j: (i, 0, j)),
        out_shape=jax.ShapeDtypeStruct((depth, 2, n6), F32),
        compiler_params=_params(2),
        name="adaln_mod",
    )(cc, ada_w, ada_b.reshape(depth, 1, n6))


def _in_proj_kernel(x_ref, mod_ref, g_ref, w_ref, rw_ref, lru_ref, mla_ref):
    h = _rms(x_ref[...], g_ref[...]) * (1.0 + mod_ref[0, 1:2, :]) + mod_ref[0, 0:1, :]
    hb = h.astype(BF16)
    rw_ref[...] = jnp.dot(hb, w_ref[:, 0:RW_COLS], preferred_element_type=F32)
    lru_ref[...] = jnp.dot(hb, w_ref[:, RW_COLS:RW_COLS + LRU_COLS], preferred_element_type=F32)
    mla_ref[...] = jnp.dot(hb, w_ref[:, RW_COLS + LRU_COLS:], preferred_element_type=F32)


def _in_proj_call(x_all, mod, g, w_cat, n_lat_tiles):
    n = x_all.shape[0]
    cols = w_cat.shape[1]
    row = lambda c: pl.BlockSpec((TILE, c), lambda j: (j, 0))
    return pl.pallas_call(
        _in_proj_kernel,
        grid=(n // TILE,),
        in_specs=[
            row(D_MODEL),
            pl.BlockSpec((1, 6, D_MODEL), lambda j: (jnp.minimum(j // n_lat_tiles, 1), 0, 0)),
            _const_spec((1, D_MODEL)),
            _const_spec((D_MODEL, cols)),
        ],
        out_specs=[row(RW_COLS), row(LRU_COLS), row(MLA_COLS)],
        out_shape=[jax.ShapeDtypeStruct((n, c), F32) for c in (RW_COLS, LRU_COLS, MLA_COLS)],
        compiler_params=_params(1),
        name="in_proj",
    )(x_all, mod, g, w_cat)


def _rwkv_prep_kernel(seg, has_vfirst, *refs):
    if has_vfirst:
        (cur_ref, prev_ref, next_ref, vf_ref, cw_ref, cb_ref, wl_ref, bl_ref, kk_ref, ka_ref, rk_ref, ones_ref,
         r_out, v_out, kk_out, g_out, cv_out, lw_out, kd_out, bd_out) = refs
    else:
        (cur_ref, prev_ref, next_ref, cw_ref, cb_ref, wl_ref, bl_ref, kk_ref, ka_ref, rk_ref, ones_ref,
         r_out, v_out, kk_out, g_out, cv_out, lw_out, kd_out, bd_out) = refs
    lvalid, rvalid = _halo_valid(pl.program_id(0), TILE, seg[0], seg[1])
    cur = cur_ref[...]
    up = _shift_down(cur, prev_ref[HALO - 1:HALO, :] * lvalid, 1)
    dn = _shift_up(cur, next_ref[0:1, :] * rvalid)
    u = cb_ref[...] + up * cw_ref[0:1, :] + cur * cw_ref[1:2, :] + dn * cw_ref[2:3, :]
    r = u[:, 0:WIDTH]
    k = u[:, WIDTH:2 * WIDTH]
    v = u[:, 2 * WIDTH:3 * WIDTH]
    blk = u[:, 3 * WIDTH:RW_COLS]
    lane = lax.broadcasted_iota(jnp.int32, blk.shape, 1)
    act = jnp.where(lane < 2 * LORA, jnp.tanh(blk),
                    jnp.where(jnp.logical_and(lane >= 4 * LORA, lane < 4 * LORA + GATE_LORA), _sigmoid(blk), blk))
    lo = jnp.dot(act.astype(BF16), wl_ref[...], preferred_element_type=F32) + bl_ref[...]
    g = lo[:, 4 * WIDTH:5 * WIDTH]
    if has_vfirst:
        mix = _sigmoid(lo[:, 5 * WIDTH:6 * WIDTH])
        v = v + (vf_ref[...] - v) * mix
    ones = ones_ref[...]
    kk = k * kk_ref[...]
    ss = jnp.dot(kk * kk, ones, preferred_element_type=F32, precision=lax.Precision.HIGHEST)
    kk = kk * lax.rsqrt(jnp.maximum(ss, 1e-24))
    ksum = None
    for d in range(2):
        lw_out[d] = -DECAY_SCALE * _sigmoid(lo[:, d * WIDTH:(d + 1) * WIDTH])
        iclr = _sigmoid(lo[:, (2 + d) * WIDTH:(3 + d) * WIDTH])
        kd = k * (1.0 + (iclr - 1.0) * ka_ref[...])
        kd_out[d] = kd
        bd_out[d] = kk * iclr
        ksum = kd if ksum is None else ksum + kd
    coef = jnp.dot(r * ksum * rk_ref[...], ones, preferred_element_type=F32, precision=lax.Precision.HIGHEST)
    r_out[...] = r
    v_out[...] = v
    kk_out[...] = kk
    g_out[...] = g
    cv_out[...] = coef * v


def _rwkv_prep_call(rw, v_first, p, seg):
    n = rw.shape[0]
    hb = TILE // HALO
    nb = n // HALO
    row = lambda c: pl.BlockSpec((TILE, c), lambda j: (j, 0))
    dir_row = pl.BlockSpec((2, TILE, WIDTH), lambda j: (0, j, 0))
    has_vf = v_first is not None
    in_specs = [
        row(RW_COLS),
        pl.BlockSpec((HALO, RW_COLS), lambda j: (jnp.maximum(j * hb - 1, 0), 0)),
        pl.BlockSpec((HALO, RW_COLS), lambda j: (jnp.minimum((j + 1) * hb, nb - 1), 0)),
    ]
    args = [rw, rw, rw]
    if has_vf:
        in_specs.append(row(WIDTH))
        args.append(v_first)
    consts = [p["rw_conv"], p["rw_conv_b"], p["w_lora"], p["b_lora"], p["k_k"], p["k_a"], p["r_k"], p["ones_bd"]]
    in_specs += [_const_spec(a.shape) for a in consts]
    args += consts
    return pl.pallas_call(
        functools.partial(_rwkv_prep_kernel, seg, has_vf),
        grid=(n // TILE,),
        in_specs=in_specs,
        out_specs=[row(WIDTH)] * 5 + [dir_row] * 3,
        out_shape=[jax.ShapeDtypeStruct((n, WIDTH), F32)] * 5 + [jax.ShapeDtypeStruct((2, n, WIDTH), F32)] * 3,
        compiler_params=_params(1),
        name="rwkv_prep",
    )(*args)


def _bd_stack(x, bd_mask):
    return jnp.concatenate([x] * GROUP, axis=0) * bd_mask


def _mm(a, b):
    return jnp.dot(a.astype(BF16), b.astype(BF16), preferred_element_type=F32)


def _mm_nt(a, b):
    return lax.dot_general(a.astype(BF16), b.astype(BF16), (((1,), (1,)), ((), ())), preferred_element_type=F32)


def _mm_tn(a, b):
    return lax.dot_general(a.astype(BF16), b.astype(BF16), (((0,), (0,)), ((), ())), preferred_element_type=F32)


def _rwkv_scan_kernel(reverse, r_ref, v_ref, kk_ref, lw_ref, kd_ref, bd_ref, tri_ref, strict_ref, incl_ref,
                      bdm_ref, eye_ref, y_ref, h_ref):
    @pl.when(pl.program_id(0) == 0)
    def _():
        h_ref[...] = jnp.zeros_like(h_ref)

    n_chunks = TILE // CHUNK
    last = 0 if reverse else CHUNK - 1

    def chunk_body(ci, carry):
        c = (n_chunks - 1 - ci) if reverse else ci
        rows = pl.ds(pl.multiple_of(c * CHUNK, CHUNK), CHUNK)
        lw = lw_ref[0, rows, :]
        cl = jnp.dot(tri_ref[...], lw, preferred_element_type=F32, precision=lax.Precision.HIGHEST)
        tot = cl[last:last + 1, :]
        e_cl = jnp.exp(cl)
        e_cle = jnp.exp(cl - lw)
        e_ncl = jnp.exp(-cl)
        e_tc = jnp.exp(tot - cl)
        e_tot = jnp.exp(tot)
        a_t = -kk_ref[rows, :] * e_cle
        r_t = r_ref[rows, :] * e_cl
        kd = kd_ref[0, rows, :]
        bd = bd_ref[0, rows, :]
        b_t = bd * e_ncl
        k_t = kd * e_ncl
        b_h = bd * e_tc
        k_h = kd * e_tc
        v = v_ref[rows, :]
        bdm = bdm_ref[...]
        strict = strict_ref[...]
        incl = incl_ref[...]
        eye = eye_ref[...]
        ys = []
        for gi in range(HEADS // GROUP):
            ln = slice(gi * GW, (gi + 1) * GW)
            a_s = _bd_stack(a_t[:, ln], bdm)
            r_s = _bd_stack(r_t[:, ln], bdm)
            b_s = _bd_stack(b_t[:, ln], bdm)
            k_s = _bd_stack(k_t[:, ln], bdm)
            bh_s = _bd_stack(b_h[:, ln], bdm)
            kh_s = _bd_stack(k_h[:, ln], bdm)
            v_s = _bd_stack(v[:, ln], bdm)
            a_ab = _mm_nt(a_s, b_s) * strict
            a_ak = _mm_nt(a_s, k_s) * strict
            a_rb = _mm_nt(r_s, b_s) * incl
            a_rk = _mm_nt(r_s, k_s) * incl
            t_inv = eye + a_ab
            pw = a_ab
            for _ in range(5):
                pw = _mm(pw, pw)
                t_inv = t_inv + _mm(t_inv, pw)
            abar = _mm(t_inv, a_s)
            u0 = _mm(t_inv, _mm(a_ak, v_s))
            m = eye * e_tot[:, ln] + _mm_tn(bh_s, abar)
            g = _mm_tn(bh_s, u0) + _mm_tn(kh_s, v_s)
            rbar = r_s + _mm(a_rb, abar)
            y0 = _mm(a_rb, u0) + _mm(a_rk, v_s)
            h = h_ref[gi]
            y_bd = _mm(rbar, h) + y0
            h_ref[gi] = _mm(m, h) + g
            y = y_bd[0:CHUNK, :]
            for hh in range(1, GROUP):
                y = y + y_bd[hh * CHUNK:(hh + 1) * CHUNK, :]
            ys.append(y)
        y_ref[rows, :] = jnp.concatenate(ys, axis=1)
        return carry

    lax.fori_loop(0, n_chunks, chunk_body, 0)


def _scan_tile(reverse, n_lat_tiles):
    if reverse:
        return lambda j: jnp.where(j == 0, n_lat_tiles, n_lat_tiles - j)
    return lambda j: jnp.where(j == 0, n_lat_tiles, j - 1)


def _rwkv_scan_call(reverse, r, v, kk, lw, kd, bd, consts, n_lat_tiles):
    n = r.shape[0]
    d = 1 if reverse else 0
    tile = _scan_tile(reverse, n_lat_tiles)
    row = pl.BlockSpec((TILE, WIDTH), lambda j: (tile(j), 0))
    dir_row = pl.BlockSpec((1, TILE, WIDTH), lambda j: (d, tile(j), 0))
    tri = consts["tri_rev"] if reverse else consts["tri_fwd"]
    strict = consts["strict_rev"] if reverse else consts["strict_fwd"]
    incl = consts["incl_rev"] if reverse else consts["incl_fwd"]
    cs = [tri, strict, incl, consts["bd_mask"], consts["eye"]]
    return pl.pallas_call(
        functools.partial(_rwkv_scan_kernel, reverse),
        grid=(n // TILE,),
        in_specs=[row, row, row, dir_row, dir_row, dir_row] + [_const_spec(a.shape) for a in cs],
        out_specs=row,
        out_shape=jax.ShapeDtypeStruct((n, WIDTH), F32),
        scratch_shapes=[pltpu.VMEM((HEADS // GROUP, GW, GW), F32)],
        compiler_params=_params(1),
        name="rwkv_scan_rev" if reverse else "rwkv_scan_fwd",
    )(r, v, kk, lw, kd, bd, *cs)


def _lru_kernel(reverse, seg, cur_ref, prev_ref, next_ref, cw_ref, cb_ref, wg_ref, bg_ref, lam_ref,
                hs_ref, a_scr, b_scr, h_scr):
    j = pl.program_id(0)

    @pl.when(j == 0)
    def _():
        h_scr[...] = jnp.zeros_like(h_scr)

    tile_idx = cur_tile_index(reverse, seg, j)
    lvalid, rvalid = _halo_valid(tile_idx, TILE, seg[0], seg[1])
    cur = cur_ref[...]
    prev = prev_ref[...] * lvalid
    x2 = _shift_down(cur, prev[HALO - 2:HALO, :], 2)
    x1 = _shift_down(cur, prev[HALO - 1:HALO, :], 1)
    xn = _shift_up(cur, next_ref[0:1, :] * rvalid)
    xb = cb_ref[...] + x2 * cw_ref[0:1, :] + x1 * cw_ref[1:2, :] + cur * cw_ref[2:3, :] + xn * cw_ref[3:4, :]
    gates = jnp.dot(xb.astype(BF16), wg_ref[0], preferred_element_type=F32) + bg_ref[0]
    gate_r = _sigmoid(gates[:, 0:WIDTH])
    gate_i = _sigmoid(gates[:, WIDTH:2 * WIDTH])
    lam = lam_ref[0]
    softplus = jnp.maximum(-lam, 0.0) + jnp.log(1.0 + jnp.exp(-jnp.abs(lam)))
    log_a = -LRU_C * gate_r * softplus
    a = jnp.exp(log_a)
    a_scr[...] = a
    b_scr[...] = jnp.sqrt(1.0 - a * a) * gate_i * xb

    n_groups = TILE // 8
    rows8 = _row_iota((8, WIDTH))

    def group_body(gi, h):
        g = (n_groups - 1 - gi) if reverse else gi
        rows = pl.ds(pl.multiple_of(g * 8, 8), 8)
        a8 = a_scr[rows, :]
        b8 = b_scr[rows, :]
        for s in (1, 2, 4):
            if reverse:
                a_sh = pltpu.roll(a8, 8 - s, 0)
                b_sh = pltpu.roll(b8, 8 - s, 0)
                ok = rows8 < 8 - s
            else:
                a_sh = pltpu.roll(a8, s, 0)
                b_sh = pltpu.roll(b8, s, 0)
                ok = rows8 >= s
            b8 = jnp.where(ok, a8 * b_sh + b8, b8)
            a8 = jnp.where(ok, a8 * a_sh, a8)
        hs = a8 * h + b8
        hs_ref[rows, :] = hs
        return hs[0:1, :] if reverse else hs[7:8, :]

    h_scr[...] = lax.fori_loop(0, n_groups, group_body, h_scr[...])


def cur_tile_index(reverse, seg, j):
    n_lat_tiles = seg[1][0] // TILE
    return _scan_tile(reverse, n_lat_tiles)(j)


def _lru_call(reverse, lru, p, seg):
    n = lru.shape[0]
    d = 1 if reverse else 0
    hb = TILE // HALO
    nb = n // HALO
    n_lat_tiles = seg[1][0] // TILE
    tile = _scan_tile(reverse, n_lat_tiles)
    dsel = lambda shape: pl.BlockSpec((1,) + shape, lambda j: (d, 0, 0), pipeline_mode=pl.Buffered(1))
    return pl.pallas_call(
        functools.partial(_lru_kernel, reverse, seg),
        grid=(n // TILE,),
        in_specs=[
            pl.BlockSpec((TILE, WIDTH), lambda j: (tile(j), 0)),
            pl.BlockSpec((HALO, WIDTH), lambda j: (jnp.maximum(tile(j) * hb - 1, 0), 0)),
            pl.BlockSpec((HALO, WIDTH), lambda j: (jnp.minimum((tile(j) + 1) * hb, nb - 1), 0)),
            _const_spec(p["lru_conv"].shape),
            _const_spec(p["lru_conv_b"].shape),
            dsel((WIDTH, 2 * WIDTH)),
            dsel((1, 2 * WIDTH)),
            dsel((1, WIDTH)),
        ],
        out_specs=pl.BlockSpec((TILE, WIDTH), lambda j: (tile(j), 0)),
        out_shape=jax.ShapeDtypeStruct((n, WIDTH), F32),
        scratch_shapes=[pltpu.VMEM((TILE, WIDTH), F32), pltpu.VMEM((TILE, WIDTH), F32), pltpu.VMEM((1, WIDTH), F32)],
        compiler_params=_params(1),
        name="lru_rev" if reverse else "lru_fwd",
    )(lru, lru, lru, p["lru_conv"], p["lru_conv_b"], p["lru_wg"], p["lru_bg"], p["lru_lam"])


def _mla_proj_kernel(cols_ref, cc_ref, ss_ref, cs_ref, gq_ref, gkv_ref, wq_ref, wkv_ref, q_ref, k_ref, v_ref):
    cols = cols_ref[...]
    qn = _rms(cols[:, 0:Q_RANK], gq_ref[...]).astype(BF16)
    q = jnp.dot(qn, wq_ref[...], preferred_element_type=F32)
    kvn = _rms(cols[:, Q_RANK:Q_RANK + KV_RANK], gkv_ref[...]).astype(BF16)
    kv = jnp.dot(kvn, wkv_ref[...], preferred_element_type=F32)
    kr = cols[:, Q_RANK + KV_RANK:MLA_COLS] * cs_ref[...]
    kr = (kr + pltpu.roll(kr, ROPE, 1)).astype(BF16)
    lane = lax.broadcasted_iota(jnp.int32, (TILE, 2 * ROPE), 1)
    cc = cc_ref[...]
    ss = ss_ref[...]
    for hp in range(HEADS // 2):
        sl = slice(HEADS * NOPE + hp * 2 * ROPE, HEADS * NOPE + (hp + 1) * 2 * ROPE)
        sw = slice(HEADS * NOPE + HEADS * ROPE + hp * 2 * ROPE, HEADS * NOPE + HEADS * ROPE + (hp + 1) * 2 * ROPE)
        roped = (q[:, sl] * cc + q[:, sw] * ss) * MLA_SCALE
        for e in range(2):
            h = 2 * hp + e
            q_ref[h, :, 0:NOPE] = (q[:, h * NOPE:(h + 1) * NOPE] * MLA_SCALE).astype(BF16)
            keep = (lane < ROPE) if e == 0 else (lane >= ROPE)
            q_ref[h, :, NOPE:QK_DIM] = jnp.where(keep, roped, 0.0).astype(BF16)
            k_ref[h, :, 0:NOPE] = kv[:, h * 2 * NOPE:h * 2 * NOPE + NOPE].astype(BF16)
            k_ref[h, :, NOPE:QK_DIM] = kr
            v_ref[h] = kv[:, h * 2 * NOPE + NOPE:(h + 1) * 2 * NOPE].astype(BF16)


def _mla_proj_call(mla, tabs, p):
    n = mla.shape[0]
    row = lambda c: pl.BlockSpec((TILE, c), lambda j: (j, 0))
    head = lambda c: pl.BlockSpec((HEADS, TILE, c), lambda j: (0, j, 0))
    consts = [p["q_norm"], p["kv_norm"], p["w_q"], p["w_kv"]]
    return pl.pallas_call(
        _mla_proj_kernel,
        grid=(n // TILE,),
        in_specs=[row(MLA_COLS), row(2 * ROPE), row(2 * ROPE), row(2 * ROPE)] + [_const_spec(a.shape) for a in consts],
        out_specs=[head(QK_DIM), head(QK_DIM), head(V_DIM)],
        out_shape=[jax.ShapeDtypeStruct((HEADS, n, QK_DIM), BF16), jax.ShapeDtypeStruct((HEADS, n, QK_DIM), BF16),
                   jax.ShapeDtypeStruct((HEADS, n, V_DIM), BF16)],
        compiler_params=_params(1),
        name="mla_proj",
    )(mla, tabs["cc"], tabs["ss"], tabs["cs"], *consts)


def _attn_kernel(kv_start, kv_chunk, n_kv, *refs):
    q_ref, k_ref, v_ref = refs[0:3]
    o_ref, m_scr, l_scr, acc_scr = refs[-4:]
    m_scr[...] = jnp.full(m_scr.shape, -jnp.inf, F32)
    l_scr[...] = jnp.zeros_like(l_scr)
    acc_scr[...] = jnp.zeros_like(acc_scr)
    q = q_ref[0]

    def body(ci, carry):
        rows = pl.ds(pl.multiple_of(kv_start + ci * kv_chunk, kv_chunk), kv_chunk)
        s = lax.dot_general(q, k_ref[0, rows, :], (((1,), (1,)), ((), ())), preferred_element_type=F32)
        m_old = m_scr[...]
        m_new = jnp.maximum(m_old, jnp.max(s, axis=-1, keepdims=True))
        alpha = jnp.exp(m_old - m_new)
        pr = jnp.exp(s - m_new)
        l_scr[...] = alpha * l_scr[...] + jnp.sum(pr, axis=-1, keepdims=True)
        acc_scr[...] = alpha * acc_scr[...] + jnp.dot(pr.astype(BF16), v_ref[0, rows, :], preferred_element_type=F32)
        m_scr[...] = m_new
        return carry

    lax.fori_loop(0, n_kv, body, 0)
    o_ref[...] = (acc_scr[...] / l_scr[...]).astype(o_ref.dtype)


def _attn_call(q, k, v, o_prev, q_tile, q_block0, n_q, kv_start, kv_chunk, n_kv):
    n = q.shape[1]
    in_specs = [
        pl.BlockSpec((1, q_tile, QK_DIM), lambda h, i: (h, q_block0 + i, 0)),
        pl.BlockSpec((1, n, QK_DIM), lambda h, i: (h, 0, 0)),
        pl.BlockSpec((1, n, V_DIM), lambda h, i: (h, 0, 0)),
    ]
    args = [q, k, v]
    aliases = {}
    if o_prev is not None:
        in_specs.append(pl.BlockSpec(memory_space=pl.ANY))
        args.append(o_prev)
        aliases = {3: 0}
    return pl.pallas_call(
        functools.partial(_attn_kernel, kv_start, kv_chunk, n_kv),
        grid=(HEADS, n_q),
        in_specs=in_specs,
        out_specs=pl.BlockSpec((q_tile, V_DIM), lambda h, i: (q_block0 + i, h)),
        out_shape=jax.ShapeDtypeStruct((n, MLA_WIDTH), BF16),
        scratch_shapes=[pltpu.VMEM((q_tile, 1), F32), pltpu.VMEM((q_tile, 1), F32), pltpu.VMEM((q_tile, V_DIM), F32)],
        input_output_aliases=aliases,
        compiler_params=_params(2),
        name="mla_attn",
    )(*args)


def _out_proj_kernel(x_ref, mod_ref, yf_ref, yb_ref, cv_ref, g_ref, hf_ref, hb_ref, gate_ref, om_ref,
                     lng_ref, lnb_ref, ones_ref, w_ref, o_ref):
    ones = ones_ref[...]
    hi = lax.Precision.HIGHEST
    y = yf_ref[...] + yb_ref[...]
    mu = jnp.dot(y, ones, preferred_element_type=F32, precision=hi) * (1.0 / HEAD_DIM)
    dlt = y - mu
    var = jnp.dot(dlt * dlt, ones, preferred_element_type=F32, precision=hi) * (1.0 / HEAD_DIM)
    yn = dlt * lax.rsqrt(var + GN_EPS) * lng_ref[...] + lnb_ref[...]
    o_rw = ((yn + cv_ref[...]) * g_ref[...]).astype(BF16)
    o_lru = ((hf_ref[...] + hb_ref[...]) * _gelu_tanh(gate_ref[...])).astype(BF16)
    acc = jnp.dot(o_rw, w_ref[0:WIDTH, :], preferred_element_type=F32)
    acc += jnp.dot(o_lru, w_ref[WIDTH:2 * WIDTH, :], preferred_element_type=F32)
    acc += jnp.dot(om_ref[...], w_ref[2 * WIDTH:, :], preferred_element_type=F32)
    o_ref[...] = x_ref[...] + mod_ref[0, 2:3, :] * acc


def _out_proj_call(x_all, mod, yf, yb, cv, g, hf, hb, lru, o_mla, p, n_tiles, n_lat_tiles):
    n = x_all.shape[0]
    row = lambda c: pl.BlockSpec((TILE, c), lambda j: (j, 0))
    consts = [p["ln_g"], p["ln_b"], p["ones_bd"], p["w_out"]]
    return pl.pallas_call(
        _out_proj_kernel,
        grid=(n_tiles,),
        in_specs=[row(D_MODEL), pl.BlockSpec((1, 6, D_MODEL), lambda j: (jnp.minimum(j // n_lat_tiles, 1), 0, 0))]
                 + [row(WIDTH)] * 6 + [pl.BlockSpec((TILE, WIDTH), lambda j: (j, 1)), row(MLA_WIDTH)]
                 + [_const_spec(a.shape) for a in consts],
        out_specs=row(D_MODEL),
        out_shape=jax.ShapeDtypeStruct((n, D_MODEL), F32),
        compiler_params=_params(1),
        name="out_proj",
    )(x_all, mod, yf, yb, cv, g, hf, hb, lru, o_mla, *consts)


def _ffn_kernel(tm, block0, seg_lo, seg_hi, mod_row, final, *refs):
    if final:
        (x_ref, prev_ref, next_ref, mod_ref, g_ref, wg_ref, wu_ref, wd_ref, cw_ref, cb_ref, fin_ref,
         o_ref, h_scr, acc_scr) = refs
    else:
        (x_ref, prev_ref, next_ref, mod_ref, g_ref, wg_ref, wu_ref, wd_ref, cw_ref, cb_ref,
         o_ref, h_scr, acc_scr) = refs
    c = pl.program_id(1)
    ext = tm + 2 * FFN_HALO

    @pl.when(c == 0)
    def _():
        sh = mod_ref[mod_row, 3:4, :]
        sc = 1.0 + mod_ref[mod_row, 4:5, :]
        g = g_ref[...]
        h_scr[0:FFN_HALO, :] = (_rms(prev_ref[...], g) * sc + sh).astype(BF16)
        h_scr[FFN_HALO:FFN_HALO + tm, :] = (_rms(x_ref[...], g) * sc + sh).astype(BF16)
        h_scr[FFN_HALO + tm:ext, :] = (_rms(next_ref[...], g) * sc + sh).astype(BF16)
        acc_scr[...] = jnp.zeros_like(acc_scr)

    ge = jnp.dot(h_scr[...], wg_ref[...], preferred_element_type=F32)
    grow = (block0 + pl.program_id(0)) * tm - FFN_HALO + _row_iota(ge.shape)
    ge = jnp.where(jnp.logical_and(grow >= seg_lo, grow < seg_hi), ge, 0.0)
    up_rows = pltpu.roll(ge, 1, 0)[FFN_HALO:FFN_HALO + tm, :]
    dn_rows = pltpu.roll(ge, ext - 1, 0)[FFN_HALO:FFN_HALO + tm, :]
    gate = cb_ref[...] + up_rows * cw_ref[0:1, :] + ge[FFN_HALO:FFN_HALO + tm, :] * cw_ref[1:2, :] + dn_rows * cw_ref[2:3, :]
    up = jnp.dot(h_scr[FFN_HALO:FFN_HALO + tm, :], wu_ref[...], preferred_element_type=F32)
    act = (_silu(gate) * up).astype(BF16)
    acc_scr[...] += jnp.dot(act, wd_ref[...], preferred_element_type=F32)

    @pl.when(c == pl.num_programs(1) - 1)
    def _():
        out = x_ref[...] + mod_ref[mod_row, 5:6, :] * acc_scr[...]
        if final:
            out = _rms(out, fin_ref[...])
        o_ref[...] = out


def _ffn_call(x_all, o_prev, mod, p, tm, block0, n_blocks, seg_lo, seg_hi, mod_row, final_g, out_rows):
    n = x_all.shape[0]
    hb = tm // FFN_HALO
    nb = n // FFN_HALO
    n_chunks = D_FF // FFN_CHUNK
    final = final_g is not None
    in_specs = [
        pl.BlockSpec((tm, D_MODEL), lambda j, c: (block0 + j, 0)),
        pl.BlockSpec((FFN_HALO, D_MODEL), lambda j, c: (jnp.maximum((block0 + j) * hb - 1, 0), 0)),
        pl.BlockSpec((FFN_HALO, D_MODEL), lambda j, c: (jnp.minimum((block0 + j + 1) * hb, nb - 1), 0)),
        pl.BlockSpec((2, 6, D_MODEL), lambda j, c: (0, 0, 0)),
        pl.BlockSpec((1, D_MODEL), lambda j, c: (0, 0)),
        pl.BlockSpec((D_MODEL, FFN_CHUNK), lambda j, c: (0, c)),
        pl.BlockSpec((D_MODEL, FFN_CHUNK), lambda j, c: (0, c)),
        pl.BlockSpec((FFN_CHUNK, D_MODEL), lambda j, c: (c, 0)),
        pl.BlockSpec((3, FFN_CHUNK), lambda j, c: (0, c)),
        pl.BlockSpec((1, FFN_CHUNK), lambda j, c: (0, c)),
    ]
    args = [x_all, x_all, x_all, mod, p["norm2"], p["w_gate"], p["w_up"], p["w_down"], p["ffn_conv"], p["ffn_conv_b"]]
    if final:
        in_specs.append(pl.BlockSpec((1, D_MODEL), lambda j, c: (0, 0)))
        args.append(final_g)
    aliases = {}
    if o_prev is not None:
        in_specs.append(pl.BlockSpec(memory_space=pl.ANY))
        args.append(o_prev)
        aliases = {len(args) - 1: 0}
    kern = functools.partial(_ffn_kernel, tm, block0, seg_lo, seg_hi, mod_row, final)
    if o_prev is not None:
        kern = _drop_ref(kern, len(args) - 1)
    return pl.pallas_call(
        kern,
        grid=(n_blocks, n_chunks),
        in_specs=in_specs,
        out_specs=pl.BlockSpec((tm, D_MODEL), lambda j, c: (block0 + j, 0)),
        out_shape=jax.ShapeDtypeStruct((out_rows, D_MODEL), F32),
        scratch_shapes=[pltpu.VMEM((tm + 2 * FFN_HALO, D_MODEL), BF16), pltpu.VMEM((tm, D_MODEL), F32)],
        input_output_aliases=aliases,
        compiler_params=_params(2),
        name="conv_ffn",
    )(*args)


def _drop_ref(kern, idx):
    def wrapped(*refs):
        return kern(*(refs[:idx] + refs[idx + 1:]))
    return wrapped


_ROPE_PERM = np.concatenate([np.arange(16, 32), np.arange(0, 16), np.arange(48, 64), np.arange(32, 48)])


def _block_diag(blocks):
    h, n, m = blocks.shape
    eye = jnp.eye(h, dtype=blocks.dtype)
    return (eye[:, None, :, None] * blocks[:, :, None, :]).reshape(h * n, h * m)


def _scan_consts():
    idx = np.arange(CHUNK)
    lower = (idx[None, :] <= idx[:, None]).astype(np.float32)
    upper = (idx[None, :] >= idx[:, None]).astype(np.float32)
    eye_g = np.eye(GROUP, dtype=np.float32)
    bd = lambda m: np.kron(eye_g, m)
    return {
        "tri_fwd": jnp.asarray(lower), "tri_rev": jnp.asarray(upper),
        "incl_fwd": jnp.asarray(bd(lower)), "incl_rev": jnp.asarray(bd(upper)),
        "strict_fwd": jnp.asarray(bd(lower - np.eye(CHUNK, dtype=np.float32))),
        "strict_rev": jnp.asarray(bd(upper - np.eye(CHUNK, dtype=np.float32))),
        "bd_mask": jnp.asarray(bd(np.ones((CHUNK, HEAD_DIM), np.float32))),
        "eye": jnp.asarray(np.eye(GW, dtype=np.float32)),
    }


def _rope_tables(n_lat, n_ctx):
    n_freq = ROPE // 4
    rows = n_lat // GRID_W
    row = jnp.repeat(jnp.arange(rows, dtype=F32), GRID_W)
    col = jnp.tile(jnp.arange(GRID_W, dtype=F32), rows)
    inv_freq = ROPE_THETA ** (-jnp.arange(n_freq, dtype=F32) / n_freq)
    ar, ac = row[:, None] * inv_freq, col[:, None] * inv_freq
    cos = jnp.concatenate([jnp.cos(ar), jnp.cos(ar), jnp.cos(ac), jnp.cos(ac)], axis=-1)
    sin = jnp.concatenate([-jnp.sin(ar), jnp.sin(ar), -jnp.sin(ac), jnp.sin(ac)], axis=-1)
    cos = jnp.concatenate([cos, jnp.ones((n_ctx, ROPE), F32)], axis=0)
    sin = jnp.concatenate([sin, jnp.zeros((n_ctx, ROPE), F32)], axis=0)
    return {"cc": jnp.concatenate([cos, cos], axis=-1), "ss": jnp.concatenate([sin, sin], axis=-1),
            "cs": jnp.concatenate([cos, sin], axis=-1)}


def kernel(x, c, ctx, c_ctx, ada_w, ada_b, norm1, norm2, w_in, w_out, rw_conv, rw_conv_b, rw_w0, rw_w_up, rw_a0, rw_a_up, rw_g_up, rw_k_k, rw_k_a, rw_r_k, rw_ln_g, rw_ln_b, rw_v0, rw_v_down, rw_v_up, lru_conv, lru_conv_b, lru_wa, lru_ba, lru_wx, lru_bx, lru_lambda, mla_q_norm, mla_w_qb, mla_kv_norm, mla_w_kvb, ffn_w_gate, ffn_w_up, ffn_conv, ffn_conv_b, ffn_w_down, final_norm):
    assert x.shape[0] == 1 and ctx.shape[0] == 1
    depth = ada_w.shape[0]
    n_lat, n_ctx = x.shape[1], ctx.shape[1]
    assert n_lat % FFN_TILE == 0 and n_ctx % TILE == 0 and n_lat % GRID_W == 0
    n = n_lat + n_ctx
    n_lat_tiles = n_lat // TILE
    seg = ((0, n_lat), (n_lat, n))

    x_all = jnp.concatenate([x[0], ctx[0]], axis=0)
    mods = _mod_call(jnp.stack([c[0], c_ctx], axis=1), ada_w, ada_b).reshape(depth, 2, 6, D_MODEL)
    tabs = _rope_tables(n_lat, n_ctx)
    consts = _scan_consts()
    ones_bd = jnp.asarray(np.kron(np.eye(HEADS, dtype=np.float32), np.ones((HEAD_DIM, HEAD_DIM), np.float32)))
    kv_chunk = 1280 if n % 1280 == 0 else TILE
    row2 = lambda a: a.reshape(1, -1)

    v_first = None
    out = None
    for i in range(depth):
        last = i == depth - 1
        mla_off = 1760 + LRU_COLS
        rope_cols = mla_off + Q_RANK + KV_RANK + _ROPE_PERM
        vdown = rw_v_down[i - 1] if i > 0 else jnp.zeros((D_MODEL, LORA), F32)
        w_cat = jnp.concatenate([w_in[i][:, :1760], vdown, w_in[i][:, 1760:], w_in[i][:, rope_cols]], axis=1).astype(BF16)

        w_lora = jnp.zeros((RW_COLS - 3 * WIDTH, 6 * WIDTH), F32)
        b_lora = jnp.zeros((6 * WIDTH,), F32)
        for d in range(2):
            w_lora = w_lora.at[d * LORA:(d + 1) * LORA, d * WIDTH:(d + 1) * WIDTH].set(rw_w_up[i][d])
            w_lora = w_lora.at[(2 + d) * LORA:(3 + d) * LORA, (2 + d) * WIDTH:(3 + d) * WIDTH].set(rw_a_up[i][d])
            b_lora = b_lora.at[d * WIDTH:(d + 1) * WIDTH].set(rw_w0[i][d])
            b_lora = b_lora.at[(2 + d) * WIDTH:(3 + d) * WIDTH].set(rw_a0[i][d])
        w_lora = w_lora.at[4 * LORA:4 * LORA + GATE_LORA, 4 * WIDTH:5 * WIDTH].set(rw_g_up[i])
        if i > 0:
            w_lora = w_lora.at[4 * LORA + GATE_LORA:, 5 * WIDTH:].set(rw_v_up[i - 1])
            b_lora = b_lora.at[5 * WIDTH:].set(rw_v0[i - 1])
        ident = jnp.array([[0.0], [1.0], [0.0]], F32) * jnp.ones((1, LORA), F32)
        rw_p = {
            "rw_conv": jnp.concatenate([rw_conv[i], ident], axis=1),
            "rw_conv_b": row2(jnp.concatenate([rw_conv_b[i], jnp.zeros((LORA,), F32)])),
            "w_lora": w_lora.astype(BF16), "b_lora": row2(b_lora),
            "k_k": row2(rw_k_k[i]), "k_a": row2(rw_k_a[i]), "r_k": row2(rw_r_k[i]), "ones_bd": ones_bd,
        }
        lru_p = {
            "lru_conv": lru_conv[i], "lru_conv_b": row2(lru_conv_b[i]),
            "lru_wg": jnp.stack([jnp.concatenate([_block_diag(lru_wa[i][d]), _block_diag(lru_wx[i][d])], axis=1)
                                 for d in range(2)]).astype(BF16),
            "lru_bg": jnp.stack([jnp.concatenate([lru_ba[i][d], lru_bx[i][d]])[None] for d in range(2)]),
            "lru_lam": lru_lambda[i][:, None, :],
        }
        wq = mla_w_qb[i].reshape(Q_RANK, HEADS, NOPE + ROPE)
        mla_p = {
            "q_norm": row2(mla_q_norm[i]), "kv_norm": row2(mla_kv_norm[i]),
            "w_q": jnp.concatenate([wq[:, :, :NOPE].reshape(Q_RANK, -1), wq[:, :, NOPE:].reshape(Q_RANK, -1),
                                    wq[:, :, NOPE + _ROPE_PERM].reshape(Q_RANK, -1)], axis=1).astype(BF16),
            "w_kv": mla_w_kvb[i].astype(BF16),
        }
        out_p = {"ln_g": row2(rw_ln_g[i]), "ln_b": row2(rw_ln_b[i]), "ones_bd": ones_bd, "w_out": w_out[i].astype(BF16)}
        ffn_p = {"norm2": row2(norm2[i]), "w_gate": ffn_w_gate[i].astype(BF16), "w_up": ffn_w_up[i].astype(BF16),
                 "w_down": ffn_w_down[i].astype(BF16), "ffn_conv": ffn_conv[i], "ffn_conv_b": row2(ffn_conv_b[i])}

        rw, lru, mla = _in_proj_call(x_all, mods[i], row2(norm1[i]), w_cat, n_lat_tiles)

        r, v, kk, g, cv, lw, kd, bd = _rwkv_prep_call(rw, v_first, rw_p, seg)
        if i == 0:
            v_first = v
        y_f = _rwkv_scan_call(False, r, v, kk, lw, kd, bd, consts, n_lat_tiles)
        y_b = _rwkv_scan_call(True, r, v, kk, lw, kd, bd, consts, n_lat_tiles)

        h_f = _lru_call(False, lru, lru_p, seg)
        h_b = _lru_call(True, lru, lru_p, seg)

        q_h, k_h, v_h = _mla_proj_call(mla, tabs, mla_p)
        o_mla = _attn_call(q_h, k_h, v_h, None, Q_TILE, 0, n_lat // Q_TILE, 0, kv_chunk, n // kv_chunk)
        if not last:
            o_mla = _attn_call(q_h, k_h, v_h, o_mla, TILE, n_lat_tiles, n_ctx // TILE, n_lat, TILE, n_ctx // TILE)

        n_tiles = n_lat_tiles if last else n // TILE
        x_mid = _out_proj_call(x_all, mods[i], y_f, y_b, cv, g, h_f, h_b, lru, o_mla, out_p, n_tiles, n_lat_tiles)

        if last:
            out = _ffn_call(x_mid, None, mods[i], ffn_p, FFN_TILE, 0, n_lat // FFN_TILE, 0, n_lat, 0,
                            row2(final_norm), n_lat)
        else:
            x_new = _ffn_call(x_mid, None, mods[i], ffn_p, FFN_TILE, 0, n_lat // FFN_TILE, 0, n_lat, 0, None, n)
            x_all = _ffn_call(x_mid, x_new, mods[i], ffn_p, TILE, n_lat_tiles, n_ctx // TILE, n_lat, n, 1, None, n)
    return out[None]
```

```python
import functools

import numpy as np
import jax
import jax.numpy as jnp
from jax import lax
from jax.experimental import pallas as pl
from jax.experimental.pallas import tpu as pltpu

F32 = jnp.float32
BF16 = jnp.bfloat16

D_MODEL = 2048
NORM_EPS = 1e-6
GN_EPS = 64e-5
DECAY_SCALE = 0.606531
LRU_C = 8.0
HEADS = 8
HEAD_DIM = 64
WIDTH = HEADS * HEAD_DIM
LORA = 32
GATE_LORA = 96
RW_COLS = 1792
LRU_COLS = 1024
MLA_COLS = 896
Q_RANK = 512
KV_RANK = 256
NOPE = 128
ROPE = 64
V_DIM = 128
QK_DIM = 256
MLA_WIDTH = HEADS * V_DIM
MLA_SCALE = (NOPE + ROPE) ** -0.5
Q_SCALE = MLA_SCALE * 1.4426950408889634
ROPE_THETA = 10000.0
GRID_W = 64
D_FF = 5632

TILE = 256
CHUNK = 64
GROUP = 4
GW = GROUP * HEAD_DIM
HALO = 8
FFN_TILE = 512
FFN_HALO = 16
FFN_CHUNK = 512
Q_TILE = 512
VMEM_LIMIT = 56 * 1024 * 1024


def _params(n_axes, vmem=VMEM_LIMIT):
    return pltpu.CompilerParams(dimension_semantics=("arbitrary",) * n_axes, vmem_limit_bytes=vmem)


def _const_spec(shape):
    nd = len(shape)
    return pl.BlockSpec(shape, lambda *_: (0,) * nd, pipeline_mode=pl.Buffered(1))


def _sigmoid(x):
    return 1.0 / (1.0 + jnp.exp(-x))


def _silu(x):
    return x * _sigmoid(x)


def _gelu_tanh(x):
    return 0.5 * x * (1.0 + jnp.tanh(0.7978845608028654 * (x + 0.044715 * (x * x * x))))


def _rms(x, g):
    return x * lax.rsqrt(jnp.mean(x * x, axis=-1, keepdims=True) + NORM_EPS) * g


def _row_iota(shape):
    return lax.broadcasted_iota(jnp.int32, shape, 0)


def _shift_down(cur, prev_rows, k):
    out = pltpu.roll(cur, k, 0)
    rows = _row_iota(cur.shape)
    for i in range(k):
        out = jnp.where(rows == i, prev_rows[i:i + 1, :], out)
    return out


def _shift_up(cur, next_row):
    n = cur.shape[0]
    out = pltpu.roll(cur, n - 1, 0)
    return jnp.where(_row_iota(cur.shape) == n - 1, next_row, out)


def _halo_valid(j, tile, seg_starts, seg_ends):
    first = j * tile
    last = first + tile
    lvalid = jnp.logical_and(first != seg_starts[0], first != seg_starts[1])
    rvalid = jnp.logical_and(last != seg_ends[0], last != seg_ends[1])
    return lvalid.astype(F32), rvalid.astype(F32)


def _mod_kernel(cc_ref, w_ref, b_ref, o_ref):
    s = _silu(cc_ref[...])
    w = w_ref[0]
    b = b_ref[0]
    o_ref[0, 0:1, :] = jnp.sum(s[:, 0:1] * w, axis=0, keepdims=True) + b
    o_ref[0, 1:2, :] = jnp.sum(s[:, 1:2] * w, axis=0, keepdims=True) + b


def _mod_call(cc, ada_w, ada_b):
    depth, d, n6 = ada_w.shape
    tn = 1024
    return pl.pallas_call(
        _mod_kernel,
        grid=(depth, n6 // tn),
        in_specs=[
            pl.BlockSpec((d, 2), lambda i, j: (0, 0)),
            pl.BlockSpec((1, d, tn), lambda i, j: (i, 0, j)),
            pl.BlockSpec((1, 1, tn), lambda i, j: (i, 0, j)),
        ],
        out_specs=pl.BlockSpec((1, 2, tn), lambda i, j: (i, 0, j)),
        out_shape=jax.ShapeDtypeStruct((depth, 2, n6), F32),
        compiler_params=_params(2),
        name="adaln_mod",
    )(cc, ada_w, ada_b.reshape(depth, 1, n6))


def _in_proj_kernel(x_ref, mod_ref, g_ref, w_ref, rw_ref, lru_ref, mla_ref):
    h = _rms(x_ref[...], g_ref[...]) * (1.0 + mod_ref[0, 1:2, :]) + mod_ref[0, 0:1, :]
    hb = h.astype(BF16)
    rw_ref[...] = jnp.dot(hb, w_ref[:, 0:RW_COLS], preferred_element_type=F32)
    lru_ref[...] = jnp.dot(hb, w_ref[:, RW_COLS:RW_COLS + LRU_COLS], preferred_element_type=F32)
    mla_ref[...] = jnp.dot(hb, w_ref[:, RW_COLS + LRU_COLS:], preferred_element_type=F32)


def _in_proj_call(x_all, mod, g, w_cat, n_lat_tiles):
    n = x_all.shape[0]
    cols = w_cat.shape[1]
    row = lambda c: pl.BlockSpec((TILE, c), lambda j: (j, 0))
    return pl.pallas_call(
        _in_proj_kernel,
        grid=(n // TILE,),
        in_specs=[
            row(D_MODEL),
            pl.BlockSpec((1, 6, D_MODEL), lambda j: (jnp.minimum(j // n_lat_tiles, 1), 0, 0)),
            _const_spec((1, D_MODEL)),
            _const_spec((D_MODEL, cols)),
        ],
        out_specs=[row(RW_COLS), row(LRU_COLS), row(MLA_COLS)],
        out_shape=[jax.ShapeDtypeStruct((n, c), F32) for c in (RW_COLS, LRU_COLS, MLA_COLS)],
        compiler_params=_params(1),
        name="in_proj",
    )(x_all, mod, g, w_cat)


def _rwkv_prep_kernel(seg, has_vfirst, *refs):
    if has_vfirst:
        (cur_ref, prev_ref, next_ref, vf_ref, cw_ref, cb_ref, wl_ref, bl_ref, kk_ref, ka_ref, rk_ref, ones_ref,
         r_out, v_out, kk_out, g_out, cv_out, lw_out, kd_out, bd_out) = refs
    else:
        (cur_ref, prev_ref, next_ref, cw_ref, cb_ref, wl_ref, bl_ref, kk_ref, ka_ref, rk_ref, ones_ref,
         r_out, v_out, kk_out, g_out, cv_out, lw_out, kd_out, bd_out) = refs
    lvalid, rvalid = _halo_valid(pl.program_id(0), TILE, seg[0], seg[1])
    cur = cur_ref[...]
    up = _shift_down(cur, prev_ref[HALO - 1:HALO, :] * lvalid, 1)
    dn = _shift_up(cur, next_ref[0:1, :] * rvalid)
    u = cb_ref[...] + up * cw_ref[0:1, :] + cur * cw_ref[1:2, :] + dn * cw_ref[2:3, :]
    r = u[:, 0:WIDTH]
    k = u[:, WIDTH:2 * WIDTH]
    v = u[:, 2 * WIDTH:3 * WIDTH]
    blk = u[:, 3 * WIDTH:RW_COLS]
    lane = lax.broadcasted_iota(jnp.int32, blk.shape, 1)
    act = jnp.where(lane < 2 * LORA, jnp.tanh(blk),
                    jnp.where(jnp.logical_and(lane >= 4 * LORA, lane < 4 * LORA + GATE_LORA), _sigmoid(blk), blk))
    lo = jnp.dot(act.astype(BF16), wl_ref[...], preferred_element_type=F32) + bl_ref[...]
    g = lo[:, 4 * WIDTH:5 * WIDTH]
    if has_vfirst:
        mix = _sigmoid(lo[:, 5 * WIDTH:6 * WIDTH])
        v = v + (vf_ref[...] - v) * mix
    ones = ones_ref[...]
    kk = k * kk_ref[...]
    ss = jnp.dot(kk * kk, ones, preferred_element_type=F32, precision=lax.Precision.HIGHEST)
    kk = kk * lax.rsqrt(jnp.maximum(ss, 1e-24))
    ksum = None
    for d in range(2):
        lw_out[d] = -DECAY_SCALE * _sigmoid(lo[:, d * WIDTH:(d + 1) * WIDTH])
        iclr = _sigmoid(lo[:, (2 + d) * WIDTH:(3 + d) * WIDTH])
        kd = k * (1.0 + (iclr - 1.0) * ka_ref[...])
        kd_out[d] = kd
        bd_out[d] = kk * iclr
        ksum = kd if ksum is None else ksum + kd
    coef = jnp.dot(r * ksum * rk_ref[...], ones, preferred_element_type=F32, precision=lax.Precision.HIGHEST)
    r_out[...] = r
    v_out[...] = v
    kk_out[...] = kk
    g_out[...] = g
    cv_out[...] = coef * v


def _rwkv_prep_call(rw, v_first, p, seg):
    n = rw.shape[0]
    hb = TILE // HALO
    nb = n // HALO
    row = lambda c: pl.BlockSpec((TILE, c), lambda j: (j, 0))
    dir_row = pl.BlockSpec((2, TILE, WIDTH), lambda j: (0, j, 0))
    has_vf = v_first is not None
    in_specs = [
        row(RW_COLS),
        pl.BlockSpec((HALO, RW_COLS), lambda j: (jnp.maximum(j * hb - 1, 0), 0)),
        pl.BlockSpec((HALO, RW_COLS), lambda j: (jnp.minimum((j + 1) * hb, nb - 1), 0)),
    ]
    args = [rw, rw, rw]
    if has_vf:
        in_specs.append(row(WIDTH))
        args.append(v_first)
    consts = [p["rw_conv"], p["rw_conv_b"], p["w_lora"], p["b_lora"], p["k_k"], p["k_a"], p["r_k"], p["ones_bd"]]
    in_specs += [_const_spec(a.shape) for a in consts]
    args += consts
    return pl.pallas_call(
        functools.partial(_rwkv_prep_kernel, seg, has_vf),
        grid=(n // TILE,),
        in_specs=in_specs,
        out_specs=[row(WIDTH)] * 5 + [dir_row] * 3,
        out_shape=[jax.ShapeDtypeStruct((n, WIDTH), F32)] * 5 + [jax.ShapeDtypeStruct((2, n, WIDTH), F32)] * 3,
        compiler_params=_params(1),
        name="rwkv_prep",
    )(*args)


def _bd_stack(x, bd_mask):
    return jnp.concatenate([x] * GROUP, axis=0) * bd_mask


def _mm(a, b):
    return jnp.dot(a.astype(BF16), b.astype(BF16), preferred_element_type=F32)


def _mm_nt(a, b):
    return lax.dot_general(a.astype(BF16), b.astype(BF16), (((1,), (1,)), ((), ())), preferred_element_type=F32)


def _mm_tn(a, b):
    return lax.dot_general(a.astype(BF16), b.astype(BF16), (((0,), (0,)), ((), ())), preferred_element_type=F32)


def _scan_chunk(reverse, rows, r_ref, v_ref, kk_ref, lw_ref, kd_ref, bd_ref, tri, strict, incl, bdm, eye, y_ref, h_ref):
    last = 0 if reverse else CHUNK - 1
    lw = lw_ref[0, rows, :]
    cl = jnp.dot(tri, lw, preferred_element_type=F32, precision=lax.Precision.HIGHEST)
    tot = cl[last:last + 1, :]
    e_cl = jnp.exp(cl)
    e_cle = jnp.exp(cl - lw)
    e_ncl = jnp.exp(-cl)
    e_tc = jnp.exp(tot - cl)
    e_tot = jnp.exp(tot)
    a_t = -kk_ref[rows, :] * e_cle
    r_t = r_ref[rows, :] * e_cl
    kd = kd_ref[0, rows, :]
    bd = bd_ref[0, rows, :]
    b_t = bd * e_ncl
    k_t = kd * e_ncl
    b_h = bd * e_tc
    k_h = kd * e_tc
    v = v_ref[rows, :]
    ys = []
    for gi in range(HEADS // GROUP):
        ln = slice(gi * GW, (gi + 1) * GW)
        r_s = _bd_stack(r_t[:, ln], bdm)
        a_s = _bd_stack(a_t[:, ln], bdm).astype(BF16)
        r_sb = r_s.astype(BF16)
        b_s = _bd_stack(b_t[:, ln], bdm).astype(BF16)
        k_s = _bd_stack(k_t[:, ln], bdm).astype(BF16)
        bh_s = _bd_stack(b_h[:, ln], bdm).astype(BF16)
        kh_s = _bd_stack(k_h[:, ln], bdm).astype(BF16)
        v_s = _bd_stack(v[:, ln], bdm).astype(BF16)
        a_ab = _mm_nt(a_s, b_s) * strict
        a_ak = _mm_nt(a_s, k_s) * strict
        a_rb = (_mm_nt(r_sb, b_s) * incl).astype(BF16)
        a_rk = _mm_nt(r_sb, k_s) * incl
        t_inv = eye + a_ab
        pw = a_ab
        for _ in range(5):
            pw = _mm(pw, pw)
            t_inv = t_inv + _mm(t_inv, pw)
        t_inv = t_inv.astype(BF16)
        abar = _mm(t_inv, a_s).astype(BF16)
        u0 = _mm(t_inv, _mm(a_ak, v_s)).astype(BF16)
        m = eye * e_tot[:, ln] + _mm_tn(bh_s, abar)
        g = _mm_tn(bh_s, u0) + _mm_tn(kh_s, v_s)
        rbar = r_s + _mm(a_rb, abar)
        y0 = _mm(a_rb, u0) + _mm(a_rk, v_s)
        h = h_ref[gi].astype(BF16)
        y_bd = _mm(rbar, h) + y0
        h_ref[gi] = _mm(m, h) + g
        y = y_bd[0:CHUNK, :]
        for hh in range(1, GROUP):
            y = y + y_bd[hh * CHUNK:(hh + 1) * CHUNK, :]
        ys.append(y)
    y_ref[rows, :] = jnp.concatenate(ys, axis=1)


def _rwkv_scan_kernel(rf_ref, vf_ref, kkf_ref, lwf_ref, kdf_ref, bdf_ref, rb_ref, vb_ref, kkb_ref, lwb_ref, kdb_ref, bdb_ref,
                      trif_ref, trib_ref, strictf_ref, strictb_ref, inclf_ref, inclb_ref, bdm_ref, eye_ref,
                      yf_ref, yb_ref, hf_ref, hb_ref):
    @pl.when(pl.program_id(0) == 0)
    def _():
        hf_ref[...] = jnp.zeros_like(hf_ref)
        hb_ref[...] = jnp.zeros_like(hb_ref)

    n_chunks = TILE // CHUNK
    bdm = bdm_ref[...]
    eye = eye_ref[...]

    def chunk_body(ci, carry):
        rows_f = pl.ds(pl.multiple_of(ci * CHUNK, CHUNK), CHUNK)
        rows_b = pl.ds(pl.multiple_of((n_chunks - 1 - ci) * CHUNK, CHUNK), CHUNK)
        _scan_chunk(False, rows_f, rf_ref, vf_ref, kkf_ref, lwf_ref, kdf_ref, bdf_ref,
                    trif_ref[...], strictf_ref[...], inclf_ref[...], bdm, eye, yf_ref, hf_ref)
        _scan_chunk(True, rows_b, rb_ref, vb_ref, kkb_ref, lwb_ref, kdb_ref, bdb_ref,
                    trib_ref[...], strictb_ref[...], inclb_ref[...], bdm, eye, yb_ref, hb_ref)
        return carry

    lax.fori_loop(0, n_chunks, chunk_body, 0)


def _scan_tile(reverse, n_lat_tiles):
    if reverse:
        return lambda j: jnp.where(j == 0, n_lat_tiles, n_lat_tiles - j)
    return lambda j: jnp.where(j == 0, n_lat_tiles, j - 1)


def _rwkv_scan_call(r, v, kk, lw, kd, bd, consts, n_lat_tiles):
    n = r.shape[0]
    specs = []
    for d, reverse in enumerate((False, True)):
        tile = _scan_tile(reverse, n_lat_tiles)
        row = pl.BlockSpec((TILE, WIDTH), lambda j, tile=tile: (tile(j), 0))
        dir_row = pl.BlockSpec((1, TILE, WIDTH), lambda j, tile=tile, d=d: (d, tile(j), 0))
        specs.append((row, dir_row))
    (row_f, dir_f), (row_b, dir_b) = specs
    cs = [consts["tri_fwd"], consts["tri_rev"], consts["strict_fwd"], consts["strict_rev"],
          consts["incl_fwd"], consts["incl_rev"], consts["bd_mask"], consts["eye"]]
    state = pltpu.VMEM((HEADS // GROUP, GW, GW), F32)
    return pl.pallas_call(
        _rwkv_scan_kernel,
        grid=(n // TILE,),
        in_specs=[row_f] * 3 + [dir_f] * 3 + [row_b] * 3 + [dir_b] * 3 + [_const_spec(a.shape) for a in cs],
        out_specs=[row_f, row_b],
        out_shape=[jax.ShapeDtypeStruct((n, WIDTH), F32)] * 2,
        scratch_shapes=[state, state],
        compiler_params=_params(1),
        name="rwkv_scan",
    )(r, v, kk, lw, kd, bd, r, v, kk, lw, kd, bd, *cs)


def _lru_kernel(reverse, seg, cur_ref, prev_ref, next_ref, cw_ref, cb_ref, wg_ref, bg_ref, lam_ref,
                hs_ref, a_scr, b_scr, h_scr):
    j = pl.program_id(0)

    @pl.when(j == 0)
    def _():
        h_scr[...] = jnp.zeros_like(h_scr)

    tile_idx = cur_tile_index(reverse, seg, j)
    lvalid, rvalid = _halo_valid(tile_idx, TILE, seg[0], seg[1])
    cur = cur_ref[...]
    prev = prev_ref[...] * lvalid
    x2 = _shift_down(cur, prev[HALO - 2:HALO, :], 2)
    x1 = _shift_down(cur, prev[HALO - 1:HALO, :], 1)
    xn = _shift_up(cur, next_ref[0:1, :] * rvalid)
    xb = cb_ref[...] + x2 * cw_ref[0:1, :] + x1 * cw_ref[1:2, :] + cur * cw_ref[2:3, :] + xn * cw_ref[3:4, :]
    gates = jnp.dot(xb.astype(BF16), wg_ref[0], preferred_element_type=F32) + bg_ref[0]
    gate_r = _sigmoid(gates[:, 0:WIDTH])
    gate_i = _sigmoid(gates[:, WIDTH:2 * WIDTH])
    lam = lam_ref[0]
    softplus = jnp.maximum(-lam, 0.0) + jnp.log(1.0 + jnp.exp(-jnp.abs(lam)))
    log_a = -LRU_C * gate_r * softplus
    a = jnp.exp(log_a)
    a_scr[...] = a
    b_scr[...] = jnp.sqrt(1.0 - a * a) * gate_i * xb

    n_groups = TILE // 8
    rows8 = _row_iota((8, WIDTH))

    def group_body(gi, h):
        g = (n_groups - 1 - gi) if reverse else gi
        rows = pl.ds(pl.multiple_of(g * 8, 8), 8)
        a8 = a_scr[rows, :]
        b8 = b_scr[rows, :]
        for s in (1, 2, 4):
            if reverse:
                a_sh = pltpu.roll(a8, 8 - s, 0)
                b_sh = pltpu.roll(b8, 8 - s, 0)
                ok = rows8 < 8 - s
            else:
                a_sh = pltpu.roll(a8, s, 0)
                b_sh = pltpu.roll(b8, s, 0)
                ok = rows8 >= s
            b8 = jnp.where(ok, a8 * b_sh + b8, b8)
            a8 = jnp.where(ok, a8 * a_sh, a8)
        hs = a8 * h + b8
        hs_ref[rows, :] = hs
        return hs[0:1, :] if reverse else hs[7:8, :]

    h_scr[...] = lax.fori_loop(0, n_groups, group_body, h_scr[...])


def cur_tile_index(reverse, seg, j):
    n_lat_tiles = seg[1][0] // TILE
    return _scan_tile(reverse, n_lat_tiles)(j)


def _lru_call(reverse, lru, p, seg):
    n = lru.shape[0]
    d = 1 if reverse else 0
    hb = TILE // HALO
    nb = n // HALO
    n_lat_tiles = seg[1][0] // TILE
    tile = _scan_tile(reverse, n_lat_tiles)
    dsel = lambda shape: pl.BlockSpec((1,) + shape, lambda j: (d, 0, 0), pipeline_mode=pl.Buffered(1))
    return pl.pallas_call(
        functools.partial(_lru_kernel, reverse, seg),
        grid=(n // TILE,),
        in_specs=[
            pl.BlockSpec((TILE, WIDTH), lambda j: (tile(j), 0)),
            pl.BlockSpec((HALO, WIDTH), lambda j: (jnp.maximum(tile(j) * hb - 1, 0), 0)),
            pl.BlockSpec((HALO, WIDTH), lambda j: (jnp.minimum((tile(j) + 1) * hb, nb - 1), 0)),
            _const_spec(p["lru_conv"].shape),
            _const_spec(p["lru_conv_b"].shape),
            dsel((WIDTH, 2 * WIDTH)),
            dsel((1, 2 * WIDTH)),
            dsel((1, WIDTH)),
        ],
        out_specs=pl.BlockSpec((TILE, WIDTH), lambda j: (tile(j), 0)),
        out_shape=jax.ShapeDtypeStruct((n, WIDTH), F32),
        scratch_shapes=[pltpu.VMEM((TILE, WIDTH), F32), pltpu.VMEM((TILE, WIDTH), F32), pltpu.VMEM((1, WIDTH), F32)],
        compiler_params=_params(1),
        name="lru_rev" if reverse else "lru_fwd",
    )(lru, lru, lru, p["lru_conv"], p["lru_conv_b"], p["lru_wg"], p["lru_bg"], p["lru_lam"])


def _mla_proj_kernel(cols_ref, cc_ref, ss_ref, cs_ref, gq_ref, gkv_ref, wq_ref, wkv_ref, q_ref, k_ref, v_ref):
    cols = cols_ref[...]
    qn = _rms(cols[:, 0:Q_RANK], gq_ref[...]).astype(BF16)
    q = jnp.dot(qn, wq_ref[...], preferred_element_type=F32)
    kvn = _rms(cols[:, Q_RANK:Q_RANK + KV_RANK], gkv_ref[...]).astype(BF16)
    kv = jnp.dot(kvn, wkv_ref[...], preferred_element_type=F32)
    kr = cols[:, Q_RANK + KV_RANK:MLA_COLS] * cs_ref[...]
    kr = (kr + pltpu.roll(kr, ROPE, 1)).astype(BF16)
    lane = lax.broadcasted_iota(jnp.int32, (TILE, 2 * ROPE), 1)
    cc = cc_ref[...]
    ss = ss_ref[...]
    for hp in range(HEADS // 2):
        sl = slice(HEADS * NOPE + hp * 2 * ROPE, HEADS * NOPE + (hp + 1) * 2 * ROPE)
        sw = slice(HEADS * NOPE + HEADS * ROPE + hp * 2 * ROPE, HEADS * NOPE + HEADS * ROPE + (hp + 1) * 2 * ROPE)
        roped = (q[:, sl] * cc + q[:, sw] * ss) * Q_SCALE
        for e in range(2):
            h = 2 * hp + e
            q_ref[h, :, 0:NOPE] = (q[:, h * NOPE:(h + 1) * NOPE] * Q_SCALE).astype(BF16)
            keep = (lane < ROPE) if e == 0 else (lane >= ROPE)
            q_ref[h, :, NOPE:QK_DIM] = jnp.where(keep, roped, 0.0).astype(BF16)
            k_ref[h, :, 0:NOPE] = kv[:, h * 2 * NOPE:h * 2 * NOPE + NOPE].astype(BF16)
            k_ref[h, :, NOPE:QK_DIM] = kr
            v_ref[h] = kv[:, h * 2 * NOPE + NOPE:(h + 1) * 2 * NOPE].astype(BF16)


def _mla_proj_call(mla, tabs, p):
    n = mla.shape[0]
    row = lambda c: pl.BlockSpec((TILE, c), lambda j: (j, 0))
    head = lambda c: pl.BlockSpec((HEADS, TILE, c), lambda j: (0, j, 0))
    consts = [p["q_norm"], p["kv_norm"], p["w_q"], p["w_kv"]]
    return pl.pallas_call(
        _mla_proj_kernel,
        grid=(n // TILE,),
        in_specs=[row(MLA_COLS), row(2 * ROPE), row(2 * ROPE), row(2 * ROPE)] + [_const_spec(a.shape) for a in consts],
        out_specs=[head(QK_DIM), head(QK_DIM), head(V_DIM)],
        out_shape=[jax.ShapeDtypeStruct((HEADS, n, QK_DIM), BF16), jax.ShapeDtypeStruct((HEADS, n, QK_DIM), BF16),
                   jax.ShapeDtypeStruct((HEADS, n, V_DIM), BF16)],
        compiler_params=_params(1),
        name="mla_proj",
    )(mla, tabs["cc"], tabs["ss"], tabs["cs"], *consts)


def _attn_kernel(kv_start, kv_chunk, n_kv, *refs):
    q_ref, k_ref, v_ref = refs[0:3]
    o_ref, s_a, s_b, m_scr, l_scr, acc_scr = refs[-6:]
    m_scr[...] = jnp.full(m_scr.shape, -jnp.inf, F32)
    l_scr[...] = jnp.zeros_like(l_scr)
    acc_scr[...] = jnp.zeros_like(acc_scr)
    q = q_ref[0]

    def chunk_rows(ci):
        return pl.ds(pl.multiple_of(kv_start + ci * kv_chunk, kv_chunk), kv_chunk)

    def scores(ci, s_ref):
        s_ref[...] = lax.dot_general(q, k_ref[0, chunk_rows(ci), :], (((1,), (1,)), ((), ())),
                                     preferred_element_type=F32)

    def softmax_pv(ci, s_ref):
        s = s_ref[...]
        m_old = m_scr[...]
        m_new = jnp.maximum(m_old, jnp.max(s, axis=-1, keepdims=True))
        alpha = jnp.exp2(m_old - m_new)
        pr = jnp.exp2(s - m_new)
        l_scr[...] = alpha * l_scr[...] + jnp.sum(pr, axis=-1, keepdims=True)
        acc_scr[...] = alpha * acc_scr[...] + jnp.dot(pr.astype(BF16), v_ref[0, chunk_rows(ci), :],
                                                      preferred_element_type=F32)
        m_scr[...] = m_new

    scores(0, s_a)
    n_pairs = (n_kv - 1) // 2

    def pair(t, carry):
        scores(2 * t + 1, s_b)
        softmax_pv(2 * t, s_a)
        scores(2 * t + 2, s_a)
        softmax_pv(2 * t + 1, s_b)
        return carry

    lax.fori_loop(0, n_pairs, pair, 0)
    if n_kv - 2 * n_pairs == 1:
        softmax_pv(n_kv - 1, s_a)
    else:
        scores(n_kv - 1, s_b)
        softmax_pv(n_kv - 2, s_a)
        softmax_pv(n_kv - 1, s_b)
    o_ref[...] = (acc_scr[...] / l_scr[...]).astype(o_ref.dtype)


def _attn_call(q, k, v, o_prev, q_tile, q_block0, n_q, kv_start, kv_chunk, n_kv):
    n = q.shape[1]
    in_specs = [
        pl.BlockSpec((1, q_tile, QK_DIM), lambda h, i: (h, q_block0 + i, 0)),
        pl.BlockSpec((1, n, QK_DIM), lambda h, i: (h, 0, 0)),
        pl.BlockSpec((1, n, V_DIM), lambda h, i: (h, 0, 0)),
    ]
    args = [q, k, v]
    aliases = {}
    if o_prev is not None:
        in_specs.append(pl.BlockSpec(memory_space=pl.ANY))
        args.append(o_prev)
        aliases = {3: 0}
    return pl.pallas_call(
        functools.partial(_attn_kernel, kv_start, kv_chunk, n_kv),
        grid=(HEADS, n_q),
        in_specs=in_specs,
        out_specs=pl.BlockSpec((q_tile, V_DIM), lambda h, i: (q_block0 + i, h)),
        out_shape=jax.ShapeDtypeStruct((n, MLA_WIDTH), BF16),
        scratch_shapes=[pltpu.VMEM((q_tile, kv_chunk), F32), pltpu.VMEM((q_tile, kv_chunk), F32),
                        pltpu.VMEM((q_tile, 1), F32), pltpu.VMEM((q_tile, 1), F32), pltpu.VMEM((q_tile, V_DIM), F32)],
        input_output_aliases=aliases,
        compiler_params=_params(2),
        name="mla_attn",
    )(*args)


def _out_proj_kernel(x_ref, mod_ref, yf_ref, yb_ref, cv_ref, g_ref, hf_ref, hb_ref, gate_ref, om_ref,
                     lng_ref, lnb_ref, ones_ref, w_ref, o_ref):
    ones = ones_ref[...]
    hi = lax.Precision.HIGHEST
    y = yf_ref[...] + yb_ref[...]
    mu = jnp.dot(y, ones, preferred_element_type=F32, precision=hi) * (1.0 / HEAD_DIM)
    dlt = y - mu
    var = jnp.dot(dlt * dlt, ones, preferred_element_type=F32, precision=hi) * (1.0 / HEAD_DIM)
    yn = dlt * lax.rsqrt(var + GN_EPS) * lng_ref[...] + lnb_ref[...]
    o_rw = ((yn + cv_ref[...]) * g_ref[...]).astype(BF16)
    o_lru = ((hf_ref[...] + hb_ref[...]) * _gelu_tanh(gate_ref[...])).astype(BF16)
    acc = jnp.dot(o_rw, w_ref[0:WIDTH, :], preferred_element_type=F32)
    acc += jnp.dot(o_lru, w_ref[WIDTH:2 * WIDTH, :], preferred_element_type=F32)
    acc += jnp.dot(om_ref[...], w_ref[2 * WIDTH:, :], preferred_element_type=F32)
    o_ref[...] = x_ref[...] + mod_ref[0, 2:3, :] * acc


def _out_proj_call(x_all, mod, yf, yb, cv, g, hf, hb, lru, o_mla, p, n_tiles, n_lat_tiles):
    n = x_all.shape[0]
    row = lambda c: pl.BlockSpec((TILE, c), lambda j: (j, 0))
    consts = [p["ln_g"], p["ln_b"], p["ones_bd"], p["w_out"]]
    return pl.pallas_call(
        _out_proj_kernel,
        grid=(n_tiles,),
        in_specs=[row(D_MODEL), pl.BlockSpec((1, 6, D_MODEL), lambda j: (jnp.minimum(j // n_lat_tiles, 1), 0, 0))]
                 + [row(WIDTH)] * 6 + [pl.BlockSpec((TILE, WIDTH), lambda j: (j, 1)), row(MLA_WIDTH)]
                 + [_const_spec(a.shape) for a in consts],
        out_specs=row(D_MODEL),
        out_shape=jax.ShapeDtypeStruct((n, D_MODEL), F32),
        compiler_params=_params(1),
        name="out_proj",
    )(x_all, mod, yf, yb, cv, g, hf, hb, lru, o_mla, *consts)


def _ffn_kernel(tm, block0, seg_lo, seg_hi, mod_row, final, *refs):
    if final:
        (x_ref, prev_ref, next_ref, mod_ref, g_ref, wg_ref, wu_ref, wd_ref, cw_ref, cb_ref, fin_ref,
         o_ref, h_scr, acc_scr) = refs
    else:
        (x_ref, prev_ref, next_ref, mod_ref, g_ref, wg_ref, wu_ref, wd_ref, cw_ref, cb_ref,
         o_ref, h_scr, acc_scr) = refs
    c = pl.program_id(1)
    ext = tm + 2 * FFN_HALO

    @pl.when(c == 0)
    def _():
        sh = mod_ref[mod_row, 3:4, :]
        sc = 1.0 + mod_ref[mod_row, 4:5, :]
        g = g_ref[...]
        h_scr[0:FFN_HALO, :] = (_rms(prev_ref[...], g) * sc + sh).astype(BF16)
        h_scr[FFN_HALO:FFN_HALO + tm, :] = (_rms(x_ref[...], g) * sc + sh).astype(BF16)
        h_scr[FFN_HALO + tm:ext, :] = (_rms(next_ref[...], g) * sc + sh).astype(BF16)
        acc_scr[...] = jnp.zeros_like(acc_scr)

    ge = jnp.dot(h_scr[...], wg_ref[...], preferred_element_type=F32)
    grow = (block0 + pl.program_id(0)) * tm - FFN_HALO + _row_iota(ge.shape)
    ge = jnp.where(jnp.logical_and(grow >= seg_lo, grow < seg_hi), ge, 0.0)
    up_rows = pltpu.roll(ge, 1, 0)[FFN_HALO:FFN_HALO + tm, :]
    dn_rows = pltpu.roll(ge, ext - 1, 0)[FFN_HALO:FFN_HALO + tm, :]
    gate = cb_ref[...] + up_rows * cw_ref[0:1, :] + ge[FFN_HALO:FFN_HALO + tm, :] * cw_ref[1:2, :] + dn_rows * cw_ref[2:3, :]
    up = jnp.dot(h_scr[FFN_HALO:FFN_HALO + tm, :], wu_ref[...], preferred_element_type=F32)
    act = (_silu(gate) * up).astype(BF16)
    acc_scr[...] += jnp.dot(act, wd_ref[...], preferred_element_type=F32)

    @pl.when(c == pl.num_programs(1) - 1)
    def _():
        out = x_ref[...] + mod_ref[mod_row, 5:6, :] * acc_scr[...]
        if final:
            out = _rms(out, fin_ref[...])
        o_ref[...] = out


def _ffn_call(x_all, o_prev, mod, p, tm, block0, n_blocks, seg_lo, seg_hi, mod_row, final_g, out_rows):
    n = x_all.shape[0]
    hb = tm // FFN_HALO
    nb = n // FFN_HALO
    n_chunks = D_FF // FFN_CHUNK
    final = final_g is not None
    in_specs = [
        pl.BlockSpec((tm, D_MODEL), lambda j, c: (block0 + j, 0)),
        pl.BlockSpec((FFN_HALO, D_MODEL), lambda j, c: (jnp.maximum((block0 + j) * hb - 1, 0), 0)),
        pl.BlockSpec((FFN_HALO, D_MODEL), lambda j, c: (jnp.minimum((block0 + j + 1) * hb, nb - 1), 0)),
        pl.BlockSpec((2, 6, D_MODEL), lambda j, c: (0, 0, 0)),
        pl.BlockSpec((1, D_MODEL), lambda j, c: (0, 0)),
        pl.BlockSpec((D_MODEL, FFN_CHUNK), lambda j, c: (0, c)),
        pl.BlockSpec((D_MODEL, FFN_CHUNK), lambda j, c: (0, c)),
        pl.BlockSpec((FFN_CHUNK, D_MODEL), lambda j, c: (c, 0)),
        pl.BlockSpec((3, FFN_CHUNK), lambda j, c: (0, c)),
        pl.BlockSpec((1, FFN_CHUNK), lambda j, c: (0, c)),
    ]
    args = [x_all, x_all, x_all, mod, p["norm2"], p["w_gate"], p["w_up"], p["w_down"], p["ffn_conv"], p["ffn_conv_b"]]
    if final:
        in_specs.append(pl.BlockSpec((1, D_MODEL), lambda j, c: (0, 0)))
        args.append(final_g)
    aliases = {}
    if o_prev is not None:
        in_specs.append(pl.BlockSpec(memory_space=pl.ANY))
        args.append(o_prev)
        aliases = {len(args) - 1: 0}
    kern = functools.partial(_ffn_kernel, tm, block0, seg_lo, seg_hi, mod_row, final)
    if o_prev is not None:
        kern = _drop_ref(kern, len(args) - 1)
    return pl.pallas_call(
        kern,
        grid=(n_blocks, n_chunks),
        in_specs=in_specs,
        out_specs=pl.BlockSpec((tm, D_MODEL), lambda j, c: (block0 + j, 0)),
        out_shape=jax.ShapeDtypeStruct((out_rows, D_MODEL), F32),
        scratch_shapes=[pltpu.VMEM((tm + 2 * FFN_HALO, D_MODEL), BF16), pltpu.VMEM((tm, D_MODEL), F32)],
        input_output_aliases=aliases,
        compiler_params=_params(2),
        name="conv_ffn",
    )(*args)


def _drop_ref(kern, idx):
    def wrapped(*refs):
        return kern(*(refs[:idx] + refs[idx + 1:]))
    return wrapped


_ROPE_PERM = np.concatenate([np.arange(16, 32), np.arange(0, 16), np.arange(48, 64), np.arange(32, 48)])


def _block_diag(blocks):
    h, n, m = blocks.shape
    eye = jnp.eye(h, dtype=blocks.dtype)
    return (eye[:, None, :, None] * blocks[:, :, None, :]).reshape(h * n, h * m)


def _scan_consts():
    idx = np.arange(CHUNK)
    lower = (idx[None, :] <= idx[:, None]).astype(np.float32)
    upper = (idx[None, :] >= idx[:, None]).astype(np.float32)
    eye_g = np.eye(GROUP, dtype=np.float32)
    bd = lambda m: np.kron(eye_g, m)
    return {
        "tri_fwd": jnp.asarray(lower), "tri_rev": jnp.asarray(upper),
        "incl_fwd": jnp.asarray(bd(lower)), "incl_rev": jnp.asarray(bd(upper)),
        "strict_fwd": jnp.asarray(bd(lower - np.eye(CHUNK, dtype=np.float32))),
        "strict_rev": jnp.asarray(bd(upper - np.eye(CHUNK, dtype=np.float32))),
        "bd_mask": jnp.asarray(bd(np.ones((CHUNK, HEAD_DIM), np.float32))),
        "eye": jnp.asarray(np.eye(GW, dtype=np.float32)),
    }


def _rope_tables(n_lat, n_ctx):
    n_freq = ROPE // 4
    rows = n_lat // GRID_W
    row = jnp.repeat(jnp.arange(rows, dtype=F32), GRID_W)
    col = jnp.tile(jnp.arange(GRID_W, dtype=F32), rows)
    inv_freq = ROPE_THETA ** (-jnp.arange(n_freq, dtype=F32) / n_freq)
    ar, ac = row[:, None] * inv_freq, col[:, None] * inv_freq
    cos = jnp.concatenate([jnp.cos(ar), jnp.cos(ar), jnp.cos(ac), jnp.cos(ac)], axis=-1)
    sin = jnp.concatenate([-jnp.sin(ar), jnp.sin(ar), -jnp.sin(ac), jnp.sin(ac)], axis=-1)
    cos = jnp.concatenate([cos, jnp.ones((n_ctx, ROPE), F32)], axis=0)
    sin = jnp.concatenate([sin, jnp.zeros((n_ctx, ROPE), F32)], axis=0)
    return {"cc": jnp.concatenate([cos, cos], axis=-1), "ss": jnp.concatenate([sin, sin], axis=-1),
            "cs": jnp.concatenate([cos, sin], axis=-1)}


def kernel(x, c, ctx, c_ctx, ada_w, ada_b, norm1, norm2, w_in, w_out, rw_conv, rw_conv_b, rw_w0, rw_w_up, rw_a0, rw_a_up, rw_g_up, rw_k_k, rw_k_a, rw_r_k, rw_ln_g, rw_ln_b, rw_v0, rw_v_down, rw_v_up, lru_conv, lru_conv_b, lru_wa, lru_ba, lru_wx, lru_bx, lru_lambda, mla_q_norm, mla_w_qb, mla_kv_norm, mla_w_kvb, ffn_w_gate, ffn_w_up, ffn_conv, ffn_conv_b, ffn_w_down, final_norm):
    assert x.shape[0] == 1 and ctx.shape[0] == 1
    depth = ada_w.shape[0]
    n_lat, n_ctx = x.shape[1], ctx.shape[1]
    assert n_lat % FFN_TILE == 0 and n_ctx % TILE == 0 and n_lat % GRID_W == 0
    n = n_lat + n_ctx
    n_lat_tiles = n_lat // TILE
    seg = ((0, n_lat), (n_lat, n))

    x_all = jnp.concatenate([x[0], ctx[0]], axis=0)
    mods = _mod_call(jnp.stack([c[0], c_ctx], axis=1), ada_w, ada_b).reshape(depth, 2, 6, D_MODEL)
    tabs = _rope_tables(n_lat, n_ctx)
    consts = _scan_consts()
    ones_bd = jnp.asarray(np.kron(np.eye(HEADS, dtype=np.float32), np.ones((HEAD_DIM, HEAD_DIM), np.float32)))
    kv_chunk = 1280 if n % 1280 == 0 else TILE
    row2 = lambda a: a.reshape(1, -1)

    v_first = None
    out = None
    for i in range(depth):
        last = i == depth - 1
        mla_off = 1760 + LRU_COLS
        rope_cols = mla_off + Q_RANK + KV_RANK + _ROPE_PERM
        vdown = rw_v_down[i - 1] if i > 0 else jnp.zeros((D_MODEL, LORA), F32)
        w_cat = jnp.concatenate([w_in[i][:, :1760], vdown, w_in[i][:, 1760:], w_in[i][:, rope_cols]], axis=1).astype(BF16)

        w_lora = jnp.zeros((RW_COLS - 3 * WIDTH, 6 * WIDTH), F32)
        b_lora = jnp.zeros((6 * WIDTH,), F32)
        for d in range(2):
            w_lora = w_lora.at[d * LORA:(d + 1) * LORA, d * WIDTH:(d + 1) * WIDTH].set(rw_w_up[i][d])
            w_lora = w_lora.at[(2 + d) * LORA:(3 + d) * LORA, (2 + d) * WIDTH:(3 + d) * WIDTH].set(rw_a_up[i][d])
            b_lora = b_lora.at[d * WIDTH:(d + 1) * WIDTH].set(rw_w0[i][d])
            b_lora = b_lora.at[(2 + d) * WIDTH:(3 + d) * WIDTH].set(rw_a0[i][d])
        w_lora = w_lora.at[4 * LORA:4 * LORA + GATE_LORA, 4 * WIDTH:5 * WIDTH].set(rw_g_up[i])
        if i > 0:
            w_lora = w_lora.at[4 * LORA + GATE_LORA:, 5 * WIDTH:].set(rw_v_up[i - 1])
            b_lora = b_lora.at[5 * WIDTH:].set(rw_v0[i - 1])
        ident = jnp.array([[0.0], [1.0], [0.0]], F32) * jnp.ones((1, LORA), F32)
        rw_p = {
            "rw_conv": jnp.concatenate([rw_conv[i], ident], axis=1),
            "rw_conv_b": row2(jnp.concatenate([rw_conv_b[i], jnp.zeros((LORA,), F32)])),
            "w_lora": w_lora.astype(BF16), "b_lora": row2(b_lora),
            "k_k": row2(rw_k_k[i]), "k_a": row2(rw_k_a[i]), "r_k": row2(rw_r_k[i]), "ones_bd": ones_bd,
        }
        lru_p = {
            "lru_conv": lru_conv[i], "lru_conv_b": row2(lru_conv_b[i]),
            "lru_wg": jnp.stack([jnp.concatenate([_block_diag(lru_wa[i][d]), _block_diag(lru_wx[i][d])], axis=1)
                                 for d in range(2)]).astype(BF16),
            "lru_bg": jnp.stack([jnp.concatenate([lru_ba[i][d], lru_bx[i][d]])[None] for d in range(2)]),
            "lru_lam": lru_lambda[i][:, None, :],
        }
        wq = mla_w_qb[i].reshape(Q_RANK, HEADS, NOPE + ROPE)
        mla_p = {
            "q_norm": row2(mla_q_norm[i]), "kv_norm": row2(mla_kv_norm[i]),
            "w_q": jnp.concatenate([wq[:, :, :NOPE].reshape(Q_RANK, -1), wq[:, :, NOPE:].reshape(Q_RANK, -1),
                                    wq[:, :, NOPE + _ROPE_PERM].reshape(Q_RANK, -1)], axis=1).astype(BF16),
            "w_kv": mla_w_kvb[i].astype(BF16),
        }
        out_p = {"ln_g": row2(rw_ln_g[i]), "ln_b": row2(rw_ln_b[i]), "ones_bd": ones_bd, "w_out": w_out[i].astype(BF16)}
        ffn_p = {"norm2": row2(norm2[i]), "w_gate": ffn_w_gate[i].astype(BF16), "w_up": ffn_w_up[i].astype(BF16),
                 "w_down": ffn_w_down[i].astype(BF16), "ffn_conv": ffn_conv[i], "ffn_conv_b": row2(ffn_conv_b[i])}

        rw, lru, mla = _in_proj_call(x_all, mods[i], row2(norm1[i]), w_cat, n_lat_tiles)

        r, v, kk, g, cv, lw, kd, bd = _rwkv_prep_call(rw, v_first, rw_p, seg)
        if i == 0:
            v_first = v
        y_f, y_b = _rwkv_scan_call(r, v, kk, lw, kd, bd, consts, n_lat_tiles)

        h_f = _lru_call(False, lru, lru_p, seg)
        h_b = _lru_call(True, lru, lru_p, seg)

        q_h, k_h, v_h = _mla_proj_call(mla, tabs, mla_p)
        o_mla = _attn_call(q_h, k_h, v_h, None, Q_TILE, 0, n_lat // Q_TILE, 0, kv_chunk, n // kv_chunk)
        if not last:
            o_mla = _attn_call(q_h, k_h, v_h, o_mla, TILE, n_lat_tiles, n_ctx // TILE, n_lat, TILE, n_ctx // TILE)

        n_tiles = n_lat_tiles if last else n // TILE
        x_mid = _out_proj_call(x_all, mods[i], y_f, y_b, cv, g, h_f, h_b, lru, o_mla, out_p, n_tiles, n_lat_tiles)

        if last:
            out = _ffn_call(x_mid, None, mods[i], ffn_p, FFN_TILE, 0, n_lat // FFN_TILE, 0, n_lat, 0,
                            row2(final_norm), n_lat)
        else:
            x_new = _ffn_call(x_mid, None, mods[i], ffn_p, FFN_TILE, 0, n_lat // FFN_TILE, 0, n_lat, 0, None, n)
            x_all = _ffn_call(x_mid, x_new, mods[i], ffn_p, TILE, n_lat_tiles, n_ctx // TILE, n_lat, n, 1, None, n)
    return out[None]
```

```python
import functools

import numpy as np
import jax
import jax.numpy as jnp
from jax import lax
from jax.experimental import pallas as pl
from jax.experimental.pallas import tpu as pltpu

F32 = jnp.float32
BF16 = jnp.bfloat16

D_MODEL = 2048
NORM_EPS = 1e-6
GN_EPS = 64e-5
DECAY_SCALE = 0.606531
LRU_C = 8.0
HEADS = 8
HEAD_DIM = 64
WIDTH = HEADS * HEAD_DIM
LORA = 32
GATE_LORA = 96
RW_COLS = 1792
LRU_COLS = 1024
MLA_COLS = 896
Q_RANK = 512
KV_RANK = 256
NOPE = 128
ROPE = 64
V_DIM = 128
V_SLOT = 256
QK_DIM = 256
MLA_WIDTH = HEADS * V_DIM
MLA_SCALE = (NOPE + ROPE) ** -0.5
Q_SCALE = MLA_SCALE * 1.4426950408889634
ROPE_THETA = 10000.0
GRID_W = 64
D_FF = 5632

TILE = 256
CHUNK = 64
GROUP = 4
GW = GROUP * HEAD_DIM
HALO = 8
FFN_TILE = 512
FFN_HALO = 16
FFN_CHUNK = 512
Q_TILE = 1024
VMEM_LIMIT = 56 * 1024 * 1024


def _params(n_axes, vmem=VMEM_LIMIT):
    return pltpu.CompilerParams(dimension_semantics=("arbitrary",) * n_axes, vmem_limit_bytes=vmem)


def _const_spec(shape):
    nd = len(shape)
    return pl.BlockSpec(shape, lambda *_: (0,) * nd, pipeline_mode=pl.Buffered(1))


def _sigmoid(x):
    return 1.0 / (1.0 + jnp.exp(-x))


def _silu(x):
    return x * _sigmoid(x)


def _gelu_tanh(x):
    return 0.5 * x * (1.0 + jnp.tanh(0.7978845608028654 * (x + 0.044715 * (x * x * x))))


def _rms(x, g):
    return x * lax.rsqrt(jnp.mean(x * x, axis=-1, keepdims=True) + NORM_EPS) * g


def _row_iota(shape):
    return lax.broadcasted_iota(jnp.int32, shape, 0)


def _shift_down(cur, prev_rows, k):
    out = pltpu.roll(cur, k, 0)
    rows = _row_iota(cur.shape)
    for i in range(k):
        out = jnp.where(rows == i, prev_rows[i:i + 1, :], out)
    return out


def _shift_up(cur, next_row):
    n = cur.shape[0]
    out = pltpu.roll(cur, n - 1, 0)
    return jnp.where(_row_iota(cur.shape) == n - 1, next_row, out)


def _halo_valid(j, tile, seg_starts, seg_ends):
    first = j * tile
    last = first + tile
    lvalid = jnp.logical_and(first != seg_starts[0], first != seg_starts[1])
    rvalid = jnp.logical_and(last != seg_ends[0], last != seg_ends[1])
    return lvalid.astype(F32), rvalid.astype(F32)


def _mod_kernel(cc_ref, w_ref, b_ref, o_ref):
    s = _silu(cc_ref[...])
    w = w_ref[0]
    b = b_ref[0]
    o_ref[0, 0:1, :] = jnp.sum(s[:, 0:1] * w, axis=0, keepdims=True) + b
    o_ref[0, 1:2, :] = jnp.sum(s[:, 1:2] * w, axis=0, keepdims=True) + b


def _mod_call(cc, ada_w, ada_b):
    depth, d, n6 = ada_w.shape
    tn = 1024
    return pl.pallas_call(
        _mod_kernel,
        grid=(depth, n6 // tn),
        in_specs=[
            pl.BlockSpec((d, 2), lambda i, j: (0, 0)),
            pl.BlockSpec((1, d, tn), lambda i, j: (i, 0, j)),
            pl.BlockSpec((1, 1, tn), lambda i, j: (i, 0, j)),
        ],
        out_specs=pl.BlockSpec((1, 2, tn), lambda i, j: (i, 0, j)),
        out_shape=jax.ShapeDtypeStruct((depth, 2, n6), F32),
        compiler_params=_params(2),
        name="adaln_mod",
    )(cc, ada_w, ada_b.reshape(depth, 1, n6))


def _in_proj_kernel(x_ref, mod_ref, g_ref, w_ref, rw_ref, lru_ref, mla_ref):
    h = _rms(x_ref[...], g_ref[...]) * (1.0 + mod_ref[0, 1:2, :]) + mod_ref[0, 0:1, :]
    hb = h.astype(BF16)
    rw_ref[...] = jnp.dot(hb, w_ref[:, 0:RW_COLS], preferred_element_type=F32)
    lru_ref[...] = jnp.dot(hb, w_ref[:, RW_COLS:RW_COLS + LRU_COLS], preferred_element_type=F32)
    mla_ref[...] = jnp.dot(hb, w_ref[:, RW_COLS + LRU_COLS:], preferred_element_type=F32)


def _in_proj_call(x_all, mod, g, w_cat, n_lat_tiles):
    n = x_all.shape[0]
    cols = w_cat.shape[1]
    row = lambda c: pl.BlockSpec((TILE, c), lambda j: (j, 0))
    return pl.pallas_call(
        _in_proj_kernel,
        grid=(n // TILE,),
        in_specs=[
            row(D_MODEL),
            pl.BlockSpec((1, 6, D_MODEL), lambda j: (jnp.minimum(j // n_lat_tiles, 1), 0, 0)),
            _const_spec((1, D_MODEL)),
            _const_spec((D_MODEL, cols)),
        ],
        out_specs=[row(RW_COLS), row(LRU_COLS), row(MLA_COLS)],
        out_shape=[jax.ShapeDtypeStruct((n, c), F32) for c in (RW_COLS, LRU_COLS, MLA_COLS)],
        compiler_params=_params(1),
        name="in_proj",
    )(x_all, mod, g, w_cat)


def _rwkv_prep_kernel(seg, has_vfirst, *refs):
    if has_vfirst:
        (cur_ref, prev_ref, next_ref, vf_ref, cw_ref, cb_ref, wl_ref, bl_ref, kk_ref, ka_ref, rk_ref, ones_ref,
         r_out, v_out, kk_out, g_out, cv_out, lw_out, kd_out, bd_out) = refs
    else:
        (cur_ref, prev_ref, next_ref, cw_ref, cb_ref, wl_ref, bl_ref, kk_ref, ka_ref, rk_ref, ones_ref,
         r_out, v_out, kk_out, g_out, cv_out, lw_out, kd_out, bd_out) = refs
    lvalid, rvalid = _halo_valid(pl.program_id(0), TILE, seg[0], seg[1])
    cur = cur_ref[...]
    up = _shift_down(cur, prev_ref[HALO - 1:HALO, :] * lvalid, 1)
    dn = _shift_up(cur, next_ref[0:1, :] * rvalid)
    u = cb_ref[...] + up * cw_ref[0:1, :] + cur * cw_ref[1:2, :] + dn * cw_ref[2:3, :]
    r = u[:, 0:WIDTH]
    k = u[:, WIDTH:2 * WIDTH]
    v = u[:, 2 * WIDTH:3 * WIDTH]
    blk = u[:, 3 * WIDTH:RW_COLS]
    lane = lax.broadcasted_iota(jnp.int32, blk.shape, 1)
    act = jnp.where(lane < 2 * LORA, jnp.tanh(blk),
                    jnp.where(jnp.logical_and(lane >= 4 * LORA, lane < 4 * LORA + GATE_LORA), _sigmoid(blk), blk))
    lo = jnp.dot(act.astype(BF16), wl_ref[...], preferred_element_type=F32) + bl_ref[...]
    g = lo[:, 4 * WIDTH:5 * WIDTH]
    if has_vfirst:
        mix = _sigmoid(lo[:, 5 * WIDTH:6 * WIDTH])
        v = v + (vf_ref[...] - v) * mix
    ones = ones_ref[...]
    kk = k * kk_ref[...]
    ss = jnp.dot(kk * kk, ones, preferred_element_type=F32, precision=lax.Precision.HIGHEST)
    kk = kk * lax.rsqrt(jnp.maximum(ss, 1e-24))
    ksum = None
    for d in range(2):
        lw_out[d] = -DECAY_SCALE * _sigmoid(lo[:, d * WIDTH:(d + 1) * WIDTH])
        iclr = _sigmoid(lo[:, (2 + d) * WIDTH:(3 + d) * WIDTH])
        kd = k * (1.0 + (iclr - 1.0) * ka_ref[...])
        kd_out[d] = kd
        bd_out[d] = kk * iclr
        ksum = kd if ksum is None else ksum + kd
    coef = jnp.dot(r * ksum * rk_ref[...], ones, preferred_element_type=F32, precision=lax.Precision.HIGHEST)
    r_out[...] = r
    v_out[...] = v
    kk_out[...] = kk
    g_out[...] = g
    cv_out[...] = coef * v


def _rwkv_prep_call(rw, v_first, p, seg):
    n = rw.shape[0]
    hb = TILE // HALO
    nb = n // HALO
    row = lambda c: pl.BlockSpec((TILE, c), lambda j: (j, 0))
    dir_row = pl.BlockSpec((2, TILE, WIDTH), lambda j: (0, j, 0))
    has_vf = v_first is not None
    in_specs = [
        row(RW_COLS),
        pl.BlockSpec((HALO, RW_COLS), lambda j: (jnp.maximum(j * hb - 1, 0), 0)),
        pl.BlockSpec((HALO, RW_COLS), lambda j: (jnp.minimum((j + 1) * hb, nb - 1), 0)),
    ]
    args = [rw, rw, rw]
    if has_vf:
        in_specs.append(row(WIDTH))
        args.append(v_first)
    consts = [p["rw_conv"], p["rw_conv_b"], p["w_lora"], p["b_lora"], p["k_k"], p["k_a"], p["r_k"], p["ones_bd"]]
    in_specs += [_const_spec(a.shape) for a in consts]
    args += consts
    return pl.pallas_call(
        functools.partial(_rwkv_prep_kernel, seg, has_vf),
        grid=(n // TILE,),
        in_specs=in_specs,
        out_specs=[row(WIDTH)] * 5 + [dir_row] * 3,
        out_shape=[jax.ShapeDtypeStruct((n, WIDTH), F32)] * 5 + [jax.ShapeDtypeStruct((2, n, WIDTH), F32)] * 3,
        compiler_params=_params(1),
        name="rwkv_prep",
    )(*args)


def _bd_stack(x, bd_mask):
    return jnp.concatenate([x] * GROUP, axis=0) * bd_mask


def _mm(a, b):
    return jnp.dot(a.astype(BF16), b.astype(BF16), preferred_element_type=F32)


def _mm_nt(a, b):
    return lax.dot_general(a.astype(BF16), b.astype(BF16), (((1,), (1,)), ((), ())), preferred_element_type=F32)


def _mm_tn(a, b):
    return lax.dot_general(a.astype(BF16), b.astype(BF16), (((0,), (0,)), ((), ())), preferred_element_type=F32)


def _scan_operands(reverse, rows, r_ref, v_ref, kk_ref, lw_ref, kd_ref, bd_ref, tri, strict, incl, bdm, h_ref):
    last = 0 if reverse else CHUNK - 1
    lw = lw_ref[0, rows, :]
    cl = jnp.dot(tri, lw, preferred_element_type=F32, precision=lax.Precision.HIGHEST)
    tot = cl[last:last + 1, :]
    e_cl = jnp.exp(cl)
    e_cle = jnp.exp(cl - lw)
    e_ncl = jnp.exp(-cl)
    e_tc = jnp.exp(tot - cl)
    e_tot = jnp.exp(tot)
    a_t = -kk_ref[rows, :] * e_cle
    r_t = r_ref[rows, :] * e_cl
    kd = kd_ref[0, rows, :]
    bd = bd_ref[0, rows, :]
    b_t = bd * e_ncl
    k_t = kd * e_ncl
    b_h = bd * e_tc
    k_h = kd * e_tc
    v = v_ref[rows, :]
    probs = []
    for gi in range(HEADS // GROUP):
        ln = slice(gi * GW, (gi + 1) * GW)
        r_s = _bd_stack(r_t[:, ln], bdm)
        probs.append(dict(
            r_s=r_s, r_sb=r_s.astype(BF16),
            a_s=_bd_stack(a_t[:, ln], bdm).astype(BF16),
            b_s=_bd_stack(b_t[:, ln], bdm).astype(BF16),
            k_s=_bd_stack(k_t[:, ln], bdm).astype(BF16),
            bh_s=_bd_stack(b_h[:, ln], bdm).astype(BF16),
            kh_s=_bd_stack(k_h[:, ln], bdm).astype(BF16),
            v_s=_bd_stack(v[:, ln], bdm).astype(BF16),
            e_tot=e_tot[:, ln], strict=strict, incl=incl, h_ref=h_ref, gi=gi))
    return probs


def _scan_solve(probs, eye):
    for p in probs:
        p["a_ab"] = _mm_nt(p["a_s"], p["b_s"]) * p["strict"]
    for p in probs:
        p["a_ak"] = _mm_nt(p["a_s"], p["k_s"]) * p["strict"]
    for p in probs:
        p["a_rb"] = (_mm_nt(p["r_sb"], p["b_s"]) * p["incl"]).astype(BF16)
    for p in probs:
        p["a_rk"] = _mm_nt(p["r_sb"], p["k_s"]) * p["incl"]
    for p in probs:
        p["t"] = eye + p["a_ab"]
        p["pw"] = p["a_ab"]
        p["x"] = _mm(p["a_ak"], p["v_s"])
    for _ in range(5):
        for p in probs:
            p["pw"] = _mm(p["pw"], p["pw"])
        for p in probs:
            p["t"] = p["t"] + _mm(p["t"], p["pw"])
    for p in probs:
        p["t"] = p["t"].astype(BF16)
        p["abar"] = _mm(p["t"], p["a_s"]).astype(BF16)
    for p in probs:
        p["u0"] = _mm(p["t"], p["x"]).astype(BF16)
    for p in probs:
        p["m"] = eye * p["e_tot"] + _mm_tn(p["bh_s"], p["abar"])
    for p in probs:
        p["g"] = _mm_tn(p["bh_s"], p["u0"]) + _mm_tn(p["kh_s"], p["v_s"])
    for p in probs:
        p["rbar"] = p["r_s"] + _mm(p["a_rb"], p["abar"])
    for p in probs:
        p["y0"] = _mm(p["a_rb"], p["u0"]) + _mm(p["a_rk"], p["v_s"])
    ys = []
    for p in probs:
        h = p["h_ref"][p["gi"]].astype(BF16)
        y_bd = _mm(p["rbar"], h) + p["y0"]
        p["h_ref"][p["gi"]] = _mm(p["m"], h) + p["g"]
        y = y_bd[0:CHUNK, :]
        for hh in range(1, GROUP):
            y = y + y_bd[hh * CHUNK:(hh + 1) * CHUNK, :]
        ys.append(y)
    return ys


def _rwkv_scan_kernel(rf_ref, vf_ref, kkf_ref, lwf_ref, kdf_ref, bdf_ref, rb_ref, vb_ref, kkb_ref, lwb_ref, kdb_ref, bdb_ref,
                      trif_ref, trib_ref, strictf_ref, strictb_ref, inclf_ref, inclb_ref, bdm_ref, eye_ref,
                      yf_ref, yb_ref, hf_ref, hb_ref):
    @pl.when(pl.program_id(0) == 0)
    def _():
        hf_ref[...] = jnp.zeros_like(hf_ref)
        hb_ref[...] = jnp.zeros_like(hb_ref)

    n_chunks = TILE // CHUNK
    bdm = bdm_ref[...]
    eye = eye_ref[...]

    def chunk_body(ci, carry):
        rows_f = pl.ds(pl.multiple_of(ci * CHUNK, CHUNK), CHUNK)
        rows_b = pl.ds(pl.multiple_of((n_chunks - 1 - ci) * CHUNK, CHUNK), CHUNK)
        probs = _scan_operands(False, rows_f, rf_ref, vf_ref, kkf_ref, lwf_ref, kdf_ref, bdf_ref,
                               trif_ref[...], strictf_ref[...], inclf_ref[...], bdm, hf_ref)
        probs += _scan_operands(True, rows_b, rb_ref, vb_ref, kkb_ref, lwb_ref, kdb_ref, bdb_ref,
                                trib_ref[...], strictb_ref[...], inclb_ref[...], bdm, hb_ref)
        ys = _scan_solve(probs, eye)
        n_groups = HEADS // GROUP
        yf_ref[rows_f, :] = jnp.concatenate(ys[:n_groups], axis=1)
        yb_ref[rows_b, :] = jnp.concatenate(ys[n_groups:], axis=1)
        return carry

    lax.fori_loop(0, n_chunks, chunk_body, 0)


def _scan_tile(reverse, n_lat_tiles):
    if reverse:
        return lambda j: jnp.where(j == 0, n_lat_tiles, n_lat_tiles - j)
    return lambda j: jnp.where(j == 0, n_lat_tiles, j - 1)


def _rwkv_scan_call(r, v, kk, lw, kd, bd, consts, n_lat_tiles):
    n = r.shape[0]
    specs = []
    for d, reverse in enumerate((False, True)):
        tile = _scan_tile(reverse, n_lat_tiles)
        row = pl.BlockSpec((TILE, WIDTH), lambda j, tile=tile: (tile(j), 0))
        dir_row = pl.BlockSpec((1, TILE, WIDTH), lambda j, tile=tile, d=d: (d, tile(j), 0))
        specs.append((row, dir_row))
    (row_f, dir_f), (row_b, dir_b) = specs
    cs = [consts["tri_fwd"], consts["tri_rev"], consts["strict_fwd"], consts["strict_rev"],
          consts["incl_fwd"], consts["incl_rev"], consts["bd_mask"], consts["eye"]]
    state = pltpu.VMEM((HEADS // GROUP, GW, GW), F32)
    return pl.pallas_call(
        _rwkv_scan_kernel,
        grid=(n // TILE,),
        in_specs=[row_f] * 3 + [dir_f] * 3 + [row_b] * 3 + [dir_b] * 3 + [_const_spec(a.shape) for a in cs],
        out_specs=[row_f, row_b],
        out_shape=[jax.ShapeDtypeStruct((n, WIDTH), F32)] * 2,
        scratch_shapes=[state, state],
        compiler_params=_params(1),
        name="rwkv_scan",
    )(r, v, kk, lw, kd, bd, r, v, kk, lw, kd, bd, *cs)


def _lru_kernel(reverse, seg, cur_ref, prev_ref, next_ref, cw_ref, cb_ref, wg_ref, bg_ref, lam_ref,
                hs_ref, a_scr, b_scr, h_scr):
    j = pl.program_id(0)

    @pl.when(j == 0)
    def _():
        h_scr[...] = jnp.zeros_like(h_scr)

    tile_idx = cur_tile_index(reverse, seg, j)
    lvalid, rvalid = _halo_valid(tile_idx, TILE, seg[0], seg[1])
    cur = cur_ref[...]
    prev = prev_ref[...] * lvalid
    x2 = _shift_down(cur, prev[HALO - 2:HALO, :], 2)
    x1 = _shift_down(cur, prev[HALO - 1:HALO, :], 1)
    xn = _shift_up(cur, next_ref[0:1, :] * rvalid)
    xb = cb_ref[...] + x2 * cw_ref[0:1, :] + x1 * cw_ref[1:2, :] + cur * cw_ref[2:3, :] + xn * cw_ref[3:4, :]
    gates = jnp.dot(xb.astype(BF16), wg_ref[0], preferred_element_type=F32) + bg_ref[0]
    gate_r = _sigmoid(gates[:, 0:WIDTH])
    gate_i = _sigmoid(gates[:, WIDTH:2 * WIDTH])
    lam = lam_ref[0]
    softplus = jnp.maximum(-lam, 0.0) + jnp.log(1.0 + jnp.exp(-jnp.abs(lam)))
    log_a = -LRU_C * gate_r * softplus
    a = jnp.exp(log_a)
    a_scr[...] = a
    b_scr[...] = jnp.sqrt(1.0 - a * a) * gate_i * xb

    n_groups = TILE // 8
    rows8 = _row_iota((8, WIDTH))

    def group_body(gi, h):
        g = (n_groups - 1 - gi) if reverse else gi
        rows = pl.ds(pl.multiple_of(g * 8, 8), 8)
        a8 = a_scr[rows, :]
        b8 = b_scr[rows, :]
        for s in (1, 2, 4):
            if reverse:
                a_sh = pltpu.roll(a8, 8 - s, 0)
                b_sh = pltpu.roll(b8, 8 - s, 0)
                ok = rows8 < 8 - s
            else:
                a_sh = pltpu.roll(a8, s, 0)
                b_sh = pltpu.roll(b8, s, 0)
                ok = rows8 >= s
            b8 = jnp.where(ok, a8 * b_sh + b8, b8)
            a8 = jnp.where(ok, a8 * a_sh, a8)
        hs = a8 * h + b8
        hs_ref[rows, :] = hs
        return hs[0:1, :] if reverse else hs[7:8, :]

    h_scr[...] = lax.fori_loop(0, n_groups, group_body, h_scr[...])


def cur_tile_index(reverse, seg, j):
    n_lat_tiles = seg[1][0] // TILE
    return _scan_tile(reverse, n_lat_tiles)(j)


def _lru_call(reverse, lru, p, seg):
    n = lru.shape[0]
    d = 1 if reverse else 0
    hb = TILE // HALO
    nb = n // HALO
    n_lat_tiles = seg[1][0] // TILE
    tile = _scan_tile(reverse, n_lat_tiles)
    dsel = lambda shape: pl.BlockSpec((1,) + shape, lambda j: (d, 0, 0), pipeline_mode=pl.Buffered(1))
    return pl.pallas_call(
        functools.partial(_lru_kernel, reverse, seg),
        grid=(n // TILE,),
        in_specs=[
            pl.BlockSpec((TILE, WIDTH), lambda j: (tile(j), 0)),
            pl.BlockSpec((HALO, WIDTH), lambda j: (jnp.maximum(tile(j) * hb - 1, 0), 0)),
            pl.BlockSpec((HALO, WIDTH), lambda j: (jnp.minimum((tile(j) + 1) * hb, nb - 1), 0)),
            _const_spec(p["lru_conv"].shape),
            _const_spec(p["lru_conv_b"].shape),
            dsel((WIDTH, 2 * WIDTH)),
            dsel((1, 2 * WIDTH)),
            dsel((1, WIDTH)),
        ],
        out_specs=pl.BlockSpec((TILE, WIDTH), lambda j: (tile(j), 0)),
        out_shape=jax.ShapeDtypeStruct((n, WIDTH), F32),
        scratch_shapes=[pltpu.VMEM((TILE, WIDTH), F32), pltpu.VMEM((TILE, WIDTH), F32), pltpu.VMEM((1, WIDTH), F32)],
        compiler_params=_params(1),
        name="lru_rev" if reverse else "lru_fwd",
    )(lru, lru, lru, p["lru_conv"], p["lru_conv_b"], p["lru_wg"], p["lru_bg"], p["lru_lam"])


def _mla_proj_kernel(cols_ref, cc_ref, ss_ref, cs_ref, gq_ref, gkv_ref, wq_ref, wkv_ref, q_ref, k_ref, v_ref):
    cols = cols_ref[...]
    qn = _rms(cols[:, 0:Q_RANK], gq_ref[...]).astype(BF16)
    q = jnp.dot(qn, wq_ref[...], preferred_element_type=F32)
    kvn = _rms(cols[:, Q_RANK:Q_RANK + KV_RANK], gkv_ref[...]).astype(BF16)
    kv = jnp.dot(kvn, wkv_ref[...], preferred_element_type=F32)
    kr = cols[:, Q_RANK + KV_RANK:MLA_COLS] * cs_ref[...]
    kr = (kr + pltpu.roll(kr, ROPE, 1)).astype(BF16)
    lane = lax.broadcasted_iota(jnp.int32, (TILE, 2 * ROPE), 1)
    ones_col = jnp.where(lax.broadcasted_iota(jnp.int32, (TILE, V_SLOT - V_DIM), 1) == 0, 1.0, 0.0).astype(BF16)
    cc = cc_ref[...]
    ss = ss_ref[...]
    for hp in range(HEADS // 2):
        sl = slice(HEADS * NOPE + hp * 2 * ROPE, HEADS * NOPE + (hp + 1) * 2 * ROPE)
        sw = slice(HEADS * NOPE + HEADS * ROPE + hp * 2 * ROPE, HEADS * NOPE + HEADS * ROPE + (hp + 1) * 2 * ROPE)
        roped = (q[:, sl] * cc + q[:, sw] * ss) * Q_SCALE
        for e in range(2):
            h = 2 * hp + e
            q_ref[h, :, 0:NOPE] = (q[:, h * NOPE:(h + 1) * NOPE] * Q_SCALE).astype(BF16)
            keep = (lane < ROPE) if e == 0 else (lane >= ROPE)
            q_ref[h, :, NOPE:QK_DIM] = jnp.where(keep, roped, 0.0).astype(BF16)
            k_ref[h, :, 0:NOPE] = kv[:, h * 2 * NOPE:h * 2 * NOPE + NOPE].astype(BF16)
            k_ref[h, :, NOPE:QK_DIM] = kr
            v_ref[h, :, 0:V_DIM] = kv[:, h * 2 * NOPE + NOPE:(h + 1) * 2 * NOPE].astype(BF16)
            v_ref[h, :, V_DIM:V_SLOT] = ones_col


def _mla_proj_call(mla, tabs, p):
    n = mla.shape[0]
    row = lambda c: pl.BlockSpec((TILE, c), lambda j: (j, 0))
    head = lambda c: pl.BlockSpec((HEADS, TILE, c), lambda j: (0, j, 0))
    consts = [p["q_norm"], p["kv_norm"], p["w_q"], p["w_kv"]]
    return pl.pallas_call(
        _mla_proj_kernel,
        grid=(n // TILE,),
        in_specs=[row(MLA_COLS), row(2 * ROPE), row(2 * ROPE), row(2 * ROPE)] + [_const_spec(a.shape) for a in consts],
        out_specs=[head(QK_DIM), head(QK_DIM), head(V_SLOT)],
        out_shape=[jax.ShapeDtypeStruct((HEADS, n, QK_DIM), BF16), jax.ShapeDtypeStruct((HEADS, n, QK_DIM), BF16),
                   jax.ShapeDtypeStruct((HEADS, n, V_SLOT), BF16)],
        compiler_params=_params(1),
        name="mla_proj",
    )(mla, tabs["cc"], tabs["ss"], tabs["cs"], *consts)


def _attn_kernel(kv_start, kv_chunk, n_kv, *refs):
    q_ref, k_ref, v_ref = refs[0:3]
    (o_ref, sa0, sa1, sb0, sb1, pb_scr, alpha_b, m_a, acc_a, m_b, acc_b) = refs[-11:]
    tq = acc_a.shape[0]
    for m_scr, acc_scr in ((m_a, acc_a), (m_b, acc_b)):
        m_scr[...] = jnp.full(m_scr.shape, -jnp.inf, F32)
        acc_scr[...] = jnp.zeros_like(acc_scr)
    q_a = q_ref[0, 0:tq, :]
    q_b = q_ref[0, tq:2 * tq, :]
    s_a = (sa0, sa1)
    s_b = (sb0, sb1)

    def chunk_rows(ci):
        return pl.ds(pl.multiple_of(kv_start + ci * kv_chunk, kv_chunk), kv_chunk)

    def scores(q, ci, s_ref):
        s_ref[...] = lax.dot_general(q, k_ref[0, chunk_rows(ci), :], (((1,), (1,)), ((), ())),
                                     preferred_element_type=F32)

    def softmax(s_ref, m_scr):
        s = s_ref[...]
        m_old = m_scr[...]
        m_new = jnp.maximum(m_old, jnp.max(s, axis=-1, keepdims=True))
        alpha = jnp.exp2(m_old - m_new)
        pr = jnp.exp2((s - m_new).astype(BF16))
        m_scr[...] = m_new
        return pr, alpha

    def values(acc_scr, alpha, pr, ci):
        acc_scr[...] = alpha * acc_scr[...] + jnp.dot(pr, v_ref[0, chunk_rows(ci), :], preferred_element_type=F32)

    def step(ci, cur, first, last):
        if not first:
            values(acc_b, alpha_b[...], pb_scr[...], ci - 1)
        if not last:
            scores(q_a, ci + 1, s_a[1 - cur])
        pr, alpha = softmax(s_a[cur], m_a)
        values(acc_a, alpha, pr, ci)
        if not last:
            scores(q_b, ci + 1, s_b[1 - cur])
        pr, alpha = softmax(s_b[cur], m_b)
        pb_scr[...] = pr
        alpha_b[...] = alpha

    scores(q_a, 0, sa0)
    scores(q_b, 0, sb0)
    step(0, 0, True, n_kv == 1)
    if n_kv > 1:
        n_mid = n_kv - 2
        n_pairs = n_mid // 2

        def pair(t, carry):
            step(1 + 2 * t, 1, False, False)
            step(2 + 2 * t, 0, False, False)
            return carry

        lax.fori_loop(0, n_pairs, pair, 0)
        if n_mid % 2 == 1:
            step(n_kv - 2, (n_kv - 2) % 2, False, False)
        step(n_kv - 1, (n_kv - 1) % 2, False, True)
    values(acc_b, alpha_b[...], pb_scr[...], n_kv - 1)
    o_ref[0:tq, :] = (acc_a[:, 0:V_DIM] / acc_a[:, V_DIM:V_DIM + 1]).astype(o_ref.dtype)
    o_ref[tq:2 * tq, :] = (acc_b[:, 0:V_DIM] / acc_b[:, V_DIM:V_DIM + 1]).astype(o_ref.dtype)


def _attn_call(q, k, v, o_prev, q_tile, q_block0, n_q, kv_start, kv_chunk, n_kv):
    n = q.shape[1]
    tq = q_tile // 2
    in_specs = [
        pl.BlockSpec((1, q_tile, QK_DIM), lambda h, i: (h, q_block0 + i, 0)),
        pl.BlockSpec((1, n, QK_DIM), lambda h, i: (h, 0, 0)),
        pl.BlockSpec((1, n, V_SLOT), lambda h, i: (h, 0, 0)),
    ]
    args = [q, k, v]
    aliases = {}
    if o_prev is not None:
        in_specs.append(pl.BlockSpec(memory_space=pl.ANY))
        args.append(o_prev)
        aliases = {3: 0}
    return pl.pallas_call(
        functools.partial(_attn_kernel, kv_start, kv_chunk, n_kv),
        grid=(HEADS, n_q),
        in_specs=in_specs,
        out_specs=pl.BlockSpec((q_tile, V_DIM), lambda h, i: (q_block0 + i, h)),
        out_shape=jax.ShapeDtypeStruct((n, MLA_WIDTH), BF16),
        scratch_shapes=[pltpu.VMEM((tq, kv_chunk), F32)] * 4
                       + [pltpu.VMEM((tq, kv_chunk), BF16), pltpu.VMEM((tq, 1), F32)]
                       + [pltpu.VMEM((tq, 1), F32), pltpu.VMEM((tq, V_SLOT), F32)] * 2,
        input_output_aliases=aliases,
        compiler_params=_params(2),
        name="mla_attn",
    )(*args)


def _out_proj_kernel(x_ref, mod_ref, yf_ref, yb_ref, cv_ref, g_ref, hf_ref, hb_ref, gate_ref, om_ref,
                     lng_ref, lnb_ref, ones_ref, w_ref, o_ref):
    ones = ones_ref[...]
    hi = lax.Precision.HIGHEST
    y = yf_ref[...] + yb_ref[...]
    mu = jnp.dot(y, ones, preferred_element_type=F32, precision=hi) * (1.0 / HEAD_DIM)
    dlt = y - mu
    var = jnp.dot(dlt * dlt, ones, preferred_element_type=F32, precision=hi) * (1.0 / HEAD_DIM)
    yn = dlt * lax.rsqrt(var + GN_EPS) * lng_ref[...] + lnb_ref[...]
    o_rw = ((yn + cv_ref[...]) * g_ref[...]).astype(BF16)
    o_lru = ((hf_ref[...] + hb_ref[...]) * _gelu_tanh(gate_ref[...])).astype(BF16)
    acc = jnp.dot(o_rw, w_ref[0:WIDTH, :], preferred_element_type=F32)
    acc += jnp.dot(o_lru, w_ref[WIDTH:2 * WIDTH, :], preferred_element_type=F32)
    acc += jnp.dot(om_ref[...], w_ref[2 * WIDTH:, :], preferred_element_type=F32)
    o_ref[...] = x_ref[...] + mod_ref[0, 2:3, :] * acc


def _out_proj_call(x_all, mod, yf, yb, cv, g, hf, hb, lru, o_mla, p, n_tiles, n_lat_tiles):
    n = x_all.shape[0]
    row = lambda c: pl.BlockSpec((TILE, c), lambda j: (j, 0))
    consts = [p["ln_g"], p["ln_b"], p["ones_bd"], p["w_out"]]
    return pl.pallas_call(
        _out_proj_kernel,
        grid=(n_tiles,),
        in_specs=[row(D_MODEL), pl.BlockSpec((1, 6, D_MODEL), lambda j: (jnp.minimum(j // n_lat_tiles, 1), 0, 0))]
                 + [row(WIDTH)] * 6 + [pl.BlockSpec((TILE, WIDTH), lambda j: (j, 1)), row(MLA_WIDTH)]
                 + [_const_spec(a.shape) for a in consts],
        out_specs=row(D_MODEL),
        out_shape=jax.ShapeDtypeStruct((n, D_MODEL), F32),
        compiler_params=_params(1),
        name="out_proj",
    )(x_all, mod, yf, yb, cv, g, hf, hb, lru, o_mla, *consts)


def _ffn_kernel(tm, block0, seg_lo, seg_hi, mod_row, final, *refs):
    if final:
        (x_ref, prev_ref, next_ref, mod_ref, g_ref, wg_ref, wu_ref, wd_ref, cw_ref, cb_ref, fin_ref,
         o_ref, h_scr, acc_scr) = refs
    else:
        (x_ref, prev_ref, next_ref, mod_ref, g_ref, wg_ref, wu_ref, wd_ref, cw_ref, cb_ref,
         o_ref, h_scr, acc_scr) = refs
    c = pl.program_id(1)
    ext = tm + 2 * FFN_HALO

    @pl.when(c == 0)
    def _():
        sh = mod_ref[mod_row, 3:4, :]
        sc = 1.0 + mod_ref[mod_row, 4:5, :]
        g = g_ref[...]
        h_scr[0:FFN_HALO, :] = (_rms(prev_ref[...], g) * sc + sh).astype(BF16)
        h_scr[FFN_HALO:FFN_HALO + tm, :] = (_rms(x_ref[...], g) * sc + sh).astype(BF16)
        h_scr[FFN_HALO + tm:ext, :] = (_rms(next_ref[...], g) * sc + sh).astype(BF16)
        acc_scr[...] = jnp.zeros_like(acc_scr)

    ge = jnp.dot(h_scr[...], wg_ref[...], preferred_element_type=F32)
    grow = (block0 + pl.program_id(0)) * tm - FFN_HALO + _row_iota(ge.shape)
    ge = jnp.where(jnp.logical_and(grow >= seg_lo, grow < seg_hi), ge, 0.0)
    up_rows = pltpu.roll(ge, 1, 0)[FFN_HALO:FFN_HALO + tm, :]
    dn_rows = pltpu.roll(ge, ext - 1, 0)[FFN_HALO:FFN_HALO + tm, :]
    gate = cb_ref[...] + up_rows * cw_ref[0:1, :] + ge[FFN_HALO:FFN_HALO + tm, :] * cw_ref[1:2, :] + dn_rows * cw_ref[2:3, :]
    up = jnp.dot(h_scr[FFN_HALO:FFN_HALO + tm, :], wu_ref[...], preferred_element_type=F32)
    act = (_silu(gate) * up).astype(BF16)
    acc_scr[...] += jnp.dot(act, wd_ref[...], preferred_element_type=F32)

    @pl.when(c == pl.num_programs(1) - 1)
    def _():
        out = x_ref[...] + mod_ref[mod_row, 5:6, :] * acc_scr[...]
        if final:
            out = _rms(out, fin_ref[...])
        o_ref[...] = out


def _ffn_call(x_all, o_prev, mod, p, tm, block0, n_blocks, seg_lo, seg_hi, mod_row, final_g, out_rows):
    n = x_all.shape[0]
    hb = tm // FFN_HALO
    nb = n // FFN_HALO
    n_chunks = D_FF // FFN_CHUNK
    final = final_g is not None
    in_specs = [
        pl.BlockSpec((tm, D_MODEL), lambda j, c: (block0 + j, 0)),
        pl.BlockSpec((FFN_HALO, D_MODEL), lambda j, c: (jnp.maximum((block0 + j) * hb - 1, 0), 0)),
        pl.BlockSpec((FFN_HALO, D_MODEL), lambda j, c: (jnp.minimum((block0 + j + 1) * hb, nb - 1), 0)),
        pl.BlockSpec((2, 6, D_MODEL), lambda j, c: (0, 0, 0)),
        pl.BlockSpec((1, D_MODEL), lambda j, c: (0, 0)),
        pl.BlockSpec((D_MODEL, FFN_CHUNK), lambda j, c: (0, c)),
        pl.BlockSpec((D_MODEL, FFN_CHUNK), lambda j, c: (0, c)),
        pl.BlockSpec((FFN_CHUNK, D_MODEL), lambda j, c: (c, 0)),
        pl.BlockSpec((3, FFN_CHUNK), lambda j, c: (0, c)),
        pl.BlockSpec((1, FFN_CHUNK), lambda j, c: (0, c)),
    ]
    args = [x_all, x_all, x_all, mod, p["norm2"], p["w_gate"], p["w_up"], p["w_down"], p["ffn_conv"], p["ffn_conv_b"]]
    if final:
        in_specs.append(pl.BlockSpec((1, D_MODEL), lambda j, c: (0, 0)))
        args.append(final_g)
    aliases = {}
    if o_prev is not None:
        in_specs.append(pl.BlockSpec(memory_space=pl.ANY))
        args.append(o_prev)
        aliases = {len(args) - 1: 0}
    kern = functools.partial(_ffn_kernel, tm, block0, seg_lo, seg_hi, mod_row, final)
    if o_prev is not None:
        kern = _drop_ref(kern, len(args) - 1)
    return pl.pallas_call(
        kern,
        grid=(n_blocks, n_chunks),
        in_specs=in_specs,
        out_specs=pl.BlockSpec((tm, D_MODEL), lambda j, c: (block0 + j, 0)),
        out_shape=jax.ShapeDtypeStruct((out_rows, D_MODEL), F32),
        scratch_shapes=[pltpu.VMEM((tm + 2 * FFN_HALO, D_MODEL), BF16), pltpu.VMEM((tm, D_MODEL), F32)],
        input_output_aliases=aliases,
        compiler_params=_params(2),
        name="conv_ffn",
    )(*args)


def _drop_ref(kern, idx):
    def wrapped(*refs):
        return kern(*(refs[:idx] + refs[idx + 1:]))
    return wrapped


_ROPE_PERM = np.concatenate([np.arange(16, 32), np.arange(0, 16), np.arange(48, 64), np.arange(32, 48)])


def _block_diag(blocks):
    h, n, m = blocks.shape
    eye = jnp.eye(h, dtype=blocks.dtype)
    return (eye[:, None, :, None] * blocks[:, :, None, :]).reshape(h * n, h * m)


def _scan_consts():
    idx = np.arange(CHUNK)
    lower = (idx[None, :] <= idx[:, None]).astype(np.float32)
    upper = (idx[None, :] >= idx[:, None]).astype(np.float32)
    eye_g = np.eye(GROUP, dtype=np.float32)
    bd = lambda m: np.kron(eye_g, m)
    return {
        "tri_fwd": jnp.asarray(lower), "tri_rev": jnp.asarray(upper),
        "incl_fwd": jnp.asarray(bd(lower)), "incl_rev": jnp.asarray(bd(upper)),
        "strict_fwd": jnp.asarray(bd(lower - np.eye(CHUNK, dtype=np.float32))),
        "strict_rev": jnp.asarray(bd(upper - np.eye(CHUNK, dtype=np.float32))),
        "bd_mask": jnp.asarray(bd(np.ones((CHUNK, HEAD_DIM), np.float32))),
        "eye": jnp.asarray(np.eye(GW, dtype=np.float32)),
    }


def _rope_tables(n_lat, n_ctx):
    n_freq = ROPE // 4
    rows = n_lat // GRID_W
    row = jnp.repeat(jnp.arange(rows, dtype=F32), GRID_W)
    col = jnp.tile(jnp.arange(GRID_W, dtype=F32), rows)
    inv_freq = ROPE_THETA ** (-jnp.arange(n_freq, dtype=F32) / n_freq)
    ar, ac = row[:, None] * inv_freq, col[:, None] * inv_freq
    cos = jnp.concatenate([jnp.cos(ar), jnp.cos(ar), jnp.cos(ac), jnp.cos(ac)], axis=-1)
    sin = jnp.concatenate([-jnp.sin(ar), jnp.sin(ar), -jnp.sin(ac), jnp.sin(ac)], axis=-1)
    cos = jnp.concatenate([cos, jnp.ones((n_ctx, ROPE), F32)], axis=0)
    sin = jnp.concatenate([sin, jnp.zeros((n_ctx, ROPE), F32)], axis=0)
    return {"cc": jnp.concatenate([cos, cos], axis=-1), "ss": jnp.concatenate([sin, sin], axis=-1),
            "cs": jnp.concatenate([cos, sin], axis=-1)}


def kernel(x, c, ctx, c_ctx, ada_w, ada_b, norm1, norm2, w_in, w_out, rw_conv, rw_conv_b, rw_w0, rw_w_up, rw_a0, rw_a_up, rw_g_up, rw_k_k, rw_k_a, rw_r_k, rw_ln_g, rw_ln_b, rw_v0, rw_v_down, rw_v_up, lru_conv, lru_conv_b, lru_wa, lru_ba, lru_wx, lru_bx, lru_lambda, mla_q_norm, mla_w_qb, mla_kv_norm, mla_w_kvb, ffn_w_gate, ffn_w_up, ffn_conv, ffn_conv_b, ffn_w_down, final_norm):
    assert x.shape[0] == 1 and ctx.shape[0] == 1
    depth = ada_w.shape[0]
    n_lat, n_ctx = x.shape[1], ctx.shape[1]
    assert n_lat % FFN_TILE == 0 and n_ctx % TILE == 0 and n_lat % GRID_W == 0
    n = n_lat + n_ctx
    n_lat_tiles = n_lat // TILE
    seg = ((0, n_lat), (n_lat, n))

    x_all = jnp.concatenate([x[0], ctx[0]], axis=0)
    mods = _mod_call(jnp.stack([c[0], c_ctx], axis=1), ada_w, ada_b).reshape(depth, 2, 6, D_MODEL)
    tabs = _rope_tables(n_lat, n_ctx)
    consts = _scan_consts()
    ones_bd = jnp.asarray(np.kron(np.eye(HEADS, dtype=np.float32), np.ones((HEAD_DIM, HEAD_DIM), np.float32)))
    kv_chunk = 1280 if n % 1280 == 0 else TILE
    row2 = lambda a: a.reshape(1, -1)

    v_first = None
    out = None
    for i in range(depth):
        last = i == depth - 1
        mla_off = 1760 + LRU_COLS
        rope_cols = mla_off + Q_RANK + KV_RANK + _ROPE_PERM
        vdown = rw_v_down[i - 1] if i > 0 else jnp.zeros((D_MODEL, LORA), F32)
        w_cat = jnp.concatenate([w_in[i][:, :1760], vdown, w_in[i][:, 1760:], w_in[i][:, rope_cols]], axis=1).astype(BF16)

        w_lora = jnp.zeros((RW_COLS - 3 * WIDTH, 6 * WIDTH), F32)
        b_lora = jnp.zeros((6 * WIDTH,), F32)
        for d in range(2):
            w_lora = w_lora.at[d * LORA:(d + 1) * LORA, d * WIDTH:(d + 1) * WIDTH].set(rw_w_up[i][d])
            w_lora = w_lora.at[(2 + d) * LORA:(3 + d) * LORA, (2 + d) * WIDTH:(3 + d) * WIDTH].set(rw_a_up[i][d])
            b_lora = b_lora.at[d * WIDTH:(d + 1) * WIDTH].set(rw_w0[i][d])
            b_lora = b_lora.at[(2 + d) * WIDTH:(3 + d) * WIDTH].set(rw_a0[i][d])
        w_lora = w_lora.at[4 * LORA:4 * LORA + GATE_LORA, 4 * WIDTH:5 * WIDTH].set(rw_g_up[i])
        if i > 0:
            w_lora = w_lora.at[4 * LORA + GATE_LORA:, 5 * WIDTH:].set(rw_v_up[i - 1])
            b_lora = b_lora.at[5 * WIDTH:].set(rw_v0[i - 1])
        ident = jnp.array([[0.0], [1.0], [0.0]], F32) * jnp.ones((1, LORA), F32)
        rw_p = {
            "rw_conv": jnp.concatenate([rw_conv[i], ident], axis=1),
            "rw_conv_b": row2(jnp.concatenate([rw_conv_b[i], jnp.zeros((LORA,), F32)])),
            "w_lora": w_lora.astype(BF16), "b_lora": row2(b_lora),
            "k_k": row2(rw_k_k[i]), "k_a": row2(rw_k_a[i]), "r_k": row2(rw_r_k[i]), "ones_bd": ones_bd,
        }
        lru_p = {
            "lru_conv": lru_conv[i], "lru_conv_b": row2(lru_conv_b[i]),
            "lru_wg": jnp.stack([jnp.concatenate([_block_diag(lru_wa[i][d]), _block_diag(lru_wx[i][d])], axis=1)
                                 for d in range(2)]).astype(BF16),
            "lru_bg": jnp.stack([jnp.concatenate([lru_ba[i][d], lru_bx[i][d]])[None] for d in range(2)]),
            "lru_lam": lru_lambda[i][:, None, :],
        }
        wq = mla_w_qb[i].reshape(Q_RANK, HEADS, NOPE + ROPE)
        mla_p = {
            "q_norm": row2(mla_q_norm[i]), "kv_norm": row2(mla_kv_norm[i]),
            "w_q": jnp.concatenate([wq[:, :, :NOPE].reshape(Q_RANK, -1), wq[:, :, NOPE:].reshape(Q_RANK, -1),
                                    wq[:, :, NOPE + _ROPE_PERM].reshape(Q_RANK, -1)], axis=1).astype(BF16),
            "w_kv": mla_w_kvb[i].astype(BF16),
        }
        out_p = {"ln_g": row2(rw_ln_g[i]), "ln_b": row2(rw_ln_b[i]), "ones_bd": ones_bd, "w_out": w_out[i].astype(BF16)}
        ffn_p = {"norm2": row2(norm2[i]), "w_gate": ffn_w_gate[i].astype(BF16), "w_up": ffn_w_up[i].astype(BF16),
                 "w_down": ffn_w_down[i].astype(BF16), "ffn_conv": ffn_conv[i], "ffn_conv_b": row2(ffn_conv_b[i])}

        rw, lru, mla = _in_proj_call(x_all, mods[i], row2(norm1[i]), w_cat, n_lat_tiles)

        r, v, kk, g, cv, lw, kd, bd = _rwkv_prep_call(rw, v_first, rw_p, seg)
        if i == 0:
            v_first = v
        y_f, y_b = _rwkv_scan_call(r, v, kk, lw, kd, bd, consts, n_lat_tiles)

        h_f = _lru_call(False, lru, lru_p, seg)
        h_b = _lru_call(True, lru, lru_p, seg)

        q_h, k_h, v_h = _mla_proj_call(mla, tabs, mla_p)
        o_mla = _attn_call(q_h, k_h, v_h, None, Q_TILE, 0, n_lat // Q_TILE, 0, kv_chunk, n // kv_chunk)
        if not last:
            o_mla = _attn_call(q_h, k_h, v_h, o_mla, TILE, n_lat_tiles, n_ctx // TILE, n_lat, TILE, n_ctx // TILE)

        n_tiles = n_lat_tiles if last else n // TILE
        x_mid = _out_proj_call(x_all, mods[i], y_f, y_b, cv, g, h_f, h_b, lru, o_mla, out_p, n_tiles, n_lat_tiles)

        if last:
            out = _ffn_call(x_mid, None, mods[i], ffn_p, FFN_TILE, 0, n_lat // FFN_TILE, 0, n_lat, 0,
                            row2(final_norm), n_lat)
        else:
            x_new = _ffn_call(x_mid, None, mods[i], ffn_p, FFN_TILE, 0, n_lat // FFN_TILE, 0, n_lat, 0, None, n)
            x_all = _ffn_call(x_mid, x_new, mods[i], ffn_p, TILE, n_lat_tiles, n_ctx // TILE, n_lat, n, 1, None, n)
    return out[None]
```

```python
import functools

import numpy as np
import jax
import jax.numpy as jnp
from jax import lax
from jax.experimental import pallas as pl
from jax.experimental.pallas import tpu as pltpu

F32 = jnp.float32
BF16 = jnp.bfloat16

D_MODEL = 2048
NORM_EPS = 1e-6
GN_EPS = 64e-5
DECAY_SCALE = 0.606531
LRU_C = 8.0
HEADS = 8
HEAD_DIM = 64
WIDTH = HEADS * HEAD_DIM
LORA = 32
GATE_LORA = 96
RW_COLS = 1792
LRU_COLS = 1024
MLA_COLS = 896
Q_RANK = 512
KV_RANK = 256
NOPE = 128
ROPE = 64
V_DIM = 128
V_SLOT = 256
QK_DIM = 256
MLA_WIDTH = HEADS * V_DIM
MLA_SCALE = (NOPE + ROPE) ** -0.5
Q_SCALE = MLA_SCALE * 1.4426950408889634
ROPE_THETA = 10000.0
GRID_W = 64
D_FF = 5632

TILE = 256
CHUNK = 64
GROUP = 4
GW = GROUP * HEAD_DIM
HALO = 8
FFN_TILE = 512
FFN_HALO = 16
FFN_CHUNK = 512
Q_TILE = 1024
VMEM_LIMIT = 56 * 1024 * 1024


def _params(n_axes, vmem=VMEM_LIMIT):
    return pltpu.CompilerParams(dimension_semantics=("arbitrary",) * n_axes, vmem_limit_bytes=vmem)


def _const_spec(shape):
    nd = len(shape)
    return pl.BlockSpec(shape, lambda *_: (0,) * nd, pipeline_mode=pl.Buffered(1))


def _sigmoid(x):
    return 1.0 / (1.0 + jnp.exp(-x))


def _silu(x):
    return x * _sigmoid(x)


def _gelu_tanh(x):
    return 0.5 * x * (1.0 + jnp.tanh(0.7978845608028654 * (x + 0.044715 * (x * x * x))))


def _rms(x, g):
    return x * lax.rsqrt(jnp.mean(x * x, axis=-1, keepdims=True) + NORM_EPS) * g


def _dot_exact01(x, w01, left=False):
    out = None
    rem = x
    for _ in range(3):
        part = rem.astype(BF16)
        rem = rem - part.astype(F32)
        term = jnp.dot(w01, part, preferred_element_type=F32) if left else jnp.dot(part, w01, preferred_element_type=F32)
        out = term if out is None else out + term
    return out


def _row_iota(shape):
    return lax.broadcasted_iota(jnp.int32, shape, 0)


def _shift_down(cur, prev_rows, k):
    out = pltpu.roll(cur, k, 0)
    rows = _row_iota(cur.shape)
    for i in range(k):
        out = jnp.where(rows == i, prev_rows[i:i + 1, :], out)
    return out


def _shift_up(cur, next_row):
    n = cur.shape[0]
    out = pltpu.roll(cur, n - 1, 0)
    return jnp.where(_row_iota(cur.shape) == n - 1, next_row, out)


def _halo_valid(j, tile, seg_starts, seg_ends):
    first = j * tile
    last = first + tile
    lvalid = jnp.logical_and(first != seg_starts[0], first != seg_starts[1])
    rvalid = jnp.logical_and(last != seg_ends[0], last != seg_ends[1])
    return lvalid.astype(F32), rvalid.astype(F32)


def _mod_kernel(cc_ref, w_ref, b_ref, o_ref):
    s = _silu(cc_ref[...])
    w = w_ref[0]
    b = b_ref[0]
    o_ref[0, 0:1, :] = jnp.sum(s[:, 0:1] * w, axis=0, keepdims=True) + b
    o_ref[0, 1:2, :] = jnp.sum(s[:, 1:2] * w, axis=0, keepdims=True) + b


def _mod_call(cc, ada_w, ada_b):
    depth, d, n6 = ada_w.shape
    tn = 1024
    return pl.pallas_call(
        _mod_kernel,
        grid=(depth, n6 // tn),
        in_specs=[
            pl.BlockSpec((d, 2), lambda i, j: (0, 0)),
            pl.BlockSpec((1, d, tn), lambda i, j: (i, 0, j)),
            pl.BlockSpec((1, 1, tn), lambda i, j: (i, 0, j)),
        ],
        out_specs=pl.BlockSpec((1, 2, tn), lambda i, j: (i, 0, j)),
        out_shape=jax.ShapeDtypeStruct((depth, 2, n6), F32),
        compiler_params=_params(2),
        name="adaln_mod",
    )(cc, ada_w, ada_b.reshape(depth, 1, n6))


def _in_proj_kernel(x_ref, mod_ref, g_ref, w_ref, rw_ref, lru_ref, mla_ref):
    h = _rms(x_ref[...], g_ref[...]) * (1.0 + mod_ref[0, 1:2, :]) + mod_ref[0, 0:1, :]
    hb = h.astype(BF16)
    rw_ref[...] = jnp.dot(hb, w_ref[:, 0:RW_COLS], preferred_element_type=F32)
    lru_ref[...] = jnp.dot(hb, w_ref[:, RW_COLS:RW_COLS + LRU_COLS], preferred_element_type=F32)
    mla_ref[...] = jnp.dot(hb, w_ref[:, RW_COLS + LRU_COLS:], preferred_element_type=F32)


def _in_proj_call(x_all, mod, g, w_cat, n_lat_tiles):
    n = x_all.shape[0]
    cols = w_cat.shape[1]
    row = lambda c: pl.BlockSpec((TILE, c), lambda j: (j, 0))
    return pl.pallas_call(
        _in_proj_kernel,
        grid=(n // TILE,),
        in_specs=[
            row(D_MODEL),
            pl.BlockSpec((1, 6, D_MODEL), lambda j: (jnp.minimum(j // n_lat_tiles, 1), 0, 0)),
            _const_spec((1, D_MODEL)),
            _const_spec((D_MODEL, cols)),
        ],
        out_specs=[row(RW_COLS), row(LRU_COLS), row(MLA_COLS)],
        out_shape=[jax.ShapeDtypeStruct((n, c), F32) for c in (RW_COLS, LRU_COLS, MLA_COLS)],
        compiler_params=_params(1),
        name="in_proj",
    )(x_all, mod, g, w_cat)


def _rwkv_prep_kernel(seg, has_vfirst, *refs):
    if has_vfirst:
        (cur_ref, prev_ref, next_ref, vf_ref, cw_ref, cb_ref, wl_ref, bl_ref, kk_ref, ka_ref, rk_ref, ones_ref,
         r_out, v_out, kk_out, g_out, cv_out, lw_out, kd_out, bd_out) = refs
    else:
        (cur_ref, prev_ref, next_ref, cw_ref, cb_ref, wl_ref, bl_ref, kk_ref, ka_ref, rk_ref, ones_ref,
         r_out, v_out, kk_out, g_out, cv_out, lw_out, kd_out, bd_out) = refs
    lvalid, rvalid = _halo_valid(pl.program_id(0), TILE, seg[0], seg[1])
    cur = cur_ref[...]
    up = _shift_down(cur, prev_ref[HALO - 1:HALO, :] * lvalid, 1)
    dn = _shift_up(cur, next_ref[0:1, :] * rvalid)
    u = cb_ref[...] + up * cw_ref[0:1, :] + cur * cw_ref[1:2, :] + dn * cw_ref[2:3, :]
    r = u[:, 0:WIDTH]
    k = u[:, WIDTH:2 * WIDTH]
    v = u[:, 2 * WIDTH:3 * WIDTH]
    blk = u[:, 3 * WIDTH:RW_COLS]
    lane = lax.broadcasted_iota(jnp.int32, blk.shape, 1)
    act = jnp.where(lane < 2 * LORA, jnp.tanh(blk),
                    jnp.where(jnp.logical_and(lane >= 4 * LORA, lane < 4 * LORA + GATE_LORA), _sigmoid(blk), blk))
    lo = jnp.dot(act.astype(BF16), wl_ref[...], preferred_element_type=F32) + bl_ref[...]
    g = lo[:, 4 * WIDTH:5 * WIDTH]
    if has_vfirst:
        mix = _sigmoid(lo[:, 5 * WIDTH:6 * WIDTH])
        v = v + (vf_ref[...] - v) * mix
    ones = ones_ref[...]
    kk = k * kk_ref[...]
    ss = _dot_exact01(kk * kk, ones)
    kk = kk * lax.rsqrt(jnp.maximum(ss, 1e-24))
    ksum = None
    for d in range(2):
        lw_out[d] = -DECAY_SCALE * _sigmoid(lo[:, d * WIDTH:(d + 1) * WIDTH])
        iclr = _sigmoid(lo[:, (2 + d) * WIDTH:(3 + d) * WIDTH])
        kd = k * (1.0 + (iclr - 1.0) * ka_ref[...])
        kd_out[d] = kd
        bd_out[d] = kk * iclr
        ksum = kd if ksum is None else ksum + kd
    coef = _dot_exact01(r * ksum * rk_ref[...], ones)
    r_out[...] = r
    v_out[...] = v
    kk_out[...] = kk
    g_out[...] = g
    cv_out[...] = coef * v


def _rwkv_prep_call(rw, v_first, p, seg):
    n = rw.shape[0]
    hb = TILE // HALO
    nb = n // HALO
    row = lambda c: pl.BlockSpec((TILE, c), lambda j: (j, 0))
    dir_row = pl.BlockSpec((2, TILE, WIDTH), lambda j: (0, j, 0))
    has_vf = v_first is not None
    in_specs = [
        row(RW_COLS),
        pl.BlockSpec((HALO, RW_COLS), lambda j: (jnp.maximum(j * hb - 1, 0), 0)),
        pl.BlockSpec((HALO, RW_COLS), lambda j: (jnp.minimum((j + 1) * hb, nb - 1), 0)),
    ]
    args = [rw, rw, rw]
    if has_vf:
        in_specs.append(row(WIDTH))
        args.append(v_first)
    consts = [p["rw_conv"], p["rw_conv_b"], p["w_lora"], p["b_lora"], p["k_k"], p["k_a"], p["r_k"], p["ones_bd"]]
    in_specs += [_const_spec(a.shape) for a in consts]
    args += consts
    return pl.pallas_call(
        functools.partial(_rwkv_prep_kernel, seg, has_vf),
        grid=(n // TILE,),
        in_specs=in_specs,
        out_specs=[row(WIDTH)] * 5 + [dir_row] * 3,
        out_shape=[jax.ShapeDtypeStruct((n, WIDTH), F32)] * 5 + [jax.ShapeDtypeStruct((2, n, WIDTH), F32)] * 3,
        compiler_params=_params(1),
        name="rwkv_prep",
    )(*args)


def _bd_stack(x, bd_mask):
    return jnp.concatenate([x] * GROUP, axis=0) * bd_mask


def _mm(a, b):
    return jnp.dot(a.astype(BF16), b.astype(BF16), preferred_element_type=F32)


def _mm_nt(a, b):
    return lax.dot_general(a.astype(BF16), b.astype(BF16), (((1,), (1,)), ((), ())), preferred_element_type=F32)


def _mm_tn(a, b):
    return lax.dot_general(a.astype(BF16), b.astype(BF16), (((0,), (0,)), ((), ())), preferred_element_type=F32)


def _scan_operands(reverse, rows, r_ref, v_ref, kk_ref, lw_ref, kd_ref, bd_ref, tri, strict, incl, bdm, h_ref):
    last = 0 if reverse else CHUNK - 1
    lw = lw_ref[0, rows, :]
    cl = _dot_exact01(lw, tri, left=True)
    tot = cl[last:last + 1, :]
    e_cl = jnp.exp(cl)
    e_cle = jnp.exp(cl - lw)
    e_ncl = jnp.exp(-cl)
    e_tc = jnp.exp(tot - cl)
    e_tot = jnp.exp(tot)
    a_t = -kk_ref[rows, :] * e_cle
    r_t = r_ref[rows, :] * e_cl
    kd = kd_ref[0, rows, :]
    bd = bd_ref[0, rows, :]
    b_t = bd * e_ncl
    k_t = kd * e_ncl
    b_h = bd * e_tc
    k_h = kd * e_tc
    v = v_ref[rows, :]
    probs = []
    for gi in range(HEADS // GROUP):
        ln = slice(gi * GW, (gi + 1) * GW)
        r_s = _bd_stack(r_t[:, ln], bdm)
        probs.append(dict(
            r_s=r_s, r_sb=r_s.astype(BF16),
            a_s=_bd_stack(a_t[:, ln], bdm).astype(BF16),
            b_s=_bd_stack(b_t[:, ln], bdm).astype(BF16),
            k_s=_bd_stack(k_t[:, ln], bdm).astype(BF16),
            bh_s=_bd_stack(b_h[:, ln], bdm).astype(BF16),
            kh_s=_bd_stack(k_h[:, ln], bdm).astype(BF16),
            v_s=_bd_stack(v[:, ln], bdm).astype(BF16),
            e_tot=e_tot[:, ln], strict=strict, incl=incl, h_ref=h_ref, gi=gi))
    return probs


def _scan_solve(probs, eye):
    for p in probs:
        p["a_ab"] = _mm_nt(p["a_s"], p["b_s"]) * p["strict"]
    for p in probs:
        p["a_ak"] = _mm_nt(p["a_s"], p["k_s"]) * p["strict"]
    for p in probs:
        p["a_rb"] = (_mm_nt(p["r_sb"], p["b_s"]) * p["incl"]).astype(BF16)
    for p in probs:
        p["a_rk"] = _mm_nt(p["r_sb"], p["k_s"]) * p["incl"]
    for p in probs:
        p["t"] = eye + p["a_ab"]
        p["pw"] = p["a_ab"]
        p["x"] = _mm(p["a_ak"], p["v_s"])
    for _ in range(5):
        for p in probs:
            p["pw"] = _mm(p["pw"], p["pw"])
        for p in probs:
            p["t"] = p["t"] + _mm(p["t"], p["pw"])
    for p in probs:
        p["t"] = p["t"].astype(BF16)
        p["abar"] = _mm(p["t"], p["a_s"]).astype(BF16)
    for p in probs:
        p["u0"] = _mm(p["t"], p["x"]).astype(BF16)
    for p in probs:
        p["m"] = eye * p["e_tot"] + _mm_tn(p["bh_s"], p["abar"])
    for p in probs:
        p["g"] = _mm_tn(p["bh_s"], p["u0"]) + _mm_tn(p["kh_s"], p["v_s"])
    for p in probs:
        p["rbar"] = p["r_s"] + _mm(p["a_rb"], p["abar"])
    for p in probs:
        p["y0"] = _mm(p["a_rb"], p["u0"]) + _mm(p["a_rk"], p["v_s"])
    ys = []
    for p in probs:
        h = p["h_ref"][p["gi"]].astype(BF16)
        y_bd = _mm(p["rbar"], h) + p["y0"]
        p["h_ref"][p["gi"]] = _mm(p["m"], h) + p["g"]
        y = y_bd[0:CHUNK, :]
        for hh in range(1, GROUP):
            y = y + y_bd[hh * CHUNK:(hh + 1) * CHUNK, :]
        ys.append(y)
    return ys


def _rwkv_scan_kernel(rf_ref, vf_ref, kkf_ref, lwf_ref, kdf_ref, bdf_ref, rb_ref, vb_ref, kkb_ref, lwb_ref, kdb_ref, bdb_ref,
                      trif_ref, trib_ref, strictf_ref, strictb_ref, inclf_ref, inclb_ref, bdm_ref, eye_ref,
                      yf_ref, yb_ref, hf_ref, hb_ref):
    @pl.when(pl.program_id(0) == 0)
    def _():
        hf_ref[...] = jnp.zeros_like(hf_ref)
        hb_ref[...] = jnp.zeros_like(hb_ref)

    n_chunks = TILE // CHUNK
    bdm = bdm_ref[...]
    eye = eye_ref[...]

    def chunk_body(ci, carry):
        rows_f = pl.ds(pl.multiple_of(ci * CHUNK, CHUNK), CHUNK)
        rows_b = pl.ds(pl.multiple_of((n_chunks - 1 - ci) * CHUNK, CHUNK), CHUNK)
        probs = _scan_operands(False, rows_f, rf_ref, vf_ref, kkf_ref, lwf_ref, kdf_ref, bdf_ref,
                               trif_ref[...], strictf_ref[...], inclf_ref[...], bdm, hf_ref)
        probs += _scan_operands(True, rows_b, rb_ref, vb_ref, kkb_ref, lwb_ref, kdb_ref, bdb_ref,
                                trib_ref[...], strictb_ref[...], inclb_ref[...], bdm, hb_ref)
        ys = _scan_solve(probs, eye)
        n_groups = HEADS // GROUP
        yf_ref[rows_f, :] = jnp.concatenate(ys[:n_groups], axis=1)
        yb_ref[rows_b, :] = jnp.concatenate(ys[n_groups:], axis=1)
        return carry

    lax.fori_loop(0, n_chunks, chunk_body, 0)


def _scan_tile(reverse, n_lat_tiles):
    if reverse:
        return lambda j: jnp.where(j == 0, n_lat_tiles, n_lat_tiles - j)
    return lambda j: jnp.where(j == 0, n_lat_tiles, j - 1)


def _rwkv_scan_call(r, v, kk, lw, kd, bd, consts, n_lat_tiles):
    n = r.shape[0]
    specs = []
    for d, reverse in enumerate((False, True)):
        tile = _scan_tile(reverse, n_lat_tiles)
        row = pl.BlockSpec((TILE, WIDTH), lambda j, tile=tile: (tile(j), 0))
        dir_row = pl.BlockSpec((1, TILE, WIDTH), lambda j, tile=tile, d=d: (d, tile(j), 0))
        specs.append((row, dir_row))
    (row_f, dir_f), (row_b, dir_b) = specs
    cs = [consts["tri_fwd"], consts["tri_rev"], consts["strict_fwd"], consts["strict_rev"],
          consts["incl_fwd"], consts["incl_rev"], consts["bd_mask"], consts["eye"]]
    state = pltpu.VMEM((HEADS // GROUP, GW, GW), F32)
    return pl.pallas_call(
        _rwkv_scan_kernel,
        grid=(n // TILE,),
        in_specs=[row_f] * 3 + [dir_f] * 3 + [row_b] * 3 + [dir_b] * 3 + [_const_spec(a.shape) for a in cs],
        out_specs=[row_f, row_b],
        out_shape=[jax.ShapeDtypeStruct((n, WIDTH), F32)] * 2,
        scratch_shapes=[state, state],
        compiler_params=_params(1),
        name="rwkv_scan",
    )(r, v, kk, lw, kd, bd, r, v, kk, lw, kd, bd, *cs)


def _lru_kernel(reverse, seg, cur_ref, prev_ref, next_ref, cw_ref, cb_ref, wg_ref, bg_ref, lam_ref,
                hs_ref, a_scr, b_scr, h_scr):
    j = pl.program_id(0)

    @pl.when(j == 0)
    def _():
        h_scr[...] = jnp.zeros_like(h_scr)

    tile_idx = cur_tile_index(reverse, seg, j)
    lvalid, rvalid = _halo_valid(tile_idx, TILE, seg[0], seg[1])
    cur = cur_ref[...]
    prev = prev_ref[...] * lvalid
    x2 = _shift_down(cur, prev[HALO - 2:HALO, :], 2)
    x1 = _shift_down(cur, prev[HALO - 1:HALO, :], 1)
    xn = _shift_up(cur, next_ref[0:1, :] * rvalid)
    xb = cb_ref[...] + x2 * cw_ref[0:1, :] + x1 * cw_ref[1:2, :] + cur * cw_ref[2:3, :] + xn * cw_ref[3:4, :]
    gates = jnp.dot(xb.astype(BF16), wg_ref[0], preferred_element_type=F32) + bg_ref[0]
    gate_r = _sigmoid(gates[:, 0:WIDTH])
    gate_i = _sigmoid(gates[:, WIDTH:2 * WIDTH])
    lam = lam_ref[0]
    softplus = jnp.maximum(-lam, 0.0) + jnp.log(1.0 + jnp.exp(-jnp.abs(lam)))
    log_a = -LRU_C * gate_r * softplus
    a = jnp.exp(log_a)
    b = jnp.sqrt(1.0 - a * a) * gate_i * xb

    in_group = _row_iota((TILE, WIDTH)) % 8
    for s in (1, 2, 4):
        if reverse:
            a_sh = pltpu.roll(a, TILE - s, 0)
            b_sh = pltpu.roll(b, TILE - s, 0)
            ok = in_group < 8 - s
        else:
            a_sh = pltpu.roll(a, s, 0)
            b_sh = pltpu.roll(b, s, 0)
            ok = in_group >= s
        b = jnp.where(ok, a * b_sh + b, b)
        a = jnp.where(ok, a * a_sh, a)
    a_scr[...] = a
    b_scr[...] = b

    n_groups = TILE // 8

    def group_body(gi, h):
        g = (n_groups - 1 - gi) if reverse else gi
        rows = pl.ds(pl.multiple_of(g * 8, 8), 8)
        hs = a_scr[rows, :] * h + b_scr[rows, :]
        hs_ref[rows, :] = hs
        return hs[0:1, :] if reverse else hs[7:8, :]

    h_scr[...] = lax.fori_loop(0, n_groups, group_body, h_scr[...], unroll=4)


def cur_tile_index(reverse, seg, j):
    n_lat_tiles = seg[1][0] // TILE
    return _scan_tile(reverse, n_lat_tiles)(j)


def _lru_call(reverse, lru, p, seg):
    n = lru.shape[0]
    d = 1 if reverse else 0
    hb = TILE // HALO
    nb = n // HALO
    n_lat_tiles = seg[1][0] // TILE
    tile = _scan_tile(reverse, n_lat_tiles)
    dsel = lambda shape: pl.BlockSpec((1,) + shape, lambda j: (d, 0, 0), pipeline_mode=pl.Buffered(1))
    return pl.pallas_call(
        functools.partial(_lru_kernel, reverse, seg),
        grid=(n // TILE,),
        in_specs=[
            pl.BlockSpec((TILE, WIDTH), lambda j: (tile(j), 0)),
            pl.BlockSpec((HALO, WIDTH), lambda j: (jnp.maximum(tile(j) * hb - 1, 0), 0)),
            pl.BlockSpec((HALO, WIDTH), lambda j: (jnp.minimum((tile(j) + 1) * hb, nb - 1), 0)),
            _const_spec(p["lru_conv"].shape),
            _const_spec(p["lru_conv_b"].shape),
            dsel((WIDTH, 2 * WIDTH)),
            dsel((1, 2 * WIDTH)),
            dsel((1, WIDTH)),
        ],
        out_specs=pl.BlockSpec((TILE, WIDTH), lambda j: (tile(j), 0)),
        out_shape=jax.ShapeDtypeStruct((n, WIDTH), F32),
        scratch_shapes=[pltpu.VMEM((TILE, WIDTH), F32), pltpu.VMEM((TILE, WIDTH), F32), pltpu.VMEM((1, WIDTH), F32)],
        compiler_params=_params(1),
        name="lru_rev" if reverse else "lru_fwd",
    )(lru, lru, lru, p["lru_conv"], p["lru_conv_b"], p["lru_wg"], p["lru_bg"], p["lru_lam"])


def _mla_proj_kernel(cols_ref, cc_ref, ss_ref, cs_ref, gq_ref, gkv_ref, wq_ref, wkv_ref, q_ref, k_ref, v_ref):
    cols = cols_ref[...]
    qn = _rms(cols[:, 0:Q_RANK], gq_ref[...]).astype(BF16)
    q = jnp.dot(qn, wq_ref[...], preferred_element_type=F32)
    kvn = _rms(cols[:, Q_RANK:Q_RANK + KV_RANK], gkv_ref[...]).astype(BF16)
    kv = jnp.dot(kvn, wkv_ref[...], preferred_element_type=F32)
    kr = cols[:, Q_RANK + KV_RANK:MLA_COLS] * cs_ref[...]
    kr = (kr + pltpu.roll(kr, ROPE, 1)).astype(BF16)
    lane = lax.broadcasted_iota(jnp.int32, (TILE, 2 * ROPE), 1)
    ones_col = jnp.where(lax.broadcasted_iota(jnp.int32, (TILE, V_SLOT - V_DIM), 1) == 0, 1.0, 0.0).astype(BF16)
    cc = cc_ref[...]
    ss = ss_ref[...]
    for hp in range(HEADS // 2):
        sl = slice(HEADS * NOPE + hp * 2 * ROPE, HEADS * NOPE + (hp + 1) * 2 * ROPE)
        sw = slice(HEADS * NOPE + HEADS * ROPE + hp * 2 * ROPE, HEADS * NOPE + HEADS * ROPE + (hp + 1) * 2 * ROPE)
        roped = (q[:, sl] * cc + q[:, sw] * ss) * Q_SCALE
        for e in range(2):
            h = 2 * hp + e
            q_ref[h, :, 0:NOPE] = (q[:, h * NOPE:(h + 1) * NOPE] * Q_SCALE).astype(BF16)
            keep = (lane < ROPE) if e == 0 else (lane >= ROPE)
            q_ref[h, :, NOPE:QK_DIM] = jnp.where(keep, roped, 0.0).astype(BF16)
            k_ref[h, :, 0:NOPE] = kv[:, h * 2 * NOPE:h * 2 * NOPE + NOPE].astype(BF16)
            k_ref[h, :, NOPE:QK_DIM] = kr
            v_ref[h, :, 0:V_DIM] = kv[:, h * 2 * NOPE + NOPE:(h + 1) * 2 * NOPE].astype(BF16)
            v_ref[h, :, V_DIM:V_SLOT] = ones_col


def _mla_proj_call(mla, tabs, p):
    n = mla.shape[0]
    row = lambda c: pl.BlockSpec((TILE, c), lambda j: (j, 0))
    head = lambda c: pl.BlockSpec((HEADS, TILE, c), lambda j: (0, j, 0))
    consts = [p["q_norm"], p["kv_norm"], p["w_q"], p["w_kv"]]
    return pl.pallas_call(
        _mla_proj_kernel,
        grid=(n // TILE,),
        in_specs=[row(MLA_COLS), row(2 * ROPE), row(2 * ROPE), row(2 * ROPE)] + [_const_spec(a.shape) for a in consts],
        out_specs=[head(QK_DIM), head(QK_DIM), head(V_SLOT)],
        out_shape=[jax.ShapeDtypeStruct((HEADS, n, QK_DIM), BF16), jax.ShapeDtypeStruct((HEADS, n, QK_DIM), BF16),
                   jax.ShapeDtypeStruct((HEADS, n, V_SLOT), BF16)],
        compiler_params=_params(1),
        name="mla_proj",
    )(mla, tabs["cc"], tabs["ss"], tabs["cs"], *consts)


def _attn_kernel(kv_start, kv_chunk, n_kv, *refs):
    q_ref, k_ref, v_ref = refs[0:3]
    (o_ref, sa0, sa1, sb0, sb1, pb_scr, alpha_b, m_a, acc_a, m_b, acc_b) = refs[-11:]
    tq = acc_a.shape[0]
    for m_scr, acc_scr in ((m_a, acc_a), (m_b, acc_b)):
        m_scr[...] = jnp.full(m_scr.shape, -jnp.inf, F32)
        acc_scr[...] = jnp.zeros_like(acc_scr)
    q_a = (0, tq)
    q_b = (tq, 2 * tq)
    s_a = (sa0, sa1)
    s_b = (sb0, sb1)

    def chunk_rows(ci):
        return pl.ds(pl.multiple_of(kv_start + ci * kv_chunk, kv_chunk), kv_chunk)

    def scores(q_rows, ci, s_ref):
        q = q_ref[0, q_rows[0]:q_rows[1], :]
        s_ref[...] = lax.dot_general(q, k_ref[0, chunk_rows(ci), :], (((1,), (1,)), ((), ())),
                                     preferred_element_type=F32)

    def softmax(s_ref, m_scr):
        s = s_ref[...]
        m_old = m_scr[...]
        m_new = jnp.maximum(m_old, jnp.max(s, axis=-1, keepdims=True))
        alpha = jnp.exp2(m_old - m_new)
        pr = jnp.exp2((s - m_new).astype(BF16))
        m_scr[...] = m_new
        return pr, alpha

    def values(acc_scr, alpha, pr, ci):
        acc_scr[...] = alpha * acc_scr[...] + jnp.dot(pr, v_ref[0, chunk_rows(ci), :], preferred_element_type=F32)

    def step(ci, cur, first, last):
        if not first:
            values(acc_b, alpha_b[...], pb_scr[...], ci - 1)
        if not last:
            scores(q_a, ci + 1, s_a[1 - cur])
        pr, alpha = softmax(s_a[cur], m_a)
        values(acc_a, alpha, pr, ci)
        if not last:
            scores(q_b, ci + 1, s_b[1 - cur])
        pr, alpha = softmax(s_b[cur], m_b)
        pb_scr[...] = pr
        alpha_b[...] = alpha

    scores(q_a, 0, sa0)
    scores(q_b, 0, sb0)
    step(0, 0, True, n_kv == 1)
    if n_kv > 1:
        n_mid = n_kv - 2
        n_pairs = n_mid // 2

        def pair(t, carry):
            step(1 + 2 * t, 1, False, False)
            step(2 + 2 * t, 0, False, False)
            return carry

        lax.fori_loop(0, n_pairs, pair, 0)
        if n_mid % 2 == 1:
            step(n_kv - 2, (n_kv - 2) % 2, False, False)
        step(n_kv - 1, (n_kv - 1) % 2, False, True)
    values(acc_b, alpha_b[...], pb_scr[...], n_kv - 1)
    o_ref[0:tq, :] = (acc_a[:, 0:V_DIM] / acc_a[:, V_DIM:V_DIM + 1]).astype(o_ref.dtype)
    o_ref[tq:2 * tq, :] = (acc_b[:, 0:V_DIM] / acc_b[:, V_DIM:V_DIM + 1]).astype(o_ref.dtype)


def _attn_call(q, k, v, o_prev, q_tile, q_block0, n_q, kv_start, kv_chunk, n_kv):
    n = q.shape[1]
    tq = q_tile // 2
    in_specs = [
        pl.BlockSpec((1, q_tile, QK_DIM), lambda h, i: (h, q_block0 + i, 0)),
        pl.BlockSpec((1, n, QK_DIM), lambda h, i: (h, 0, 0)),
        pl.BlockSpec((1, n, V_SLOT), lambda h, i: (h, 0, 0)),
    ]
    args = [q, k, v]
    aliases = {}
    if o_prev is not None:
        in_specs.append(pl.BlockSpec(memory_space=pl.ANY))
        args.append(o_prev)
        aliases = {3: 0}
    return pl.pallas_call(
        functools.partial(_attn_kernel, kv_start, kv_chunk, n_kv),
        grid=(HEADS, n_q),
        in_specs=in_specs,
        out_specs=pl.BlockSpec((q_tile, V_DIM), lambda h, i: (q_block0 + i, h)),
        out_shape=jax.ShapeDtypeStruct((n, MLA_WIDTH), BF16),
        scratch_shapes=[pltpu.VMEM((tq, kv_chunk), F32)] * 4
                       + [pltpu.VMEM((tq, kv_chunk), BF16), pltpu.VMEM((tq, 1), F32)]
                       + [pltpu.VMEM((tq, 1), F32), pltpu.VMEM((tq, V_SLOT), F32)] * 2,
        input_output_aliases=aliases,
        compiler_params=_params(2),
        name="mla_attn",
    )(*args)


def _out_proj_kernel(x_ref, mod_ref, yf_ref, yb_ref, cv_ref, g_ref, hf_ref, hb_ref, gate_ref, om_ref,
                     lng_ref, lnb_ref, ones_ref, w_ref, o_ref):
    ones = ones_ref[...]
    y = yf_ref[...] + yb_ref[...]
    mu = _dot_exact01(y, ones) * (1.0 / HEAD_DIM)
    dlt = y - mu
    var = _dot_exact01(dlt * dlt, ones) * (1.0 / HEAD_DIM)
    yn = dlt * lax.rsqrt(var + GN_EPS) * lng_ref[...] + lnb_ref[...]
    o_rw = ((yn + cv_ref[...]) * g_ref[...]).astype(BF16)
    o_lru = ((hf_ref[...] + hb_ref[...]) * _gelu_tanh(gate_ref[...])).astype(BF16)
    acc = jnp.dot(o_rw, w_ref[0:WIDTH, :], preferred_element_type=F32)
    acc += jnp.dot(o_lru, w_ref[WIDTH:2 * WIDTH, :], preferred_element_type=F32)
    acc += jnp.dot(om_ref[...], w_ref[2 * WIDTH:, :], preferred_element_type=F32)
    o_ref[...] = x_ref[...] + mod_ref[0, 2:3, :] * acc


def _out_proj_call(x_all, mod, yf, yb, cv, g, hf, hb, lru, o_mla, p, n_tiles, n_lat_tiles):
    n = x_all.shape[0]
    row = lambda c: pl.BlockSpec((TILE, c), lambda j: (j, 0))
    consts = [p["ln_g"], p["ln_b"], p["ones_bd"], p["w_out"]]
    return pl.pallas_call(
        _out_proj_kernel,
        grid=(n_tiles,),
        in_specs=[row(D_MODEL), pl.BlockSpec((1, 6, D_MODEL), lambda j: (jnp.minimum(j // n_lat_tiles, 1), 0, 0))]
                 + [row(WIDTH)] * 6 + [pl.BlockSpec((TILE, WIDTH), lambda j: (j, 1)), row(MLA_WIDTH)]
                 + [_const_spec(a.shape) for a in consts],
        out_specs=row(D_MODEL),
        out_shape=jax.ShapeDtypeStruct((n, D_MODEL), F32),
        compiler_params=_params(1),
        name="out_proj",
    )(x_all, mod, yf, yb, cv, g, hf, hb, lru, o_mla, *consts)


def _ffn_kernel(tm, block0, seg_lo, seg_hi, mod_row, final, *refs):
    if final:
        (x_ref, prev_ref, next_ref, mod_ref, g_ref, wg_ref, wu_ref, wd_ref, cw_ref, cb_ref, fin_ref,
         o_ref, h_scr, acc_scr) = refs
    else:
        (x_ref, prev_ref, next_ref, mod_ref, g_ref, wg_ref, wu_ref, wd_ref, cw_ref, cb_ref,
         o_ref, h_scr, acc_scr) = refs
    c = pl.program_id(1)
    ext = tm + 2 * FFN_HALO

    @pl.when(c == 0)
    def _():
        sh = mod_ref[mod_row, 3:4, :]
        sc = 1.0 + mod_ref[mod_row, 4:5, :]
        g = g_ref[...]
        h_scr[0:FFN_HALO, :] = (_rms(prev_ref[...], g) * sc + sh).astype(BF16)
        h_scr[FFN_HALO:FFN_HALO + tm, :] = (_rms(x_ref[...], g) * sc + sh).astype(BF16)
        h_scr[FFN_HALO + tm:ext, :] = (_rms(next_ref[...], g) * sc + sh).astype(BF16)
        acc_scr[...] = jnp.zeros_like(acc_scr)

    ge = jnp.dot(h_scr[...], wg_ref[...], preferred_element_type=F32)
    grow = (block0 + pl.program_id(0)) * tm - FFN_HALO + _row_iota(ge.shape)
    ge = jnp.where(jnp.logical_and(grow >= seg_lo, grow < seg_hi), ge, 0.0)
    up_rows = pltpu.roll(ge, 1, 0)[FFN_HALO:FFN_HALO + tm, :]
    dn_rows = pltpu.roll(ge, ext - 1, 0)[FFN_HALO:FFN_HALO + tm, :]
    gate = cb_ref[...] + up_rows * cw_ref[0:1, :] + ge[FFN_HALO:FFN_HALO + tm, :] * cw_ref[1:2, :] + dn_rows * cw_ref[2:3, :]
    up = jnp.dot(h_scr[FFN_HALO:FFN_HALO + tm, :], wu_ref[...], preferred_element_type=F32)
    act = (_silu(gate) * up).astype(BF16)
    acc_scr[...] += jnp.dot(act, wd_ref[...], preferred_element_type=F32)

    @pl.when(c == pl.num_programs(1) - 1)
    def _():
        out = x_ref[...] + mod_ref[mod_row, 5:6, :] * acc_scr[...]
        if final:
            out = _rms(out, fin_ref[...])
        o_ref[...] = out


def _ffn_call(x_all, o_prev, mod, p, tm, block0, n_blocks, seg_lo, seg_hi, mod_row, final_g, out_rows):
    n = x_all.shape[0]
    hb = tm // FFN_HALO
    nb = n // FFN_HALO
    n_chunks = D_FF // FFN_CHUNK
    final = final_g is not None
    in_specs = [
        pl.BlockSpec((tm, D_MODEL), lambda j, c: (block0 + j, 0)),
        pl.BlockSpec((FFN_HALO, D_MODEL), lambda j, c: (jnp.maximum((block0 + j) * hb - 1, 0), 0)),
        pl.BlockSpec((FFN_HALO, D_MODEL), lambda j, c: (jnp.minimum((block0 + j + 1) * hb, nb - 1), 0)),
        pl.BlockSpec((2, 6, D_MODEL), lambda j, c: (0, 0, 0)),
        pl.BlockSpec((1, D_MODEL), lambda j, c: (0, 0)),
        pl.BlockSpec((D_MODEL, FFN_CHUNK), lambda j, c: (0, c)),
        pl.BlockSpec((D_MODEL, FFN_CHUNK), lambda j, c: (0, c)),
        pl.BlockSpec((FFN_CHUNK, D_MODEL), lambda j, c: (c, 0)),
        pl.BlockSpec((3, FFN_CHUNK), lambda j, c: (0, c)),
        pl.BlockSpec((1, FFN_CHUNK), lambda j, c: (0, c)),
    ]
    args = [x_all, x_all, x_all, mod, p["norm2"], p["w_gate"], p["w_up"], p["w_down"], p["ffn_conv"], p["ffn_conv_b"]]
    if final:
        in_specs.append(pl.BlockSpec((1, D_MODEL), lambda j, c: (0, 0)))
        args.append(final_g)
    aliases = {}
    if o_prev is not None:
        in_specs.append(pl.BlockSpec(memory_space=pl.ANY))
        args.append(o_prev)
        aliases = {len(args) - 1: 0}
    kern = functools.partial(_ffn_kernel, tm, block0, seg_lo, seg_hi, mod_row, final)
    if o_prev is not None:
        kern = _drop_ref(kern, len(args) - 1)
    return pl.pallas_call(
        kern,
        grid=(n_blocks, n_chunks),
        in_specs=in_specs,
        out_specs=pl.BlockSpec((tm, D_MODEL), lambda j, c: (block0 + j, 0)),
        out_shape=jax.ShapeDtypeStruct((out_rows, D_MODEL), F32),
        scratch_shapes=[pltpu.VMEM((tm + 2 * FFN_HALO, D_MODEL), BF16), pltpu.VMEM((tm, D_MODEL), F32)],
        input_output_aliases=aliases,
        compiler_params=_params(2),
        name="conv_ffn",
    )(*args)


def _drop_ref(kern, idx):
    def wrapped(*refs):
        return kern(*(refs[:idx] + refs[idx + 1:]))
    return wrapped


_ROPE_PERM = np.concatenate([np.arange(16, 32), np.arange(0, 16), np.arange(48, 64), np.arange(32, 48)])


def _block_diag(blocks):
    h, n, m = blocks.shape
    eye = jnp.eye(h, dtype=blocks.dtype)
    return (eye[:, None, :, None] * blocks[:, :, None, :]).reshape(h * n, h * m)


def _scan_consts():
    idx = np.arange(CHUNK)
    lower = (idx[None, :] <= idx[:, None]).astype(np.float32)
    upper = (idx[None, :] >= idx[:, None]).astype(np.float32)
    eye_g = np.eye(GROUP, dtype=np.float32)
    bd = lambda m: np.kron(eye_g, m)
    return {
        "tri_fwd": jnp.asarray(lower, BF16), "tri_rev": jnp.asarray(upper, BF16),
        "incl_fwd": jnp.asarray(bd(lower)), "incl_rev": jnp.asarray(bd(upper)),
        "strict_fwd": jnp.asarray(bd(lower - np.eye(CHUNK, dtype=np.float32))),
        "strict_rev": jnp.asarray(bd(upper - np.eye(CHUNK, dtype=np.float32))),
        "bd_mask": jnp.asarray(bd(np.ones((CHUNK, HEAD_DIM), np.float32))),
        "eye": jnp.asarray(np.eye(GW, dtype=np.float32)),
    }


def _rope_tables(n_lat, n_ctx):
    n_freq = ROPE // 4
    rows = n_lat // GRID_W
    row = jnp.repeat(jnp.arange(rows, dtype=F32), GRID_W)
    col = jnp.tile(jnp.arange(GRID_W, dtype=F32), rows)
    inv_freq = ROPE_THETA ** (-jnp.arange(n_freq, dtype=F32) / n_freq)
    ar, ac = row[:, None] * inv_freq, col[:, None] * inv_freq
    cos = jnp.concatenate([jnp.cos(ar), jnp.cos(ar), jnp.cos(ac), jnp.cos(ac)], axis=-1)
    sin = jnp.concatenate([-jnp.sin(ar), jnp.sin(ar), -jnp.sin(ac), jnp.sin(ac)], axis=-1)
    cos = jnp.concatenate([cos, jnp.ones((n_ctx, ROPE), F32)], axis=0)
    sin = jnp.concatenate([sin, jnp.zeros((n_ctx, ROPE), F32)], axis=0)
    return {"cc": jnp.concatenate([cos, cos], axis=-1), "ss": jnp.concatenate([sin, sin], axis=-1),
            "cs": jnp.concatenate([cos, sin], axis=-1)}


def kernel(x, c, ctx, c_ctx, ada_w, ada_b, norm1, norm2, w_in, w_out, rw_conv, rw_conv_b, rw_w0, rw_w_up, rw_a0, rw_a_up, rw_g_up, rw_k_k, rw_k_a, rw_r_k, rw_ln_g, rw_ln_b, rw_v0, rw_v_down, rw_v_up, lru_conv, lru_conv_b, lru_wa, lru_ba, lru_wx, lru_bx, lru_lambda, mla_q_norm, mla_w_qb, mla_kv_norm, mla_w_kvb, ffn_w_gate, ffn_w_up, ffn_conv, ffn_conv_b, ffn_w_down, final_norm):
    assert x.shape[0] == 1 and ctx.shape[0] == 1
    depth = ada_w.shape[0]
    n_lat, n_ctx = x.shape[1], ctx.shape[1]
    assert n_lat % FFN_TILE == 0 and n_ctx % TILE == 0 and n_lat % GRID_W == 0
    n = n_lat + n_ctx
    n_lat_tiles = n_lat // TILE
    seg = ((0, n_lat), (n_lat, n))

    x_all = jnp.concatenate([x[0], ctx[0]], axis=0)
    mods = _mod_call(jnp.stack([c[0], c_ctx], axis=1), ada_w, ada_b).reshape(depth, 2, 6, D_MODEL)
    tabs = _rope_tables(n_lat, n_ctx)
    consts = _scan_consts()
    ones_bd = jnp.asarray(np.kron(np.eye(HEADS, dtype=np.float32), np.ones((HEAD_DIM, HEAD_DIM), np.float32)), BF16)
    kv_chunk = 1280 if n % 1280 == 0 else TILE
    row2 = lambda a: a.reshape(1, -1)

    v_first = None
    out = None
    for i in range(depth):
        last = i == depth - 1
        mla_off = 1760 + LRU_COLS
        rope_cols = mla_off + Q_RANK + KV_RANK + _ROPE_PERM
        vdown = rw_v_down[i - 1] if i > 0 else jnp.zeros((D_MODEL, LORA), F32)
        w_cat = jnp.concatenate([w_in[i][:, :1760], vdown, w_in[i][:, 1760:], w_in[i][:, rope_cols]], axis=1).astype(BF16)

        w_lora = jnp.zeros((RW_COLS - 3 * WIDTH, 6 * WIDTH), F32)
        b_lora = jnp.zeros((6 * WIDTH,), F32)
        for d in range(2):
            w_lora = w_lora.at[d * LORA:(d + 1) * LORA, d * WIDTH:(d + 1) * WIDTH].set(rw_w_up[i][d])
            w_lora = w_lora.at[(2 + d) * LORA:(3 + d) * LORA, (2 + d) * WIDTH:(3 + d) * WIDTH].set(rw_a_up[i][d])
            b_lora = b_lora.at[d * WIDTH:(d + 1) * WIDTH].set(rw_w0[i][d])
            b_lora = b_lora.at[(2 + d) * WIDTH:(3 + d) * WIDTH].set(rw_a0[i][d])
        w_lora = w_lora.at[4 * LORA:4 * LORA + GATE_LORA, 4 * WIDTH:5 * WIDTH].set(rw_g_up[i])
        if i > 0:
            w_lora = w_lora.at[4 * LORA + GATE_LORA:, 5 * WIDTH:].set(rw_v_up[i - 1])
            b_lora = b_lora.at[5 * WIDTH:].set(rw_v0[i - 1])
        ident = jnp.array([[0.0], [1.0], [0.0]], F32) * jnp.ones((1, LORA), F32)
        rw_p = {
            "rw_conv": jnp.concatenate([rw_conv[i], ident], axis=1),
            "rw_conv_b": row2(jnp.concatenate([rw_conv_b[i], jnp.zeros((LORA,), F32)])),
            "w_lora": w_lora.astype(BF16), "b_lora": row2(b_lora),
            "k_k": row2(rw_k_k[i]), "k_a": row2(rw_k_a[i]), "r_k": row2(rw_r_k[i]), "ones_bd": ones_bd,
        }
        lru_p = {
            "lru_conv": lru_conv[i], "lru_conv_b": row2(lru_conv_b[i]),
            "lru_wg": jnp.stack([jnp.concatenate([_block_diag(lru_wa[i][d]), _block_diag(lru_wx[i][d])], axis=1)
                                 for d in range(2)]).astype(BF16),
            "lru_bg": jnp.stack([jnp.concatenate([lru_ba[i][d], lru_bx[i][d]])[None] for d in range(2)]),
            "lru_lam": lru_lambda[i][:, None, :],
        }
        wq = mla_w_qb[i].reshape(Q_RANK, HEADS, NOPE + ROPE)
        mla_p = {
            "q_norm": row2(mla_q_norm[i]), "kv_norm": row2(mla_kv_norm[i]),
            "w_q": jnp.concatenate([wq[:, :, :NOPE].reshape(Q_RANK, -1), wq[:, :, NOPE:].reshape(Q_RANK, -1),
                                    wq[:, :, NOPE + _ROPE_PERM].reshape(Q_RANK, -1)], axis=1).astype(BF16),
            "w_kv": mla_w_kvb[i].astype(BF16),
        }
        out_p = {"ln_g": row2(rw_ln_g[i]), "ln_b": row2(rw_ln_b[i]), "ones_bd": ones_bd, "w_out": w_out[i].astype(BF16)}
        ffn_p = {"norm2": row2(norm2[i]), "w_gate": ffn_w_gate[i].astype(BF16), "w_up": ffn_w_up[i].astype(BF16),
                 "w_down": ffn_w_down[i].astype(BF16), "ffn_conv": ffn_conv[i], "ffn_conv_b": row2(ffn_conv_b[i])}

        rw, lru, mla = _in_proj_call(x_all, mods[i], row2(norm1[i]), w_cat, n_lat_tiles)

        r, v, kk, g, cv, lw, kd, bd = _rwkv_prep_call(rw, v_first, rw_p, seg)
        if i == 0:
            v_first = v
        y_f, y_b = _rwkv_scan_call(r, v, kk, lw, kd, bd, consts, n_lat_tiles)

        h_f = _lru_call(False, lru, lru_p, seg)
        h_b = _lru_call(True, lru, lru_p, seg)

        q_h, k_h, v_h = _mla_proj_call(mla, tabs, mla_p)
        o_mla = _attn_call(q_h, k_h, v_h, None, Q_TILE, 0, n_lat // Q_TILE, 0, kv_chunk, n // kv_chunk)
        if not last:
            o_mla = _attn_call(q_h, k_h, v_h, o_mla, TILE, n_lat_tiles, n_ctx // TILE, n_lat, TILE, n_ctx // TILE)

        n_tiles = n_lat_tiles if last else n // TILE
        x_mid = _out_proj_call(x_all, mods[i], y_f, y_b, cv, g, h_f, h_b, lru, o_mla, out_p, n_tiles, n_lat_tiles)

        if last:
            out = _ffn_call(x_mid, None, mods[i], ffn_p, FFN_TILE, 0, n_lat // FFN_TILE, 0, n_lat, 0,
                            row2(final_norm), n_lat)
        else:
            x_new = _ffn_call(x_mid, None, mods[i], ffn_p, FFN_TILE, 0, n_lat // FFN_TILE, 0, n_lat, 0, None, n)
            x_all = _ffn_call(x_mid, x_new, mods[i], ffn_p, TILE, n_lat_tiles, n_ctx // TILE, n_lat, n, 1, None, n)
    return out[None]
```

```python
import functools

import numpy as np
import jax
import jax.numpy as jnp
from jax import lax
from jax.experimental import pallas as pl
from jax.experimental.pallas import tpu as pltpu

F32 = jnp.float32
BF16 = jnp.bfloat16

D_MODEL = 2048
NORM_EPS = 1e-6
GN_EPS = 64e-5
DECAY_SCALE = 0.606531
LRU_C = 8.0
HEADS = 8
HEAD_DIM = 64
WIDTH = HEADS * HEAD_DIM
LORA = 32
GATE_LORA = 96
RW_COLS = 1792
LRU_COLS = 1024
MLA_COLS = 896
Q_RANK = 512
KV_RANK = 256
NOPE = 128
ROPE = 64
V_DIM = 128
V_SLOT = 256
QK_DIM = 256
MLA_WIDTH = HEADS * V_DIM
MLA_SCALE = (NOPE + ROPE) ** -0.5
Q_SCALE = MLA_SCALE * 1.4426950408889634
ROPE_THETA = 10000.0
GRID_W = 64
D_FF = 5632

TILE = 256
CHUNK = 64
GROUP = 4
GW = GROUP * HEAD_DIM
HALO = 8
FFN_TILE = 1024
FFN_HALO = 16
FFN_CHUNK = 512
Q_TILE = 1024
VMEM_LIMIT = 56 * 1024 * 1024


def _params(n_axes, vmem=VMEM_LIMIT):
    return pltpu.CompilerParams(dimension_semantics=("arbitrary",) * n_axes, vmem_limit_bytes=vmem)


def _const_spec(shape):
    nd = len(shape)
    return pl.BlockSpec(shape, lambda *_: (0,) * nd, pipeline_mode=pl.Buffered(1))


def _sigmoid(x):
    return 1.0 / (1.0 + jnp.exp(-x))


def _silu(x):
    return x * _sigmoid(x)


def _gelu_tanh(x):
    return 0.5 * x * (1.0 + jnp.tanh(0.7978845608028654 * (x + 0.044715 * (x * x * x))))


def _rms(x, g):
    return x * lax.rsqrt(jnp.mean(x * x, axis=-1, keepdims=True) + NORM_EPS) * g


def _dot_exact01(x, w01, left=False):
    out = None
    rem = x
    for _ in range(3):
        part = rem.astype(BF16)
        rem = rem - part.astype(F32)
        term = jnp.dot(w01, part, preferred_element_type=F32) if left else jnp.dot(part, w01, preferred_element_type=F32)
        out = term if out is None else out + term
    return out


def _row_iota(shape):
    return lax.broadcasted_iota(jnp.int32, shape, 0)


def _shift_down(cur, prev_rows, k):
    out = pltpu.roll(cur, k, 0)
    rows = _row_iota(cur.shape)
    for i in range(k):
        out = jnp.where(rows == i, prev_rows[i:i + 1, :], out)
    return out


def _shift_up(cur, next_row):
    n = cur.shape[0]
    out = pltpu.roll(cur, n - 1, 0)
    return jnp.where(_row_iota(cur.shape) == n - 1, next_row, out)


def _halo_valid(j, tile, seg_starts, seg_ends):
    first = j * tile
    last = first + tile
    lvalid = jnp.logical_and(first != seg_starts[0], first != seg_starts[1])
    rvalid = jnp.logical_and(last != seg_ends[0], last != seg_ends[1])
    return lvalid.astype(F32), rvalid.astype(F32)


def _mod_kernel(cc_ref, w_ref, b_ref, o_ref):
    s = _silu(cc_ref[...])
    w = w_ref[0]
    b = b_ref[0]
    o_ref[0, 0:1, :] = jnp.sum(s[:, 0:1] * w, axis=0, keepdims=True) + b
    o_ref[0, 1:2, :] = jnp.sum(s[:, 1:2] * w, axis=0, keepdims=True) + b


def _mod_call(cc, ada_w, ada_b):
    depth, d, n6 = ada_w.shape
    tn = 1024
    return pl.pallas_call(
        _mod_kernel,
        grid=(depth, n6 // tn),
        in_specs=[
            pl.BlockSpec((d, 2), lambda i, j: (0, 0)),
            pl.BlockSpec((1, d, tn), lambda i, j: (i, 0, j)),
            pl.BlockSpec((1, 1, tn), lambda i, j: (i, 0, j)),
        ],
        out_specs=pl.BlockSpec((1, 2, tn), lambda i, j: (i, 0, j)),
        out_shape=jax.ShapeDtypeStruct((depth, 2, n6), F32),
        compiler_params=_params(2),
        name="adaln_mod",
    )(cc, ada_w, ada_b.reshape(depth, 1, n6))


def _in_proj_kernel(x_ref, mod_ref, g_ref, w_ref, rw_ref, lru_ref, mla_ref):
    h = _rms(x_ref[...], g_ref[...]) * (1.0 + mod_ref[0, 1:2, :]) + mod_ref[0, 0:1, :]
    hb = h.astype(BF16)
    rw_ref[...] = jnp.dot(hb, w_ref[:, 0:RW_COLS], preferred_element_type=F32)
    lru_ref[...] = jnp.dot(hb, w_ref[:, RW_COLS:RW_COLS + LRU_COLS], preferred_element_type=F32)
    mla_ref[...] = jnp.dot(hb, w_ref[:, RW_COLS + LRU_COLS:], preferred_element_type=F32)


def _in_proj_call(x_all, mod, g, w_cat, n_lat_tiles):
    n = x_all.shape[0]
    cols = w_cat.shape[1]
    row = lambda c: pl.BlockSpec((TILE, c), lambda j: (j, 0))
    return pl.pallas_call(
        _in_proj_kernel,
        grid=(n // TILE,),
        in_specs=[
            row(D_MODEL),
            pl.BlockSpec((1, 6, D_MODEL), lambda j: (jnp.minimum(j // n_lat_tiles, 1), 0, 0)),
            _const_spec((1, D_MODEL)),
            _const_spec((D_MODEL, cols)),
        ],
        out_specs=[row(RW_COLS), row(LRU_COLS), row(MLA_COLS)],
        out_shape=[jax.ShapeDtypeStruct((n, c), F32) for c in (RW_COLS, LRU_COLS, MLA_COLS)],
        compiler_params=_params(1),
        name="in_proj",
    )(x_all, mod, g, w_cat)


def _rwkv_prep_kernel(seg, has_vfirst, *refs):
    if has_vfirst:
        (cur_ref, prev_ref, next_ref, vf_ref, cw_ref, cb_ref, wl_ref, bl_ref, kk_ref, ka_ref, rk_ref, ones_ref,
         r_out, v_out, kk_out, g_out, cv_out, lw_out, kd_out, bd_out) = refs
    else:
        (cur_ref, prev_ref, next_ref, cw_ref, cb_ref, wl_ref, bl_ref, kk_ref, ka_ref, rk_ref, ones_ref,
         r_out, v_out, kk_out, g_out, cv_out, lw_out, kd_out, bd_out) = refs
    lvalid, rvalid = _halo_valid(pl.program_id(0), TILE, seg[0], seg[1])
    cur = cur_ref[...]
    up = _shift_down(cur, prev_ref[HALO - 1:HALO, :] * lvalid, 1)
    dn = _shift_up(cur, next_ref[0:1, :] * rvalid)
    u = cb_ref[...] + up * cw_ref[0:1, :] + cur * cw_ref[1:2, :] + dn * cw_ref[2:3, :]
    r = u[:, 0:WIDTH]
    k = u[:, WIDTH:2 * WIDTH]
    v = u[:, 2 * WIDTH:3 * WIDTH]
    blk = u[:, 3 * WIDTH:RW_COLS]
    lane = lax.broadcasted_iota(jnp.int32, blk.shape, 1)
    act = jnp.where(lane < 2 * LORA, jnp.tanh(blk),
                    jnp.where(jnp.logical_and(lane >= 4 * LORA, lane < 4 * LORA + GATE_LORA), _sigmoid(blk), blk))
    lo = jnp.dot(act.astype(BF16), wl_ref[...], preferred_element_type=F32) + bl_ref[...]
    g = lo[:, 4 * WIDTH:5 * WIDTH]
    if has_vfirst:
        mix = _sigmoid(lo[:, 5 * WIDTH:6 * WIDTH])
        v = v + (vf_ref[...] - v) * mix
    ones = ones_ref[...]
    kk = k * kk_ref[...]
    ss = _dot_exact01(kk * kk, ones)
    kk = kk * lax.rsqrt(jnp.maximum(ss, 1e-24))
    ksum = None
    for d in range(2):
        lw_out[d] = -DECAY_SCALE * _sigmoid(lo[:, d * WIDTH:(d + 1) * WIDTH])
        iclr = _sigmoid(lo[:, (2 + d) * WIDTH:(3 + d) * WIDTH])
        kd = k * (1.0 + (iclr - 1.0) * ka_ref[...])
        kd_out[d] = kd
        bd_out[d] = kk * iclr
        ksum = kd if ksum is None else ksum + kd
    coef = _dot_exact01(r * ksum * rk_ref[...], ones)
    r_out[...] = r
    v_out[...] = v
    kk_out[...] = kk
    g_out[...] = g
    cv_out[...] = coef * v


def _rwkv_prep_call(rw, v_first, p, seg):
    n = rw.shape[0]
    hb = TILE // HALO
    nb = n // HALO
    row = lambda c: pl.BlockSpec((TILE, c), lambda j: (j, 0))
    dir_row = pl.BlockSpec((2, TILE, WIDTH), lambda j: (0, j, 0))
    has_vf = v_first is not None
    in_specs = [
        row(RW_COLS),
        pl.BlockSpec((HALO, RW_COLS), lambda j: (jnp.maximum(j * hb - 1, 0), 0)),
        pl.BlockSpec((HALO, RW_COLS), lambda j: (jnp.minimum((j + 1) * hb, nb - 1), 0)),
    ]
    args = [rw, rw, rw]
    if has_vf:
        in_specs.append(row(WIDTH))
        args.append(v_first)
    consts = [p["rw_conv"], p["rw_conv_b"], p["w_lora"], p["b_lora"], p["k_k"], p["k_a"], p["r_k"], p["ones_bd"]]
    in_specs += [_const_spec(a.shape) for a in consts]
    args += consts
    return pl.pallas_call(
        functools.partial(_rwkv_prep_kernel, seg, has_vf),
        grid=(n // TILE,),
        in_specs=in_specs,
        out_specs=[row(WIDTH)] * 5 + [dir_row] * 3,
        out_shape=[jax.ShapeDtypeStruct((n, WIDTH), F32)] * 5 + [jax.ShapeDtypeStruct((2, n, WIDTH), F32)] * 3,
        compiler_params=_params(1),
        name="rwkv_prep",
    )(*args)


def _bd_stack(x, bd_mask):
    return jnp.concatenate([x] * GROUP, axis=0) * bd_mask


def _mm(a, b):
    return jnp.dot(a.astype(BF16), b.astype(BF16), preferred_element_type=F32)


def _mm_nt(a, b):
    return lax.dot_general(a.astype(BF16), b.astype(BF16), (((1,), (1,)), ((), ())), preferred_element_type=F32)


def _mm_tn(a, b):
    return lax.dot_general(a.astype(BF16), b.astype(BF16), (((0,), (0,)), ((), ())), preferred_element_type=F32)


def _scan_operands(reverse, rows, r_ref, v_ref, kk_ref, lw_ref, kd_ref, bd_ref, tri, strict, incl, bdm, h_ref):
    last = 0 if reverse else CHUNK - 1
    lw = lw_ref[0, rows, :]
    cl = _dot_exact01(lw, tri, left=True)
    tot = cl[last:last + 1, :]
    e_cl = jnp.exp(cl)
    e_cle = jnp.exp(cl - lw)
    e_ncl = jnp.exp(-cl)
    e_tc = jnp.exp(tot - cl)
    e_tot = jnp.exp(tot)
    a_t = -kk_ref[rows, :] * e_cle
    r_t = r_ref[rows, :] * e_cl
    kd = kd_ref[0, rows, :]
    bd = bd_ref[0, rows, :]
    b_t = bd * e_ncl
    k_t = kd * e_ncl
    b_h = bd * e_tc
    k_h = kd * e_tc
    v = v_ref[rows, :]
    probs = []
    for gi in range(HEADS // GROUP):
        ln = slice(gi * GW, (gi + 1) * GW)
        r_s = _bd_stack(r_t[:, ln], bdm)
        probs.append(dict(
            r_s=r_s, r_sb=r_s.astype(BF16),
            a_s=_bd_stack(a_t[:, ln], bdm).astype(BF16),
            b_s=_bd_stack(b_t[:, ln], bdm).astype(BF16),
            k_s=_bd_stack(k_t[:, ln], bdm).astype(BF16),
            bh_s=_bd_stack(b_h[:, ln], bdm).astype(BF16),
            kh_s=_bd_stack(k_h[:, ln], bdm).astype(BF16),
            v_s=_bd_stack(v[:, ln], bdm).astype(BF16),
            e_tot=e_tot[:, ln], strict=strict, incl=incl, h_ref=h_ref, gi=gi))
    return probs


def _scan_solve(probs, eye):
    for p in probs:
        p["a_ab"] = _mm_nt(p["a_s"], p["b_s"]) * p["strict"]
    for p in probs:
        p["a_ak"] = _mm_nt(p["a_s"], p["k_s"]) * p["strict"]
    for p in probs:
        p["a_rb"] = (_mm_nt(p["r_sb"], p["b_s"]) * p["incl"]).astype(BF16)
    for p in probs:
        p["a_rk"] = _mm_nt(p["r_sb"], p["k_s"]) * p["incl"]
    for p in probs:
        p["t"] = eye + p["a_ab"]
        p["pw"] = p["a_ab"]
        p["x"] = _mm(p["a_ak"], p["v_s"])
    for _ in range(5):
        for p in probs:
            p["pw"] = _mm(p["pw"], p["pw"])
        for p in probs:
            p["t"] = p["t"] + _mm(p["t"], p["pw"])
    for p in probs:
        p["t"] = p["t"].astype(BF16)
        p["abar"] = _mm(p["t"], p["a_s"]).astype(BF16)
    for p in probs:
        p["u0"] = _mm(p["t"], p["x"]).astype(BF16)
    for p in probs:
        p["m"] = eye * p["e_tot"] + _mm_tn(p["bh_s"], p["abar"])
    for p in probs:
        p["g"] = _mm_tn(p["bh_s"], p["u0"]) + _mm_tn(p["kh_s"], p["v_s"])
    for p in probs:
        p["rbar"] = p["r_s"] + _mm(p["a_rb"], p["abar"])
    for p in probs:
        p["y0"] = _mm(p["a_rb"], p["u0"]) + _mm(p["a_rk"], p["v_s"])
    ys = []
    for p in probs:
        h = p["h_ref"][p["gi"]].astype(BF16)
        y_bd = _mm(p["rbar"], h) + p["y0"]
        p["h_ref"][p["gi"]] = _mm(p["m"], h) + p["g"]
        y = y_bd[0:CHUNK, :]
        for hh in range(1, GROUP):
            y = y + y_bd[hh * CHUNK:(hh + 1) * CHUNK, :]
        ys.append(y)
    return ys


def _rwkv_scan_kernel(rf_ref, vf_ref, kkf_ref, lwf_ref, kdf_ref, bdf_ref, rb_ref, vb_ref, kkb_ref, lwb_ref, kdb_ref, bdb_ref,
                      trif_ref, trib_ref, strictf_ref, strictb_ref, inclf_ref, inclb_ref, bdm_ref, eye_ref,
                      yf_ref, yb_ref, hf_ref, hb_ref):
    @pl.when(pl.program_id(0) == 0)
    def _():
        hf_ref[...] = jnp.zeros_like(hf_ref)
        hb_ref[...] = jnp.zeros_like(hb_ref)

    n_chunks = TILE // CHUNK
    bdm = bdm_ref[...]
    eye = eye_ref[...]

    def chunk_body(ci, carry):
        rows_f = pl.ds(pl.multiple_of(ci * CHUNK, CHUNK), CHUNK)
        rows_b = pl.ds(pl.multiple_of((n_chunks - 1 - ci) * CHUNK, CHUNK), CHUNK)
        probs = _scan_operands(False, rows_f, rf_ref, vf_ref, kkf_ref, lwf_ref, kdf_ref, bdf_ref,
                               trif_ref[...], strictf_ref[...], inclf_ref[...], bdm, hf_ref)
        probs += _scan_operands(True, rows_b, rb_ref, vb_ref, kkb_ref, lwb_ref, kdb_ref, bdb_ref,
                                trib_ref[...], strictb_ref[...], inclb_ref[...], bdm, hb_ref)
        ys = _scan_solve(probs, eye)
        n_groups = HEADS // GROUP
        yf_ref[rows_f, :] = jnp.concatenate(ys[:n_groups], axis=1)
        yb_ref[rows_b, :] = jnp.concatenate(ys[n_groups:], axis=1)
        return carry

    lax.fori_loop(0, n_chunks, chunk_body, 0)


def _scan_tile(reverse, n_lat_tiles):
    if reverse:
        return lambda j: jnp.where(j == 0, n_lat_tiles, n_lat_tiles - j)
    return lambda j: jnp.where(j == 0, n_lat_tiles, j - 1)


def _rwkv_scan_call(r, v, kk, lw, kd, bd, consts, n_lat_tiles):
    n = r.shape[0]
    specs = []
    for d, reverse in enumerate((False, True)):
        tile = _scan_tile(reverse, n_lat_tiles)
        row = pl.BlockSpec((TILE, WIDTH), lambda j, tile=tile: (tile(j), 0))
        dir_row = pl.BlockSpec((1, TILE, WIDTH), lambda j, tile=tile, d=d: (d, tile(j), 0))
        specs.append((row, dir_row))
    (row_f, dir_f), (row_b, dir_b) = specs
    cs = [consts["tri_fwd"], consts["tri_rev"], consts["strict_fwd"], consts["strict_rev"],
          consts["incl_fwd"], consts["incl_rev"], consts["bd_mask"], consts["eye"]]
    state = pltpu.VMEM((HEADS // GROUP, GW, GW), F32)
    return pl.pallas_call(
        _rwkv_scan_kernel,
        grid=(n // TILE,),
        in_specs=[row_f] * 3 + [dir_f] * 3 + [row_b] * 3 + [dir_b] * 3 + [_const_spec(a.shape) for a in cs],
        out_specs=[row_f, row_b],
        out_shape=[jax.ShapeDtypeStruct((n, WIDTH), F32)] * 2,
        scratch_shapes=[state, state],
        compiler_params=_params(1),
        name="rwkv_scan",
    )(r, v, kk, lw, kd, bd, r, v, kk, lw, kd, bd, *cs)


def _lru_kernel(reverse, seg, cur_ref, prev_ref, next_ref, cw_ref, cb_ref, wg_ref, bg_ref, lam_ref,
                hs_ref, a_scr, b_scr, h_scr):
    j = pl.program_id(0)

    @pl.when(j == 0)
    def _():
        h_scr[...] = jnp.zeros_like(h_scr)

    tile_idx = cur_tile_index(reverse, seg, j)
    lvalid, rvalid = _halo_valid(tile_idx, TILE, seg[0], seg[1])
    cur = cur_ref[...]
    prev = prev_ref[...] * lvalid
    x2 = _shift_down(cur, prev[HALO - 2:HALO, :], 2)
    x1 = _shift_down(cur, prev[HALO - 1:HALO, :], 1)
    xn = _shift_up(cur, next_ref[0:1, :] * rvalid)
    xb = cb_ref[...] + x2 * cw_ref[0:1, :] + x1 * cw_ref[1:2, :] + cur * cw_ref[2:3, :] + xn * cw_ref[3:4, :]
    gates = jnp.dot(xb.astype(BF16), wg_ref[0], preferred_element_type=F32) + bg_ref[0]
    gate_r = _sigmoid(gates[:, 0:WIDTH])
    gate_i = _sigmoid(gates[:, WIDTH:2 * WIDTH])
    lam = lam_ref[0]
    softplus = jnp.maximum(-lam, 0.0) + jnp.log(1.0 + jnp.exp(-jnp.abs(lam)))
    log_a = -LRU_C * gate_r * softplus
    a = jnp.exp(log_a)
    b = jnp.sqrt(1.0 - a * a) * gate_i * xb

    in_group = _row_iota((TILE, WIDTH)) % 8
    for s in (1, 2, 4):
        if reverse:
            a_sh = pltpu.roll(a, TILE - s, 0)
            b_sh = pltpu.roll(b, TILE - s, 0)
            ok = in_group < 8 - s
        else:
            a_sh = pltpu.roll(a, s, 0)
            b_sh = pltpu.roll(b, s, 0)
            ok = in_group >= s
        b = jnp.where(ok, a * b_sh + b, b)
        a = jnp.where(ok, a * a_sh, a)
    a_scr[...] = a
    b_scr[...] = b

    n_groups = TILE // 8

    def group_body(gi, h):
        g = (n_groups - 1 - gi) if reverse else gi
        rows = pl.ds(pl.multiple_of(g * 8, 8), 8)
        hs = a_scr[rows, :] * h + b_scr[rows, :]
        hs_ref[rows, :] = hs
        return hs[0:1, :] if reverse else hs[7:8, :]

    h_scr[...] = lax.fori_loop(0, n_groups, group_body, h_scr[...], unroll=4)


def cur_tile_index(reverse, seg, j):
    n_lat_tiles = seg[1][0] // TILE
    return _scan_tile(reverse, n_lat_tiles)(j)


def _lru_call(reverse, lru, p, seg):
    n = lru.shape[0]
    d = 1 if reverse else 0
    hb = TILE // HALO
    nb = n // HALO
    n_lat_tiles = seg[1][0] // TILE
    tile = _scan_tile(reverse, n_lat_tiles)
    dsel = lambda shape: pl.BlockSpec((1,) + shape, lambda j: (d, 0, 0), pipeline_mode=pl.Buffered(1))
    return pl.pallas_call(
        functools.partial(_lru_kernel, reverse, seg),
        grid=(n // TILE,),
        in_specs=[
            pl.BlockSpec((TILE, WIDTH), lambda j: (tile(j), 0)),
            pl.BlockSpec((HALO, WIDTH), lambda j: (jnp.maximum(tile(j) * hb - 1, 0), 0)),
            pl.BlockSpec((HALO, WIDTH), lambda j: (jnp.minimum((tile(j) + 1) * hb, nb - 1), 0)),
            _const_spec(p["lru_conv"].shape),
            _const_spec(p["lru_conv_b"].shape),
            dsel((WIDTH, 2 * WIDTH)),
            dsel((1, 2 * WIDTH)),
            dsel((1, WIDTH)),
        ],
        out_specs=pl.BlockSpec((TILE, WIDTH), lambda j: (tile(j), 0)),
        out_shape=jax.ShapeDtypeStruct((n, WIDTH), F32),
        scratch_shapes=[pltpu.VMEM((TILE, WIDTH), F32), pltpu.VMEM((TILE, WIDTH), F32), pltpu.VMEM((1, WIDTH), F32)],
        compiler_params=_params(1),
        name="lru_rev" if reverse else "lru_fwd",
    )(lru, lru, lru, p["lru_conv"], p["lru_conv_b"], p["lru_wg"], p["lru_bg"], p["lru_lam"])


def _mla_proj_kernel(cols_ref, cc_ref, ss_ref, cs_ref, gq_ref, gkv_ref, wq_ref, wkv_ref, q_ref, k_ref, v_ref):
    cols = cols_ref[...]
    qn = _rms(cols[:, 0:Q_RANK], gq_ref[...]).astype(BF16)
    q = jnp.dot(qn, wq_ref[...], preferred_element_type=F32)
    kvn = _rms(cols[:, Q_RANK:Q_RANK + KV_RANK], gkv_ref[...]).astype(BF16)
    kv = jnp.dot(kvn, wkv_ref[...], preferred_element_type=F32)
    kr = cols[:, Q_RANK + KV_RANK:MLA_COLS] * cs_ref[...]
    kr = (kr + pltpu.roll(kr, ROPE, 1)).astype(BF16)
    lane = lax.broadcasted_iota(jnp.int32, (TILE, 2 * ROPE), 1)
    ones_col = jnp.where(lax.broadcasted_iota(jnp.int32, (TILE, V_SLOT - V_DIM), 1) == 0, 1.0, 0.0).astype(BF16)
    cc = cc_ref[...]
    ss = ss_ref[...]
    for hp in range(HEADS // 2):
        sl = slice(HEADS * NOPE + hp * 2 * ROPE, HEADS * NOPE + (hp + 1) * 2 * ROPE)
        sw = slice(HEADS * NOPE + HEADS * ROPE + hp * 2 * ROPE, HEADS * NOPE + HEADS * ROPE + (hp + 1) * 2 * ROPE)
        roped = (q[:, sl] * cc + q[:, sw] * ss) * Q_SCALE
        for e in range(2):
            h = 2 * hp + e
            q_ref[h, :, 0:NOPE] = (q[:, h * NOPE:(h + 1) * NOPE] * Q_SCALE).astype(BF16)
            keep = (lane < ROPE) if e == 0 else (lane >= ROPE)
            q_ref[h, :, NOPE:QK_DIM] = jnp.where(keep, roped, 0.0).astype(BF16)
            k_ref[h, :, 0:NOPE] = kv[:, h * 2 * NOPE:h * 2 * NOPE + NOPE].astype(BF16)
            k_ref[h, :, NOPE:QK_DIM] = kr
            v_ref[h, :, 0:V_DIM] = kv[:, h * 2 * NOPE + NOPE:(h + 1) * 2 * NOPE].astype(BF16)
            v_ref[h, :, V_DIM:V_SLOT] = ones_col


def _mla_proj_call(mla, tabs, p):
    n = mla.shape[0]
    row = lambda c: pl.BlockSpec((TILE, c), lambda j: (j, 0))
    head = lambda c: pl.BlockSpec((HEADS, TILE, c), lambda j: (0, j, 0))
    consts = [p["q_norm"], p["kv_norm"], p["w_q"], p["w_kv"]]
    return pl.pallas_call(
        _mla_proj_kernel,
        grid=(n // TILE,),
        in_specs=[row(MLA_COLS), row(2 * ROPE), row(2 * ROPE), row(2 * ROPE)] + [_const_spec(a.shape) for a in consts],
        out_specs=[head(QK_DIM), head(QK_DIM), head(V_SLOT)],
        out_shape=[jax.ShapeDtypeStruct((HEADS, n, QK_DIM), BF16), jax.ShapeDtypeStruct((HEADS, n, QK_DIM), BF16),
                   jax.ShapeDtypeStruct((HEADS, n, V_SLOT), BF16)],
        compiler_params=_params(1),
        name="mla_proj",
    )(mla, tabs["cc"], tabs["ss"], tabs["cs"], *consts)


def _attn_kernel(kv_start, kv_chunk, n_kv, *refs):
    q_ref, k_ref, v_ref = refs[0:3]
    (o_ref, sa0, sa1, sb0, sb1, pb_scr, alpha_b, m_a, acc_a, m_b, acc_b) = refs[-11:]
    tq = acc_a.shape[0]
    for m_scr, acc_scr in ((m_a, acc_a), (m_b, acc_b)):
        m_scr[...] = jnp.full(m_scr.shape, -jnp.inf, F32)
        acc_scr[...] = jnp.zeros_like(acc_scr)
    q_a = (0, tq)
    q_b = (tq, 2 * tq)
    s_a = (sa0, sa1)
    s_b = (sb0, sb1)

    def chunk_rows(ci):
        return pl.ds(pl.multiple_of(kv_start + ci * kv_chunk, kv_chunk), kv_chunk)

    def scores(q_rows, ci, s_ref):
        q = q_ref[0, q_rows[0]:q_rows[1], :]
        s_ref[...] = lax.dot_general(q, k_ref[0, chunk_rows(ci), :], (((1,), (1,)), ((), ())),
                                     preferred_element_type=F32)

    def softmax(s_ref, m_scr):
        s = s_ref[...]
        m_old = m_scr[...]
        m_new = jnp.maximum(m_old, jnp.max(s, axis=-1, keepdims=True))
        alpha = jnp.exp2(m_old - m_new)
        pr = jnp.exp2((s - m_new).astype(BF16))
        m_scr[...] = m_new
        return pr, alpha

    def values(acc_scr, alpha, pr, ci):
        acc_scr[...] = alpha * acc_scr[...] + jnp.dot(pr, v_ref[0, chunk_rows(ci), :], preferred_element_type=F32)

    def step(ci, cur, first, last):
        if not first:
            values(acc_b, alpha_b[...], pb_scr[...], ci - 1)
        if not last:
            scores(q_a, ci + 1, s_a[1 - cur])
        pr, alpha = softmax(s_a[cur], m_a)
        values(acc_a, alpha, pr, ci)
        if not last:
            scores(q_b, ci + 1, s_b[1 - cur])
        pr, alpha = softmax(s_b[cur], m_b)
        pb_scr[...] = pr
        alpha_b[...] = alpha

    scores(q_a, 0, sa0)
    scores(q_b, 0, sb0)
    step(0, 0, True, n_kv == 1)
    if n_kv > 1:
        n_mid = n_kv - 2
        n_pairs = n_mid // 2

        def pair(t, carry):
            step(1 + 2 * t, 1, False, False)
            step(2 + 2 * t, 0, False, False)
            return carry

        lax.fori_loop(0, n_pairs, pair, 0)
        if n_mid % 2 == 1:
            step(n_kv - 2, (n_kv - 2) % 2, False, False)
        step(n_kv - 1, (n_kv - 1) % 2, False, True)
    values(acc_b, alpha_b[...], pb_scr[...], n_kv - 1)
    o_ref[0:tq, :] = (acc_a[:, 0:V_DIM] / acc_a[:, V_DIM:V_DIM + 1]).astype(o_ref.dtype)
    o_ref[tq:2 * tq, :] = (acc_b[:, 0:V_DIM] / acc_b[:, V_DIM:V_DIM + 1]).astype(o_ref.dtype)


def _attn_call(q, k, v, o_prev, q_tile, q_block0, n_q, kv_start, kv_chunk, n_kv):
    n = q.shape[1]
    tq = q_tile // 2
    in_specs = [
        pl.BlockSpec((1, q_tile, QK_DIM), lambda h, i: (h, q_block0 + i, 0)),
        pl.BlockSpec((1, n, QK_DIM), lambda h, i: (h, 0, 0)),
        pl.BlockSpec((1, n, V_SLOT), lambda h, i: (h, 0, 0)),
    ]
    args = [q, k, v]
    aliases = {}
    if o_prev is not None:
        in_specs.append(pl.BlockSpec(memory_space=pl.ANY))
        args.append(o_prev)
        aliases = {3: 0}
    return pl.pallas_call(
        functools.partial(_attn_kernel, kv_start, kv_chunk, n_kv),
        grid=(HEADS, n_q),
        in_specs=in_specs,
        out_specs=pl.BlockSpec((q_tile, V_DIM), lambda h, i: (q_block0 + i, h)),
        out_shape=jax.ShapeDtypeStruct((n, MLA_WIDTH), BF16),
        scratch_shapes=[pltpu.VMEM((tq, kv_chunk), F32)] * 4
                       + [pltpu.VMEM((tq, kv_chunk), BF16), pltpu.VMEM((tq, 1), F32)]
                       + [pltpu.VMEM((tq, 1), F32), pltpu.VMEM((tq, V_SLOT), F32)] * 2,
        input_output_aliases=aliases,
        compiler_params=_params(2),
        name="mla_attn",
    )(*args)


def _out_proj_kernel(x_ref, mod_ref, yf_ref, yb_ref, cv_ref, g_ref, hf_ref, hb_ref, gate_ref, om_ref,
                     lng_ref, lnb_ref, ones_ref, w_ref, o_ref):
    ones = ones_ref[...]
    y = yf_ref[...] + yb_ref[...]
    mu = _dot_exact01(y, ones) * (1.0 / HEAD_DIM)
    dlt = y - mu
    var = _dot_exact01(dlt * dlt, ones) * (1.0 / HEAD_DIM)
    yn = dlt * lax.rsqrt(var + GN_EPS) * lng_ref[...] + lnb_ref[...]
    o_rw = ((yn + cv_ref[...]) * g_ref[...]).astype(BF16)
    o_lru = ((hf_ref[...] + hb_ref[...]) * _gelu_tanh(gate_ref[...])).astype(BF16)
    acc = jnp.dot(o_rw, w_ref[0:WIDTH, :], preferred_element_type=F32)
    acc += jnp.dot(o_lru, w_ref[WIDTH:2 * WIDTH, :], preferred_element_type=F32)
    acc += jnp.dot(om_ref[...], w_ref[2 * WIDTH:, :], preferred_element_type=F32)
    o_ref[...] = x_ref[...] + mod_ref[0, 2:3, :] * acc


def _out_proj_call(x_all, mod, yf, yb, cv, g, hf, hb, lru, o_mla, p, n_tiles, n_lat_tiles):
    n = x_all.shape[0]
    row = lambda c: pl.BlockSpec((TILE, c), lambda j: (j, 0))
    consts = [p["ln_g"], p["ln_b"], p["ones_bd"], p["w_out"]]
    return pl.pallas_call(
        _out_proj_kernel,
        grid=(n_tiles,),
        in_specs=[row(D_MODEL), pl.BlockSpec((1, 6, D_MODEL), lambda j: (jnp.minimum(j // n_lat_tiles, 1), 0, 0))]
                 + [row(WIDTH)] * 6 + [pl.BlockSpec((TILE, WIDTH), lambda j: (j, 1)), row(MLA_WIDTH)]
                 + [_const_spec(a.shape) for a in consts],
        out_specs=row(D_MODEL),
        out_shape=jax.ShapeDtypeStruct((n, D_MODEL), F32),
        compiler_params=_params(1),
        name="out_proj",
    )(x_all, mod, yf, yb, cv, g, hf, hb, lru, o_mla, *consts)


def _ffn_kernel(tm, block0, seg_lo, seg_hi, mod_row, final, *refs):
    if final:
        (x_ref, prev_ref, next_ref, mod_ref, g_ref, wg_ref, wu_ref, wd_ref, cw_ref, cb_ref, fin_ref,
         o_ref, h_scr) = refs
    else:
        (x_ref, prev_ref, next_ref, mod_ref, g_ref, wg_ref, wu_ref, wd_ref, cw_ref, cb_ref,
         o_ref, h_scr) = refs
    c = pl.program_id(1)
    ext = tm + 2 * FFN_HALO

    @pl.when(c == 0)
    def _():
        sh = mod_ref[mod_row, 3:4, :]
        sc = 1.0 + mod_ref[mod_row, 4:5, :]
        g = g_ref[...]
        h_scr[0:FFN_HALO, :] = (_rms(prev_ref[...], g) * sc + sh).astype(BF16)
        h_scr[FFN_HALO:FFN_HALO + tm, :] = (_rms(x_ref[...], g) * sc + sh).astype(BF16)
        h_scr[FFN_HALO + tm:ext, :] = (_rms(next_ref[...], g) * sc + sh).astype(BF16)
        o_ref[...] = jnp.zeros_like(o_ref)

    ge = jnp.dot(h_scr[...], wg_ref[...], preferred_element_type=F32)
    grow = (block0 + pl.program_id(0)) * tm - FFN_HALO + _row_iota(ge.shape)
    ge = jnp.where(jnp.logical_and(grow >= seg_lo, grow < seg_hi), ge, 0.0)
    up_rows = pltpu.roll(ge, 1, 0)[FFN_HALO:FFN_HALO + tm, :]
    dn_rows = pltpu.roll(ge, ext - 1, 0)[FFN_HALO:FFN_HALO + tm, :]
    gate = cb_ref[...] + up_rows * cw_ref[0:1, :] + ge[FFN_HALO:FFN_HALO + tm, :] * cw_ref[1:2, :] + dn_rows * cw_ref[2:3, :]
    up = jnp.dot(h_scr[FFN_HALO:FFN_HALO + tm, :], wu_ref[...], preferred_element_type=F32)
    act = (_silu(gate) * up).astype(BF16)
    o_ref[...] += jnp.dot(act, wd_ref[...], preferred_element_type=F32)

    @pl.when(c == pl.num_programs(1) - 1)
    def _():
        out = x_ref[...] + mod_ref[mod_row, 5:6, :] * o_ref[...]
        if final:
            out = _rms(out, fin_ref[...])
        o_ref[...] = out


def _ffn_call(x_all, o_prev, mod, p, tm, block0, n_blocks, seg_lo, seg_hi, mod_row, final_g, out_rows):
    n = x_all.shape[0]
    hb = tm // FFN_HALO
    nb = n // FFN_HALO
    n_chunks = D_FF // FFN_CHUNK
    final = final_g is not None
    in_specs = [
        pl.BlockSpec((tm, D_MODEL), lambda j, c: (block0 + j, 0), pipeline_mode=pl.Buffered(1)),
        pl.BlockSpec((FFN_HALO, D_MODEL), lambda j, c: (jnp.maximum((block0 + j) * hb - 1, 0), 0)),
        pl.BlockSpec((FFN_HALO, D_MODEL), lambda j, c: (jnp.minimum((block0 + j + 1) * hb, nb - 1), 0)),
        pl.BlockSpec((2, 6, D_MODEL), lambda j, c: (0, 0, 0)),
        pl.BlockSpec((1, D_MODEL), lambda j, c: (0, 0)),
        pl.BlockSpec((D_MODEL, FFN_CHUNK), lambda j, c: (0, c)),
        pl.BlockSpec((D_MODEL, FFN_CHUNK), lambda j, c: (0, c)),
        pl.BlockSpec((FFN_CHUNK, D_MODEL), lambda j, c: (c, 0)),
        pl.BlockSpec((3, FFN_CHUNK), lambda j, c: (0, c)),
        pl.BlockSpec((1, FFN_CHUNK), lambda j, c: (0, c)),
    ]
    args = [x_all, x_all, x_all, mod, p["norm2"], p["w_gate"], p["w_up"], p["w_down"], p["ffn_conv"], p["ffn_conv_b"]]
    if final:
        in_specs.append(pl.BlockSpec((1, D_MODEL), lambda j, c: (0, 0)))
        args.append(final_g)
    aliases = {}
    if o_prev is not None:
        in_specs.append(pl.BlockSpec(memory_space=pl.ANY))
        args.append(o_prev)
        aliases = {len(args) - 1: 0}
    kern = functools.partial(_ffn_kernel, tm, block0, seg_lo, seg_hi, mod_row, final)
    if o_prev is not None:
        kern = _drop_ref(kern, len(args) - 1)
    return pl.pallas_call(
        kern,
        grid=(n_blocks, n_chunks),
        in_specs=in_specs,
        out_specs=pl.BlockSpec((tm, D_MODEL), lambda j, c: (block0 + j, 0)),
        out_shape=jax.ShapeDtypeStruct((out_rows, D_MODEL), F32),
        scratch_shapes=[pltpu.VMEM((tm + 2 * FFN_HALO, D_MODEL), BF16)],
        input_output_aliases=aliases,
        compiler_params=_params(2),
        name="conv_ffn",
    )(*args)


def _drop_ref(kern, idx):
    def wrapped(*refs):
        return kern(*(refs[:idx] + refs[idx + 1:]))
    return wrapped


_ROPE_PERM = np.concatenate([np.arange(16, 32), np.arange(0, 16), np.arange(48, 64), np.arange(32, 48)])


def _block_diag(blocks):
    h, n, m = blocks.shape
    eye = jnp.eye(h, dtype=blocks.dtype)
    return (eye[:, None, :, None] * blocks[:, :, None, :]).reshape(h * n, h * m)


def _scan_consts():
    idx = np.arange(CHUNK)
    lower = (idx[None, :] <= idx[:, None]).astype(np.float32)
    upper = (idx[None, :] >= idx[:, None]).astype(np.float32)
    eye_g = np.eye(GROUP, dtype=np.float32)
    bd = lambda m: np.kron(eye_g, m)
    return {
        "tri_fwd": jnp.asarray(lower, BF16), "tri_rev": jnp.asarray(upper, BF16),
        "incl_fwd": jnp.asarray(bd(lower)), "incl_rev": jnp.asarray(bd(upper)),
        "strict_fwd": jnp.asarray(bd(lower - np.eye(CHUNK, dtype=np.float32))),
        "strict_rev": jnp.asarray(bd(upper - np.eye(CHUNK, dtype=np.float32))),
        "bd_mask": jnp.asarray(bd(np.ones((CHUNK, HEAD_DIM), np.float32))),
        "eye": jnp.asarray(np.eye(GW, dtype=np.float32)),
    }


def _rope_tables(n_lat, n_ctx):
    n_freq = ROPE // 4
    rows = n_lat // GRID_W
    row = jnp.repeat(jnp.arange(rows, dtype=F32), GRID_W)
    col = jnp.tile(jnp.arange(GRID_W, dtype=F32), rows)
    inv_freq = ROPE_THETA ** (-jnp.arange(n_freq, dtype=F32) / n_freq)
    ar, ac = row[:, None] * inv_freq, col[:, None] * inv_freq
    cos = jnp.concatenate([jnp.cos(ar), jnp.cos(ar), jnp.cos(ac), jnp.cos(ac)], axis=-1)
    sin = jnp.concatenate([-jnp.sin(ar), jnp.sin(ar), -jnp.sin(ac), jnp.sin(ac)], axis=-1)
    cos = jnp.concatenate([cos, jnp.ones((n_ctx, ROPE), F32)], axis=0)
    sin = jnp.concatenate([sin, jnp.zeros((n_ctx, ROPE), F32)], axis=0)
    return {"cc": jnp.concatenate([cos, cos], axis=-1), "ss": jnp.concatenate([sin, sin], axis=-1),
            "cs": jnp.concatenate([cos, sin], axis=-1)}


def kernel(x, c, ctx, c_ctx, ada_w, ada_b, norm1, norm2, w_in, w_out, rw_conv, rw_conv_b, rw_w0, rw_w_up, rw_a0, rw_a_up, rw_g_up, rw_k_k, rw_k_a, rw_r_k, rw_ln_g, rw_ln_b, rw_v0, rw_v_down, rw_v_up, lru_conv, lru_conv_b, lru_wa, lru_ba, lru_wx, lru_bx, lru_lambda, mla_q_norm, mla_w_qb, mla_kv_norm, mla_w_kvb, ffn_w_gate, ffn_w_up, ffn_conv, ffn_conv_b, ffn_w_down, final_norm):
    assert x.shape[0] == 1 and ctx.shape[0] == 1
    depth = ada_w.shape[0]
    n_lat, n_ctx = x.shape[1], ctx.shape[1]
    assert n_lat % FFN_TILE == 0 and n_ctx % TILE == 0 and n_lat % GRID_W == 0
    n = n_lat + n_ctx
    n_lat_tiles = n_lat // TILE
    seg = ((0, n_lat), (n_lat, n))

    x_all = jnp.concatenate([x[0], ctx[0]], axis=0)
    mods = _mod_call(jnp.stack([c[0], c_ctx], axis=1), ada_w, ada_b).reshape(depth, 2, 6, D_MODEL)
    tabs = _rope_tables(n_lat, n_ctx)
    consts = _scan_consts()
    ones_bd = jnp.asarray(np.kron(np.eye(HEADS, dtype=np.float32), np.ones((HEAD_DIM, HEAD_DIM), np.float32)), BF16)
    kv_chunk = 1280 if n % 1280 == 0 else TILE
    row2 = lambda a: a.reshape(1, -1)

    v_first = None
    out = None
    for i in range(depth):
        last = i == depth - 1
        mla_off = 1760 + LRU_COLS
        rope_cols = mla_off + Q_RANK + KV_RANK + _ROPE_PERM
        vdown = rw_v_down[i - 1] if i > 0 else jnp.zeros((D_MODEL, LORA), F32)
        w_cat = jnp.concatenate([w_in[i][:, :1760], vdown, w_in[i][:, 1760:], w_in[i][:, rope_cols]], axis=1).astype(BF16)

        w_lora = jnp.zeros((RW_COLS - 3 * WIDTH, 6 * WIDTH), F32)
        b_lora = jnp.zeros((6 * WIDTH,), F32)
        for d in range(2):
            w_lora = w_lora.at[d * LORA:(d + 1) * LORA, d * WIDTH:(d + 1) * WIDTH].set(rw_w_up[i][d])
            w_lora = w_lora.at[(2 + d) * LORA:(3 + d) * LORA, (2 + d) * WIDTH:(3 + d) * WIDTH].set(rw_a_up[i][d])
            b_lora = b_lora.at[d * WIDTH:(d + 1) * WIDTH].set(rw_w0[i][d])
            b_lora = b_lora.at[(2 + d) * WIDTH:(3 + d) * WIDTH].set(rw_a0[i][d])
        w_lora = w_lora.at[4 * LORA:4 * LORA + GATE_LORA, 4 * WIDTH:5 * WIDTH].set(rw_g_up[i])
        if i > 0:
            w_lora = w_lora.at[4 * LORA + GATE_LORA:, 5 * WIDTH:].set(rw_v_up[i - 1])
            b_lora = b_lora.at[5 * WIDTH:].set(rw_v0[i - 1])
        ident = jnp.array([[0.0], [1.0], [0.0]], F32) * jnp.ones((1, LORA), F32)
        rw_p = {
            "rw_conv": jnp.concatenate([rw_conv[i], ident], axis=1),
            "rw_conv_b": row2(jnp.concatenate([rw_conv_b[i], jnp.zeros((LORA,), F32)])),
            "w_lora": w_lora.astype(BF16), "b_lora": row2(b_lora),
            "k_k": row2(rw_k_k[i]), "k_a": row2(rw_k_a[i]), "r_k": row2(rw_r_k[i]), "ones_bd": ones_bd,
        }
        lru_p = {
            "lru_conv": lru_conv[i], "lru_conv_b": row2(lru_conv_b[i]),
            "lru_wg": jnp.stack([jnp.concatenate([_block_diag(lru_wa[i][d]), _block_diag(lru_wx[i][d])], axis=1)
                                 for d in range(2)]).astype(BF16),
            "lru_bg": jnp.stack([jnp.concatenate([lru_ba[i][d], lru_bx[i][d]])[None] for d in range(2)]),
            "lru_lam": lru_lambda[i][:, None, :],
        }
        wq = mla_w_qb[i].reshape(Q_RANK, HEADS, NOPE + ROPE)
        mla_p = {
            "q_norm": row2(mla_q_norm[i]), "kv_norm": row2(mla_kv_norm[i]),
            "w_q": jnp.concatenate([wq[:, :, :NOPE].reshape(Q_RANK, -1), wq[:, :, NOPE:].reshape(Q_RANK, -1),
                                    wq[:, :, NOPE + _ROPE_PERM].reshape(Q_RANK, -1)], axis=1).astype(BF16),
            "w_kv": mla_w_kvb[i].astype(BF16),
        }
        out_p = {"ln_g": row2(rw_ln_g[i]), "ln_b": row2(rw_ln_b[i]), "ones_bd": ones_bd, "w_out": w_out[i].astype(BF16)}
        ffn_p = {"norm2": row2(norm2[i]), "w_gate": ffn_w_gate[i].astype(BF16), "w_up": ffn_w_up[i].astype(BF16),
                 "w_down": ffn_w_down[i].astype(BF16), "ffn_conv": ffn_conv[i], "ffn_conv_b": row2(ffn_conv_b[i])}

        rw, lru, mla = _in_proj_call(x_all, mods[i], row2(norm1[i]), w_cat, n_lat_tiles)

        r, v, kk, g, cv, lw, kd, bd = _rwkv_prep_call(rw, v_first, rw_p, seg)
        if i == 0:
            v_first = v
        y_f, y_b = _rwkv_scan_call(r, v, kk, lw, kd, bd, consts, n_lat_tiles)

        h_f = _lru_call(False, lru, lru_p, seg)
        h_b = _lru_call(True, lru, lru_p, seg)

        q_h, k_h, v_h = _mla_proj_call(mla, tabs, mla_p)
        o_mla = _attn_call(q_h, k_h, v_h, None, Q_TILE, 0, n_lat // Q_TILE, 0, kv_chunk, n // kv_chunk)
        if not last:
            o_mla = _attn_call(q_h, k_h, v_h, o_mla, TILE, n_lat_tiles, n_ctx // TILE, n_lat, TILE, n_ctx // TILE)

        n_tiles = n_lat_tiles if last else n // TILE
        x_mid = _out_proj_call(x_all, mods[i], y_f, y_b, cv, g, h_f, h_b, lru, o_mla, out_p, n_tiles, n_lat_tiles)

        if last:
            out = _ffn_call(x_mid, None, mods[i], ffn_p, FFN_TILE, 0, n_lat // FFN_TILE, 0, n_lat, 0,
                            row2(final_norm), n_lat)
        else:
            x_new = _ffn_call(x_mid, None, mods[i], ffn_p, FFN_TILE, 0, n_lat // FFN_TILE, 0, n_lat, 0, None, n)
            x_all = _ffn_call(x_mid, x_new, mods[i], ffn_p, TILE, n_lat_tiles, n_ctx // TILE, n_lat, n, 1, None, n)
    return out[None]
```

```python
import functools

import numpy as np
import jax
import jax.numpy as jnp
from jax import lax
from jax.experimental import pallas as pl
from jax.experimental.pallas import tpu as pltpu

F32 = jnp.float32
BF16 = jnp.bfloat16

D_MODEL = 2048
NORM_EPS = 1e-6
GN_EPS = 64e-5
DECAY_SCALE = 0.606531
LRU_C = 8.0
HEADS = 8
HEAD_DIM = 64
WIDTH = HEADS * HEAD_DIM
LORA = 32
GATE_LORA = 96
RW_COLS = 1792
LRU_COLS = 1024
MLA_COLS = 896
Q_RANK = 512
KV_RANK = 256
NOPE = 128
ROPE = 64
V_DIM = 128
V_SLOT = 256
QK_DIM = 256
MLA_WIDTH = HEADS * V_DIM
MLA_SCALE = (NOPE + ROPE) ** -0.5
Q_SCALE = MLA_SCALE * 1.4426950408889634
ROPE_THETA = 10000.0
GRID_W = 64
D_FF = 5632

TILE = 256
CHUNK = 64
GROUP = 4
GW = GROUP * HEAD_DIM
HALO = 8
FFN_TILE = 1024
FFN_HALO = 16
FFN_CHUNK = 512
Q_TILE = 1024
VMEM_LIMIT = 56 * 1024 * 1024


def _params(n_axes, vmem=VMEM_LIMIT):
    return pltpu.CompilerParams(dimension_semantics=("arbitrary",) * n_axes, vmem_limit_bytes=vmem)


def _const_spec(shape):
    nd = len(shape)
    return pl.BlockSpec(shape, lambda *_: (0,) * nd, pipeline_mode=pl.Buffered(1))


def _sigmoid(x):
    return 1.0 / (1.0 + jnp.exp(-x))


def _silu(x):
    return x * _sigmoid(x)


def _gelu_tanh(x):
    return 0.5 * x * (1.0 + jnp.tanh(0.7978845608028654 * (x + 0.044715 * (x * x * x))))


def _rms(x, g):
    return x * lax.rsqrt(jnp.mean(x * x, axis=-1, keepdims=True) + NORM_EPS) * g


def _dot_exact01(x, w01, left=False):
    out = None
    rem = x
    for _ in range(3):
        part = rem.astype(BF16)
        rem = rem - part.astype(F32)
        term = jnp.dot(w01, part, preferred_element_type=F32) if left else jnp.dot(part, w01, preferred_element_type=F32)
        out = term if out is None else out + term
    return out


def _row_iota(shape):
    return lax.broadcasted_iota(jnp.int32, shape, 0)


def _shift_down(cur, prev_rows, k):
    out = pltpu.roll(cur, k, 0)
    rows = _row_iota(cur.shape)
    for i in range(k):
        out = jnp.where(rows == i, prev_rows[i:i + 1, :], out)
    return out


def _shift_up(cur, next_row):
    n = cur.shape[0]
    out = pltpu.roll(cur, n - 1, 0)
    return jnp.where(_row_iota(cur.shape) == n - 1, next_row, out)


def _halo_valid(j, tile, seg_starts, seg_ends):
    first = j * tile
    last = first + tile
    lvalid = jnp.logical_and(first != seg_starts[0], first != seg_starts[1])
    rvalid = jnp.logical_and(last != seg_ends[0], last != seg_ends[1])
    return lvalid.astype(F32), rvalid.astype(F32)


def _mod_kernel(cc_ref, w_ref, b_ref, o_ref):
    s = _silu(cc_ref[...])
    w = w_ref[0]
    b = b_ref[0]
    o_ref[0, 0:1, :] = jnp.sum(s[:, 0:1] * w, axis=0, keepdims=True) + b
    o_ref[0, 1:2, :] = jnp.sum(s[:, 1:2] * w, axis=0, keepdims=True) + b


def _mod_call(cc, ada_w, ada_b):
    depth, d, n6 = ada_w.shape
    tn = 1024
    return pl.pallas_call(
        _mod_kernel,
        grid=(depth, n6 // tn),
        in_specs=[
            pl.BlockSpec((d, 2), lambda i, j: (0, 0)),
            pl.BlockSpec((1, d, tn), lambda i, j: (i, 0, j)),
            pl.BlockSpec((1, 1, tn), lambda i, j: (i, 0, j)),
        ],
        out_specs=pl.BlockSpec((1, 2, tn), lambda i, j: (i, 0, j)),
        out_shape=jax.ShapeDtypeStruct((depth, 2, n6), F32),
        compiler_params=_params(2),
        name="adaln_mod",
    )(cc, ada_w, ada_b.reshape(depth, 1, n6))


def _stream_specs(stream, n_lat_tiles):
    if not isinstance(stream, tuple):
        return [pl.BlockSpec((TILE, D_MODEL), lambda j: (j, 0))], [stream]
    return ([pl.BlockSpec((TILE, D_MODEL), lambda j: (jnp.minimum(j, n_lat_tiles - 1), 0)),
             pl.BlockSpec((TILE, D_MODEL), lambda j: (jnp.maximum(j - n_lat_tiles, 0), 0))], list(stream))


def _stream_tile(refs, n_lat_tiles):
    if len(refs) == 1:
        return refs[0][...]
    return jnp.where(pl.program_id(0) >= n_lat_tiles, refs[1][...], refs[0][...])


def _in_proj_kernel(n_src, n_lat_tiles, *refs):
    mod_ref, g_ref, w_ref, rw_ref, lru_ref, mla_ref = refs[n_src:]
    x = _stream_tile(refs[:n_src], n_lat_tiles)
    h = _rms(x, g_ref[...]) * (1.0 + mod_ref[0, 1:2, :]) + mod_ref[0, 0:1, :]
    hb = h.astype(BF16)
    rw_ref[...] = jnp.dot(hb, w_ref[:, 0:RW_COLS], preferred_element_type=F32)
    lru_ref[...] = jnp.dot(hb, w_ref[:, RW_COLS:RW_COLS + LRU_COLS], preferred_element_type=F32)
    mla_ref[...] = jnp.dot(hb, w_ref[:, RW_COLS + LRU_COLS:], preferred_element_type=F32)


def _in_proj_call(stream, n, mod, g, w_cat, n_lat_tiles):
    cols = w_cat.shape[1]
    row = lambda c: pl.BlockSpec((TILE, c), lambda j: (j, 0))
    x_specs, x_args = _stream_specs(stream, n_lat_tiles)
    return pl.pallas_call(
        functools.partial(_in_proj_kernel, len(x_args), n_lat_tiles),
        grid=(n // TILE,),
        in_specs=x_specs + [
            pl.BlockSpec((1, 6, D_MODEL), lambda j: (jnp.minimum(j // n_lat_tiles, 1), 0, 0)),
            _const_spec((1, D_MODEL)),
            _const_spec((D_MODEL, cols)),
        ],
        out_specs=[row(RW_COLS), row(LRU_COLS), row(MLA_COLS)],
        out_shape=[jax.ShapeDtypeStruct((n, c), F32) for c in (RW_COLS, LRU_COLS, MLA_COLS)],
        compiler_params=_params(1),
        name="in_proj",
    )(*x_args, mod, g, w_cat)


def _rwkv_prep_kernel(seg, has_vfirst, *refs):
    if has_vfirst:
        (cur_ref, prev_ref, next_ref, vf_ref, cw_ref, cb_ref, wl_ref, bl_ref, kk_ref, ka_ref, rk_ref, ones_ref,
         r_out, v_out, kk_out, g_out, cv_out, lw_out, kd_out, bd_out) = refs
    else:
        (cur_ref, prev_ref, next_ref, cw_ref, cb_ref, wl_ref, bl_ref, kk_ref, ka_ref, rk_ref, ones_ref,
         r_out, v_out, kk_out, g_out, cv_out, lw_out, kd_out, bd_out) = refs
    lvalid, rvalid = _halo_valid(pl.program_id(0), TILE, seg[0], seg[1])
    cur = cur_ref[...]
    up = _shift_down(cur, prev_ref[HALO - 1:HALO, :] * lvalid, 1)
    dn = _shift_up(cur, next_ref[0:1, :] * rvalid)
    u = cb_ref[...] + up * cw_ref[0:1, :] + cur * cw_ref[1:2, :] + dn * cw_ref[2:3, :]
    r = u[:, 0:WIDTH]
    k = u[:, WIDTH:2 * WIDTH]
    v = u[:, 2 * WIDTH:3 * WIDTH]
    blk = u[:, 3 * WIDTH:RW_COLS]
    lane = lax.broadcasted_iota(jnp.int32, blk.shape, 1)
    act = jnp.where(lane < 2 * LORA, jnp.tanh(blk),
                    jnp.where(jnp.logical_and(lane >= 4 * LORA, lane < 4 * LORA + GATE_LORA), _sigmoid(blk), blk))
    lo = jnp.dot(act.astype(BF16), wl_ref[...], preferred_element_type=F32) + bl_ref[...]
    g = lo[:, 4 * WIDTH:5 * WIDTH]
    if has_vfirst:
        mix = _sigmoid(lo[:, 5 * WIDTH:6 * WIDTH])
        v = v + (vf_ref[...] - v) * mix
    ones = ones_ref[...]
    kk = k * kk_ref[...]
    ss = _dot_exact01(kk * kk, ones)
    kk = kk * lax.rsqrt(jnp.maximum(ss, 1e-24))
    ksum = None
    for d in range(2):
        lw_out[d] = -DECAY_SCALE * _sigmoid(lo[:, d * WIDTH:(d + 1) * WIDTH])
        iclr = _sigmoid(lo[:, (2 + d) * WIDTH:(3 + d) * WIDTH])
        kd = k * (1.0 + (iclr - 1.0) * ka_ref[...])
        kd_out[d] = kd
        bd_out[d] = kk * iclr
        ksum = kd if ksum is None else ksum + kd
    coef = _dot_exact01(r * ksum * rk_ref[...], ones)
    r_out[...] = r
    v_out[...] = v
    kk_out[...] = kk
    g_out[...] = g
    cv_out[...] = coef * v


def _rwkv_prep_call(rw, v_first, p, seg):
    n = rw.shape[0]
    hb = TILE // HALO
    nb = n // HALO
    row = lambda c: pl.BlockSpec((TILE, c), lambda j: (j, 0))
    dir_row = pl.BlockSpec((2, TILE, WIDTH), lambda j: (0, j, 0))
    has_vf = v_first is not None
    in_specs = [
        row(RW_COLS),
        pl.BlockSpec((HALO, RW_COLS), lambda j: (jnp.maximum(j * hb - 1, 0), 0)),
        pl.BlockSpec((HALO, RW_COLS), lambda j: (jnp.minimum((j + 1) * hb, nb - 1), 0)),
    ]
    args = [rw, rw, rw]
    if has_vf:
        in_specs.append(row(WIDTH))
        args.append(v_first)
    consts = [p["rw_conv"], p["rw_conv_b"], p["w_lora"], p["b_lora"], p["k_k"], p["k_a"], p["r_k"], p["ones_bd"]]
    in_specs += [_const_spec(a.shape) for a in consts]
    args += consts
    return pl.pallas_call(
        functools.partial(_rwkv_prep_kernel, seg, has_vf),
        grid=(n // TILE,),
        in_specs=in_specs,
        out_specs=[row(WIDTH)] * 5 + [dir_row] * 3,
        out_shape=[jax.ShapeDtypeStruct((n, WIDTH), F32)] * 5 + [jax.ShapeDtypeStruct((2, n, WIDTH), F32)] * 3,
        compiler_params=_params(1),
        name="rwkv_prep",
    )(*args)


def _bd_stack(x, bd_mask):
    return jnp.concatenate([x] * GROUP, axis=0) * bd_mask


def _mm(a, b):
    return jnp.dot(a.astype(BF16), b.astype(BF16), preferred_element_type=F32)


def _mm_nt(a, b):
    return lax.dot_general(a.astype(BF16), b.astype(BF16), (((1,), (1,)), ((), ())), preferred_element_type=F32)


def _mm_tn(a, b):
    return lax.dot_general(a.astype(BF16), b.astype(BF16), (((0,), (0,)), ((), ())), preferred_element_type=F32)


def _scan_operands(reverse, rows, r_ref, v_ref, kk_ref, lw_ref, kd_ref, bd_ref, tri, strict, incl, bdm, h_ref):
    last = 0 if reverse else CHUNK - 1
    lw = lw_ref[0, rows, :]
    cl = _dot_exact01(lw, tri, left=True)
    tot = cl[last:last + 1, :]
    e_cl = jnp.exp(cl)
    e_cle = jnp.exp(cl - lw)
    e_ncl = jnp.exp(-cl)
    e_tc = jnp.exp(tot - cl)
    e_tot = jnp.exp(tot)
    a_t = -kk_ref[rows, :] * e_cle
    r_t = r_ref[rows, :] * e_cl
    kd = kd_ref[0, rows, :]
    bd = bd_ref[0, rows, :]
    b_t = bd * e_ncl
    k_t = kd * e_ncl
    b_h = bd * e_tc
    k_h = kd * e_tc
    v = v_ref[rows, :]
    probs = []
    for gi in range(HEADS // GROUP):
        ln = slice(gi * GW, (gi + 1) * GW)
        r_s = _bd_stack(r_t[:, ln], bdm)
        probs.append(dict(
            r_s=r_s, r_sb=r_s.astype(BF16),
            a_s=_bd_stack(a_t[:, ln], bdm).astype(BF16),
            b_s=_bd_stack(b_t[:, ln], bdm).astype(BF16),
            k_s=_bd_stack(k_t[:, ln], bdm).astype(BF16),
            bh_s=_bd_stack(b_h[:, ln], bdm).astype(BF16),
            kh_s=_bd_stack(k_h[:, ln], bdm).astype(BF16),
            v_s=_bd_stack(v[:, ln], bdm).astype(BF16),
            e_tot=e_tot[:, ln], strict=strict, incl=incl, h_ref=h_ref, gi=gi))
    return probs


def _scan_solve(probs, eye):
    for p in probs:
        p["a_ab"] = _mm_nt(p["a_s"], p["b_s"]) * p["strict"]
    for p in probs:
        p["a_ak"] = _mm_nt(p["a_s"], p["k_s"]) * p["strict"]
    for p in probs:
        p["a_rb"] = (_mm_nt(p["r_sb"], p["b_s"]) * p["incl"]).astype(BF16)
    for p in probs:
        p["a_rk"] = _mm_nt(p["r_sb"], p["k_s"]) * p["incl"]
    for p in probs:
        p["t"] = eye + p["a_ab"]
        p["pw"] = p["a_ab"]
        p["x"] = _mm(p["a_ak"], p["v_s"])
    for _ in range(5):
        for p in probs:
            p["pw"] = _mm(p["pw"], p["pw"])
        for p in probs:
            p["t"] = p["t"] + _mm(p["t"], p["pw"])
    for p in probs:
        p["t"] = p["t"].astype(BF16)
        p["abar"] = _mm(p["t"], p["a_s"]).astype(BF16)
    for p in probs:
        p["u0"] = _mm(p["t"], p["x"]).astype(BF16)
    for p in probs:
        p["m"] = eye * p["e_tot"] + _mm_tn(p["bh_s"], p["abar"])
    for p in probs:
        p["g"] = _mm_tn(p["bh_s"], p["u0"]) + _mm_tn(p["kh_s"], p["v_s"])
    for p in probs:
        p["rbar"] = p["r_s"] + _mm(p["a_rb"], p["abar"])
    for p in probs:
        p["y0"] = _mm(p["a_rb"], p["u0"]) + _mm(p["a_rk"], p["v_s"])
    ys = []
    for p in probs:
        h = p["h_ref"][p["gi"]].astype(BF16)
        y_bd = _mm(p["rbar"], h) + p["y0"]
        p["h_ref"][p["gi"]] = _mm(p["m"], h) + p["g"]
        y = y_bd[0:CHUNK, :]
        for hh in range(1, GROUP):
            y = y + y_bd[hh * CHUNK:(hh + 1) * CHUNK, :]
        ys.append(y)
    return ys


def _rwkv_scan_kernel(rf_ref, vf_ref, kkf_ref, lwf_ref, kdf_ref, bdf_ref, rb_ref, vb_ref, kkb_ref, lwb_ref, kdb_ref, bdb_ref,
                      trif_ref, trib_ref, strictf_ref, strictb_ref, inclf_ref, inclb_ref, bdm_ref, eye_ref,
                      yf_ref, yb_ref, hf_ref, hb_ref):
    @pl.when(pl.program_id(0) == 0)
    def _():
        hf_ref[...] = jnp.zeros_like(hf_ref)
        hb_ref[...] = jnp.zeros_like(hb_ref)

    n_chunks = TILE // CHUNK
    bdm = bdm_ref[...]
    eye = eye_ref[...]

    def chunk_body(ci, carry):
        rows_f = pl.ds(pl.multiple_of(ci * CHUNK, CHUNK), CHUNK)
        rows_b = pl.ds(pl.multiple_of((n_chunks - 1 - ci) * CHUNK, CHUNK), CHUNK)
        probs = _scan_operands(False, rows_f, rf_ref, vf_ref, kkf_ref, lwf_ref, kdf_ref, bdf_ref,
                               trif_ref[...], strictf_ref[...], inclf_ref[...], bdm, hf_ref)
        probs += _scan_operands(True, rows_b, rb_ref, vb_ref, kkb_ref, lwb_ref, kdb_ref, bdb_ref,
                                trib_ref[...], strictb_ref[...], inclb_ref[...], bdm, hb_ref)
        ys = _scan_solve(probs, eye)
        n_groups = HEADS // GROUP
        yf_ref[rows_f, :] = jnp.concatenate(ys[:n_groups], axis=1)
        yb_ref[rows_b, :] = jnp.concatenate(ys[n_groups:], axis=1)
        return carry

    lax.fori_loop(0, n_chunks, chunk_body, 0)


def _scan_tile(reverse, n_lat_tiles):
    if reverse:
        return lambda j: jnp.where(j == 0, n_lat_tiles, n_lat_tiles - j)
    return lambda j: jnp.where(j == 0, n_lat_tiles, j - 1)


def _rwkv_scan_call(r, v, kk, lw, kd, bd, consts, n_lat_tiles):
    n = r.shape[0]
    specs = []
    for d, reverse in enumerate((False, True)):
        tile = _scan_tile(reverse, n_lat_tiles)
        row = pl.BlockSpec((TILE, WIDTH), lambda j, tile=tile: (tile(j), 0))
        dir_row = pl.BlockSpec((1, TILE, WIDTH), lambda j, tile=tile, d=d: (d, tile(j), 0))
        specs.append((row, dir_row))
    (row_f, dir_f), (row_b, dir_b) = specs
    cs = [consts["tri_fwd"], consts["tri_rev"], consts["strict_fwd"], consts["strict_rev"],
          consts["incl_fwd"], consts["incl_rev"], consts["bd_mask"], consts["eye"]]
    state = pltpu.VMEM((HEADS // GROUP, GW, GW), F32)
    return pl.pallas_call(
        _rwkv_scan_kernel,
        grid=(n // TILE,),
        in_specs=[row_f] * 3 + [dir_f] * 3 + [row_b] * 3 + [dir_b] * 3 + [_const_spec(a.shape) for a in cs],
        out_specs=[row_f, row_b],
        out_shape=[jax.ShapeDtypeStruct((n, WIDTH), F32)] * 2,
        scratch_shapes=[state, state],
        compiler_params=_params(1),
        name="rwkv_scan",
    )(r, v, kk, lw, kd, bd, r, v, kk, lw, kd, bd, *cs)


def _lru_kernel(reverse, seg, cur_ref, prev_ref, next_ref, cw_ref, cb_ref, wg_ref, bg_ref, lam_ref,
                hs_ref, a_scr, b_scr, h_scr):
    j = pl.program_id(0)

    @pl.when(j == 0)
    def _():
        h_scr[...] = jnp.zeros_like(h_scr)

    tile_idx = cur_tile_index(reverse, seg, j)
    lvalid, rvalid = _halo_valid(tile_idx, TILE, seg[0], seg[1])
    cur = cur_ref[...]
    prev = prev_ref[...] * lvalid
    x2 = _shift_down(cur, prev[HALO - 2:HALO, :], 2)
    x1 = _shift_down(cur, prev[HALO - 1:HALO, :], 1)
    xn = _shift_up(cur, next_ref[0:1, :] * rvalid)
    xb = cb_ref[...] + x2 * cw_ref[0:1, :] + x1 * cw_ref[1:2, :] + cur * cw_ref[2:3, :] + xn * cw_ref[3:4, :]
    gates = jnp.dot(xb.astype(BF16), wg_ref[0], preferred_element_type=F32) + bg_ref[0]
    gate_r = _sigmoid(gates[:, 0:WIDTH])
    gate_i = _sigmoid(gates[:, WIDTH:2 * WIDTH])
    lam = lam_ref[0]
    softplus = jnp.maximum(-lam, 0.0) + jnp.log(1.0 + jnp.exp(-jnp.abs(lam)))
    log_a = -LRU_C * gate_r * softplus
    a = jnp.exp(log_a)
    b = jnp.sqrt(1.0 - a * a) * gate_i * xb

    in_group = _row_iota((TILE, WIDTH)) % 8
    for s in (1, 2, 4):
        if reverse:
            a_sh = pltpu.roll(a, TILE - s, 0)
            b_sh = pltpu.roll(b, TILE - s, 0)
            ok = in_group < 8 - s
        else:
            a_sh = pltpu.roll(a, s, 0)
            b_sh = pltpu.roll(b, s, 0)
            ok = in_group >= s
        b = jnp.where(ok, a * b_sh + b, b)
        a = jnp.where(ok, a * a_sh, a)
    a_scr[...] = a
    b_scr[...] = b

    n_groups = TILE // 8

    def group_body(gi, h):
        g = (n_groups - 1 - gi) if reverse else gi
        rows = pl.ds(pl.multiple_of(g * 8, 8), 8)
        hs = a_scr[rows, :] * h + b_scr[rows, :]
        hs_ref[rows, :] = hs
        return hs[0:1, :] if reverse else hs[7:8, :]

    h_scr[...] = lax.fori_loop(0, n_groups, group_body, h_scr[...], unroll=4)


def cur_tile_index(reverse, seg, j):
    n_lat_tiles = seg[1][0] // TILE
    return _scan_tile(reverse, n_lat_tiles)(j)


def _lru_call(reverse, lru, p, seg):
    n = lru.shape[0]
    d = 1 if reverse else 0
    hb = TILE // HALO
    nb = n // HALO
    n_lat_tiles = seg[1][0] // TILE
    tile = _scan_tile(reverse, n_lat_tiles)
    dsel = lambda shape: pl.BlockSpec((1,) + shape, lambda j: (d, 0, 0), pipeline_mode=pl.Buffered(1))
    return pl.pallas_call(
        functools.partial(_lru_kernel, reverse, seg),
        grid=(n // TILE,),
        in_specs=[
            pl.BlockSpec((TILE, WIDTH), lambda j: (tile(j), 0)),
            pl.BlockSpec((HALO, WIDTH), lambda j: (jnp.maximum(tile(j) * hb - 1, 0), 0)),
            pl.BlockSpec((HALO, WIDTH), lambda j: (jnp.minimum((tile(j) + 1) * hb, nb - 1), 0)),
            _const_spec(p["lru_conv"].shape),
            _const_spec(p["lru_conv_b"].shape),
            dsel((WIDTH, 2 * WIDTH)),
            dsel((1, 2 * WIDTH)),
            dsel((1, WIDTH)),
        ],
        out_specs=pl.BlockSpec((TILE, WIDTH), lambda j: (tile(j), 0)),
        out_shape=jax.ShapeDtypeStruct((n, WIDTH), F32),
        scratch_shapes=[pltpu.VMEM((TILE, WIDTH), F32), pltpu.VMEM((TILE, WIDTH), F32), pltpu.VMEM((1, WIDTH), F32)],
        compiler_params=_params(1),
        name="lru_rev" if reverse else "lru_fwd",
    )(lru, lru, lru, p["lru_conv"], p["lru_conv_b"], p["lru_wg"], p["lru_bg"], p["lru_lam"])


def _mla_proj_kernel(cols_ref, cc_ref, ss_ref, cs_ref, gq_ref, gkv_ref, wq_ref, wkv_ref, q_ref, k_ref, v_ref):
    cols = cols_ref[...]
    qn = _rms(cols[:, 0:Q_RANK], gq_ref[...]).astype(BF16)
    q = jnp.dot(qn, wq_ref[...], preferred_element_type=F32)
    kvn = _rms(cols[:, Q_RANK:Q_RANK + KV_RANK], gkv_ref[...]).astype(BF16)
    kv = jnp.dot(kvn, wkv_ref[...], preferred_element_type=F32)
    kr = cols[:, Q_RANK + KV_RANK:MLA_COLS] * cs_ref[...]
    kr = (kr + pltpu.roll(kr, ROPE, 1)).astype(BF16)
    lane = lax.broadcasted_iota(jnp.int32, (TILE, 2 * ROPE), 1)
    ones_col = jnp.where(lax.broadcasted_iota(jnp.int32, (TILE, V_SLOT - V_DIM), 1) == 0, 1.0, 0.0).astype(BF16)
    cc = cc_ref[...]
    ss = ss_ref[...]
    for hp in range(HEADS // 2):
        sl = slice(HEADS * NOPE + hp * 2 * ROPE, HEADS * NOPE + (hp + 1) * 2 * ROPE)
        sw = slice(HEADS * NOPE + HEADS * ROPE + hp * 2 * ROPE, HEADS * NOPE + HEADS * ROPE + (hp + 1) * 2 * ROPE)
        roped = (q[:, sl] * cc + q[:, sw] * ss) * Q_SCALE
        for e in range(2):
            h = 2 * hp + e
            q_ref[h, :, 0:NOPE] = (q[:, h * NOPE:(h + 1) * NOPE] * Q_SCALE).astype(BF16)
            keep = (lane < ROPE) if e == 0 else (lane >= ROPE)
            q_ref[h, :, NOPE:QK_DIM] = jnp.where(keep, roped, 0.0).astype(BF16)
            k_ref[h, :, 0:NOPE] = kv[:, h * 2 * NOPE:h * 2 * NOPE + NOPE].astype(BF16)
            k_ref[h, :, NOPE:QK_DIM] = kr
            v_ref[h, :, 0:V_DIM] = kv[:, h * 2 * NOPE + NOPE:(h + 1) * 2 * NOPE].astype(BF16)
            v_ref[h, :, V_DIM:V_SLOT] = ones_col


def _mla_proj_call(mla, tabs, p):
    n = mla.shape[0]
    row = lambda c: pl.BlockSpec((TILE, c), lambda j: (j, 0))
    head = lambda c: pl.BlockSpec((HEADS, TILE, c), lambda j: (0, j, 0))
    consts = [p["q_norm"], p["kv_norm"], p["w_q"], p["w_kv"]]
    return pl.pallas_call(
        _mla_proj_kernel,
        grid=(n // TILE,),
        in_specs=[row(MLA_COLS), row(2 * ROPE), row(2 * ROPE), row(2 * ROPE)] + [_const_spec(a.shape) for a in consts],
        out_specs=[head(QK_DIM), head(QK_DIM), head(V_SLOT)],
        out_shape=[jax.ShapeDtypeStruct((HEADS, n, QK_DIM), BF16), jax.ShapeDtypeStruct((HEADS, n, QK_DIM), BF16),
                   jax.ShapeDtypeStruct((HEADS, n, V_SLOT), BF16)],
        compiler_params=_params(1),
        name="mla_proj",
    )(mla, tabs["cc"], tabs["ss"], tabs["cs"], *consts)


def _attn_kernel(kv_start, kv_chunk, n_kv, *refs):
    q_ref, k_ref, v_ref = refs[0:3]
    (o_ref, sa0, sa1, sb0, sb1, pb_scr, alpha_b, m_a, acc_a, m_b, acc_b) = refs[-11:]
    tq = acc_a.shape[0]
    for m_scr, acc_scr in ((m_a, acc_a), (m_b, acc_b)):
        m_scr[...] = jnp.full(m_scr.shape, -jnp.inf, F32)
        acc_scr[...] = jnp.zeros_like(acc_scr)
    q_a = (0, tq)
    q_b = (tq, 2 * tq)
    s_a = (sa0, sa1)
    s_b = (sb0, sb1)

    def chunk_rows(ci):
        return pl.ds(pl.multiple_of(kv_start + ci * kv_chunk, kv_chunk), kv_chunk)

    def scores(q_rows, ci, s_ref):
        q = q_ref[0, q_rows[0]:q_rows[1], :]
        s_ref[...] = lax.dot_general(q, k_ref[0, chunk_rows(ci), :], (((1,), (1,)), ((), ())),
                                     preferred_element_type=F32)

    def softmax(s_ref, m_scr):
        s = s_ref[...]
        m_old = m_scr[...]
        m_new = jnp.maximum(m_old, jnp.max(s, axis=-1, keepdims=True))
        alpha = jnp.exp2(m_old - m_new)
        pr = jnp.exp2((s - m_new).astype(BF16))
        m_scr[...] = m_new
        return pr, alpha

    def values(acc_scr, alpha, pr, ci):
        acc_scr[...] = alpha * acc_scr[...] + jnp.dot(pr, v_ref[0, chunk_rows(ci), :], preferred_element_type=F32)

    def step(ci, cur, first, last):
        if not first:
            values(acc_b, alpha_b[...], pb_scr[...], ci - 1)
        if not last:
            scores(q_a, ci + 1, s_a[1 - cur])
        pr, alpha = softmax(s_a[cur], m_a)
        values(acc_a, alpha, pr, ci)
        if not last:
            scores(q_b, ci + 1, s_b[1 - cur])
        pr, alpha = softmax(s_b[cur], m_b)
        pb_scr[...] = pr
        alpha_b[...] = alpha

    scores(q_a, 0, sa0)
    scores(q_b, 0, sb0)
    step(0, 0, True, n_kv == 1)
    if n_kv > 1:
        n_mid = n_kv - 2
        n_pairs = n_mid // 2

        def pair(t, carry):
            step(1 + 2 * t, 1, False, False)
            step(2 + 2 * t, 0, False, False)
            return carry

        lax.fori_loop(0, n_pairs, pair, 0)
        if n_mid % 2 == 1:
            step(n_kv - 2, (n_kv - 2) % 2, False, False)
        step(n_kv - 1, (n_kv - 1) % 2, False, True)
    values(acc_b, alpha_b[...], pb_scr[...], n_kv - 1)
    o_ref[0:tq, :] = (acc_a[:, 0:V_DIM] / acc_a[:, V_DIM:V_DIM + 1]).astype(o_ref.dtype)
    o_ref[tq:2 * tq, :] = (acc_b[:, 0:V_DIM] / acc_b[:, V_DIM:V_DIM + 1]).astype(o_ref.dtype)


def _attn_call(q, k, v, o_prev, q_tile, q_block0, n_q, kv_start, kv_chunk, n_kv):
    n = q.shape[1]
    tq = q_tile // 2
    in_specs = [
        pl.BlockSpec((1, q_tile, QK_DIM), lambda h, i: (h, q_block0 + i, 0)),
        pl.BlockSpec((1, n, QK_DIM), lambda h, i: (h, 0, 0)),
        pl.BlockSpec((1, n, V_SLOT), lambda h, i: (h, 0, 0)),
    ]
    args = [q, k, v]
    aliases = {}
    if o_prev is not None:
        in_specs.append(pl.BlockSpec(memory_space=pl.ANY))
        args.append(o_prev)
        aliases = {3: 0}
    return pl.pallas_call(
        functools.partial(_attn_kernel, kv_start, kv_chunk, n_kv),
        grid=(HEADS, n_q),
        in_specs=in_specs,
        out_specs=pl.BlockSpec((q_tile, V_DIM), lambda h, i: (q_block0 + i, h)),
        out_shape=jax.ShapeDtypeStruct((n, MLA_WIDTH), BF16),
        scratch_shapes=[pltpu.VMEM((tq, kv_chunk), F32)] * 4
                       + [pltpu.VMEM((tq, kv_chunk), BF16), pltpu.VMEM((tq, 1), F32)]
                       + [pltpu.VMEM((tq, 1), F32), pltpu.VMEM((tq, V_SLOT), F32)] * 2,
        input_output_aliases=aliases,
        compiler_params=_params(2),
        name="mla_attn",
    )(*args)


def _out_proj_kernel(n_src, n_lat_tiles, *refs):
    (mod_ref, yf_ref, yb_ref, cv_ref, g_ref, hf_ref, hb_ref, gate_ref, om_ref,
     lng_ref, lnb_ref, ones_ref, w_ref, o_ref) = refs[n_src:]
    ones = ones_ref[...]
    y = yf_ref[...] + yb_ref[...]
    mu = _dot_exact01(y, ones) * (1.0 / HEAD_DIM)
    dlt = y - mu
    var = _dot_exact01(dlt * dlt, ones) * (1.0 / HEAD_DIM)
    yn = dlt * lax.rsqrt(var + GN_EPS) * lng_ref[...] + lnb_ref[...]
    o_rw = ((yn + cv_ref[...]) * g_ref[...]).astype(BF16)
    o_lru = ((hf_ref[...] + hb_ref[...]) * _gelu_tanh(gate_ref[...])).astype(BF16)
    acc = jnp.dot(o_rw, w_ref[0:WIDTH, :], preferred_element_type=F32)
    acc += jnp.dot(o_lru, w_ref[WIDTH:2 * WIDTH, :], preferred_element_type=F32)
    acc += jnp.dot(om_ref[...], w_ref[2 * WIDTH:, :], preferred_element_type=F32)
    o_ref[...] = _stream_tile(refs[:n_src], n_lat_tiles) + mod_ref[0, 2:3, :] * acc


def _out_proj_call(stream, n, mod, yf, yb, cv, g, hf, hb, lru, o_mla, p, n_tiles, n_lat_tiles):
    row = lambda c: pl.BlockSpec((TILE, c), lambda j: (j, 0))
    consts = [p["ln_g"], p["ln_b"], p["ones_bd"], p["w_out"]]
    x_specs, x_args = _stream_specs(stream, n_lat_tiles)
    return pl.pallas_call(
        functools.partial(_out_proj_kernel, len(x_args), n_lat_tiles),
        grid=(n_tiles,),
        in_specs=x_specs + [pl.BlockSpec((1, 6, D_MODEL), lambda j: (jnp.minimum(j // n_lat_tiles, 1), 0, 0))]
                 + [row(WIDTH)] * 6 + [pl.BlockSpec((TILE, WIDTH), lambda j: (j, 1)), row(MLA_WIDTH)]
                 + [_const_spec(a.shape) for a in consts],
        out_specs=row(D_MODEL),
        out_shape=jax.ShapeDtypeStruct((n, D_MODEL), F32),
        compiler_params=_params(1),
        name="out_proj",
    )(*x_args, mod, yf, yb, cv, g, hf, hb, lru, o_mla, *consts)


def _ffn_kernel(tm, block0, seg_lo, seg_hi, mod_row, final, *refs):
    if final:
        (x_ref, prev_ref, next_ref, mod_ref, g_ref, wg_ref, wu_ref, wd_ref, cw_ref, cb_ref, fin_ref,
         o_ref, h_scr) = refs
    else:
        (x_ref, prev_ref, next_ref, mod_ref, g_ref, wg_ref, wu_ref, wd_ref, cw_ref, cb_ref,
         o_ref, h_scr) = refs
    c = pl.program_id(1)
    ext = tm + 2 * FFN_HALO

    @pl.when(c == 0)
    def _():
        sh = mod_ref[mod_row, 3:4, :]
        sc = 1.0 + mod_ref[mod_row, 4:5, :]
        g = g_ref[...]
        h_scr[0:FFN_HALO, :] = (_rms(prev_ref[...], g) * sc + sh).astype(BF16)
        h_scr[FFN_HALO:FFN_HALO + tm, :] = (_rms(x_ref[...], g) * sc + sh).astype(BF16)
        h_scr[FFN_HALO + tm:ext, :] = (_rms(next_ref[...], g) * sc + sh).astype(BF16)
        o_ref[...] = jnp.zeros_like(o_ref)

    ge = jnp.dot(h_scr[...], wg_ref[...], preferred_element_type=F32)
    grow = (block0 + pl.program_id(0)) * tm - FFN_HALO + _row_iota(ge.shape)
    ge = jnp.where(jnp.logical_and(grow >= seg_lo, grow < seg_hi), ge, 0.0)
    up_rows = pltpu.roll(ge, 1, 0)[FFN_HALO:FFN_HALO + tm, :]
    dn_rows = pltpu.roll(ge, ext - 1, 0)[FFN_HALO:FFN_HALO + tm, :]
    gate = cb_ref[...] + up_rows * cw_ref[0:1, :] + ge[FFN_HALO:FFN_HALO + tm, :] * cw_ref[1:2, :] + dn_rows * cw_ref[2:3, :]
    up = jnp.dot(h_scr[FFN_HALO:FFN_HALO + tm, :], wu_ref[...], preferred_element_type=F32)
    act = (_silu(gate) * up).astype(BF16)
    o_ref[...] += jnp.dot(act, wd_ref[...], preferred_element_type=F32)

    @pl.when(c == pl.num_programs(1) - 1)
    def _():
        out = x_ref[...] + mod_ref[mod_row, 5:6, :] * o_ref[...]
        if final:
            out = _rms(out, fin_ref[...])
        o_ref[...] = out


def _ffn_call(x_all, o_prev, mod, p, layer, tm, block0, n_blocks, seg_lo, seg_hi, mod_row, final_g, out_rows):
    n = x_all.shape[0]
    hb = tm // FFN_HALO
    nb = n // FFN_HALO
    n_chunks = D_FF // FFN_CHUNK
    final = final_g is not None
    in_specs = [
        pl.BlockSpec((tm, D_MODEL), lambda j, c: (block0 + j, 0), pipeline_mode=pl.Buffered(1)),
        pl.BlockSpec((FFN_HALO, D_MODEL), lambda j, c: (jnp.maximum((block0 + j) * hb - 1, 0), 0)),
        pl.BlockSpec((FFN_HALO, D_MODEL), lambda j, c: (jnp.minimum((block0 + j + 1) * hb, nb - 1), 0)),
        pl.BlockSpec((2, 6, D_MODEL), lambda j, c: (0, 0, 0)),
        pl.BlockSpec((1, D_MODEL), lambda j, c: (0, 0)),
        pl.BlockSpec((None, D_MODEL, FFN_CHUNK), lambda j, c: (layer, 0, c)),
        pl.BlockSpec((None, D_MODEL, FFN_CHUNK), lambda j, c: (layer, 0, c)),
        pl.BlockSpec((None, FFN_CHUNK, D_MODEL), lambda j, c: (layer, c, 0)),
        pl.BlockSpec((3, FFN_CHUNK), lambda j, c: (0, c)),
        pl.BlockSpec((1, FFN_CHUNK), lambda j, c: (0, c)),
    ]
    args = [x_all, x_all, x_all, mod, p["norm2"], p["w_gate"], p["w_up"], p["w_down"], p["ffn_conv"], p["ffn_conv_b"]]
    if final:
        in_specs.append(pl.BlockSpec((1, D_MODEL), lambda j, c: (0, 0)))
        args.append(final_g)
    aliases = {}
    if o_prev is not None:
        in_specs.append(pl.BlockSpec(memory_space=pl.ANY))
        args.append(o_prev)
        aliases = {len(args) - 1: 0}
    kern = functools.partial(_ffn_kernel, tm, block0, seg_lo, seg_hi, mod_row, final)
    if o_prev is not None:
        kern = _drop_ref(kern, len(args) - 1)
    return pl.pallas_call(
        kern,
        grid=(n_blocks, n_chunks),
        in_specs=in_specs,
        out_specs=pl.BlockSpec((tm, D_MODEL), lambda j, c: (block0 + j, 0)),
        out_shape=jax.ShapeDtypeStruct((out_rows, D_MODEL), F32),
        scratch_shapes=[pltpu.VMEM((tm + 2 * FFN_HALO, D_MODEL), BF16)],
        input_output_aliases=aliases,
        compiler_params=_params(2),
        name="conv_ffn",
    )(*args)


def _cast_kernel(x_ref, o_ref):
    o_ref[...] = x_ref[...].astype(o_ref.dtype)


def _cast_bf16(w):
    depth, r, c = w.shape
    rb = 1 << (((2 * 1024 * 1024) // c).bit_length() - 1)
    assert (depth * r) % rb == 0 and rb % 16 == 0
    out = pl.pallas_call(
        _cast_kernel,
        grid=(depth * r // rb,),
        in_specs=[pl.BlockSpec((rb, c), lambda j: (j, 0))],
        out_specs=pl.BlockSpec((rb, c), lambda j: (j, 0)),
        out_shape=jax.ShapeDtypeStruct((depth * r, c), BF16),
        compiler_params=_params(1),
        name="cast_bf16",
    )(w.reshape(depth * r, c))
    return out.reshape(depth, r, c)


def _drop_ref(kern, idx):
    def wrapped(*refs):
        return kern(*(refs[:idx] + refs[idx + 1:]))
    return wrapped


_ROPE_PERM = np.concatenate([np.arange(16, 32), np.arange(0, 16), np.arange(48, 64), np.arange(32, 48)])


def _block_diag(blocks):
    h, n, m = blocks.shape
    eye = jnp.eye(h, dtype=blocks.dtype)
    return (eye[:, None, :, None] * blocks[:, :, None, :]).reshape(h * n, h * m)


def _scan_consts():
    idx = np.arange(CHUNK)
    lower = (idx[None, :] <= idx[:, None]).astype(np.float32)
    upper = (idx[None, :] >= idx[:, None]).astype(np.float32)
    eye_g = np.eye(GROUP, dtype=np.float32)
    bd = lambda m: np.kron(eye_g, m)
    return {
        "tri_fwd": jnp.asarray(lower, BF16), "tri_rev": jnp.asarray(upper, BF16),
        "incl_fwd": jnp.asarray(bd(lower)), "incl_rev": jnp.asarray(bd(upper)),
        "strict_fwd": jnp.asarray(bd(lower - np.eye(CHUNK, dtype=np.float32))),
        "strict_rev": jnp.asarray(bd(upper - np.eye(CHUNK, dtype=np.float32))),
        "bd_mask": jnp.asarray(bd(np.ones((CHUNK, HEAD_DIM), np.float32))),
        "eye": jnp.asarray(np.eye(GW, dtype=np.float32)),
    }


def _rope_tables(n_lat, n_ctx):
    n_freq = ROPE // 4
    rows = n_lat // GRID_W
    row = jnp.repeat(jnp.arange(rows, dtype=F32), GRID_W)
    col = jnp.tile(jnp.arange(GRID_W, dtype=F32), rows)
    inv_freq = ROPE_THETA ** (-jnp.arange(n_freq, dtype=F32) / n_freq)
    ar, ac = row[:, None] * inv_freq, col[:, None] * inv_freq
    cos = jnp.concatenate([jnp.cos(ar), jnp.cos(ar), jnp.cos(ac), jnp.cos(ac)], axis=-1)
    sin = jnp.concatenate([-jnp.sin(ar), jnp.sin(ar), -jnp.sin(ac), jnp.sin(ac)], axis=-1)
    cos = jnp.concatenate([cos, jnp.ones((n_ctx, ROPE), F32)], axis=0)
    sin = jnp.concatenate([sin, jnp.zeros((n_ctx, ROPE), F32)], axis=0)
    return {"cc": jnp.concatenate([cos, cos], axis=-1), "ss": jnp.concatenate([sin, sin], axis=-1),
            "cs": jnp.concatenate([cos, sin], axis=-1)}


def kernel(x, c, ctx, c_ctx, ada_w, ada_b, norm1, norm2, w_in, w_out, rw_conv, rw_conv_b, rw_w0, rw_w_up, rw_a0, rw_a_up, rw_g_up, rw_k_k, rw_k_a, rw_r_k, rw_ln_g, rw_ln_b, rw_v0, rw_v_down, rw_v_up, lru_conv, lru_conv_b, lru_wa, lru_ba, lru_wx, lru_bx, lru_lambda, mla_q_norm, mla_w_qb, mla_kv_norm, mla_w_kvb, ffn_w_gate, ffn_w_up, ffn_conv, ffn_conv_b, ffn_w_down, final_norm):
    assert x.shape[0] == 1 and ctx.shape[0] == 1
    depth = ada_w.shape[0]
    n_lat, n_ctx = x.shape[1], ctx.shape[1]
    assert n_lat % FFN_TILE == 0 and n_ctx % TILE == 0 and n_lat % GRID_W == 0
    n = n_lat + n_ctx
    n_lat_tiles = n_lat // TILE
    seg = ((0, n_lat), (n_lat, n))

    x_all = (x[0], ctx[0])
    mods =_mod_call(jnp.stack([c[0], c_ctx], axis=1), ada_w, ada_b).reshape(depth, 2, 6, D_MODEL)
    tabs = _rope_tables(n_lat, n_ctx)
    consts = _scan_consts()
    ones_bd = jnp.asarray(np.kron(np.eye(HEADS, dtype=np.float32), np.ones((HEAD_DIM, HEAD_DIM), np.float32)), BF16)
    kv_chunk = 1280 if n % 1280 == 0 else TILE
    row2 = lambda a: a.reshape(1, -1)
    w_gate_bf, w_up_bf, w_down_bf = _cast_bf16(ffn_w_gate), _cast_bf16(ffn_w_up), _cast_bf16(ffn_w_down)

    v_first = None
    out = None
    for i in range(depth):
        last = i == depth - 1
        mla_off = 1760 + LRU_COLS
        rope_cols = mla_off + Q_RANK + KV_RANK + _ROPE_PERM
        vdown = rw_v_down[i - 1] if i > 0 else jnp.zeros((D_MODEL, LORA), F32)
        w_cat = jnp.concatenate([w_in[i][:, :1760], vdown, w_in[i][:, 1760:], w_in[i][:, rope_cols]], axis=1).astype(BF16)

        w_lora = jnp.zeros((RW_COLS - 3 * WIDTH, 6 * WIDTH), F32)
        b_lora = jnp.zeros((6 * WIDTH,), F32)
        for d in range(2):
            w_lora = w_lora.at[d * LORA:(d + 1) * LORA, d * WIDTH:(d + 1) * WIDTH].set(rw_w_up[i][d])
            w_lora = w_lora.at[(2 + d) * LORA:(3 + d) * LORA, (2 + d) * WIDTH:(3 + d) * WIDTH].set(rw_a_up[i][d])
            b_lora = b_lora.at[d * WIDTH:(d + 1) * WIDTH].set(rw_w0[i][d])
            b_lora = b_lora.at[(2 + d) * WIDTH:(3 + d) * WIDTH].set(rw_a0[i][d])
        w_lora = w_lora.at[4 * LORA:4 * LORA + GATE_LORA, 4 * WIDTH:5 * WIDTH].set(rw_g_up[i])
        if i > 0:
            w_lora = w_lora.at[4 * LORA + GATE_LORA:, 5 * WIDTH:].set(rw_v_up[i - 1])
            b_lora = b_lora.at[5 * WIDTH:].set(rw_v0[i - 1])
        ident = jnp.array([[0.0], [1.0], [0.0]], F32) * jnp.ones((1, LORA), F32)
        rw_p = {
            "rw_conv": jnp.concatenate([rw_conv[i], ident], axis=1),
            "rw_conv_b": row2(jnp.concatenate([rw_conv_b[i], jnp.zeros((LORA,), F32)])),
            "w_lora": w_lora.astype(BF16), "b_lora": row2(b_lora),
            "k_k": row2(rw_k_k[i]), "k_a": row2(rw_k_a[i]), "r_k": row2(rw_r_k[i]), "ones_bd": ones_bd,
        }
        lru_p = {
            "lru_conv": lru_conv[i], "lru_conv_b": row2(lru_conv_b[i]),
            "lru_wg": jnp.stack([jnp.concatenate([_block_diag(lru_wa[i][d]), _block_diag(lru_wx[i][d])], axis=1)
                                 for d in range(2)]).astype(BF16),
            "lru_bg": jnp.stack([jnp.concatenate([lru_ba[i][d], lru_bx[i][d]])[None] for d in range(2)]),
            "lru_lam": lru_lambda[i][:, None, :],
        }
        wq = mla_w_qb[i].reshape(Q_RANK, HEADS, NOPE + ROPE)
        mla_p = {
            "q_norm": row2(mla_q_norm[i]), "kv_norm": row2(mla_kv_norm[i]),
            "w_q": jnp.concatenate([wq[:, :, :NOPE].reshape(Q_RANK, -1), wq[:, :, NOPE:].reshape(Q_RANK, -1),
                                    wq[:, :, NOPE + _ROPE_PERM].reshape(Q_RANK, -1)], axis=1).astype(BF16),
            "w_kv": mla_w_kvb[i].astype(BF16),
        }
        out_p = {"ln_g": row2(rw_ln_g[i]), "ln_b": row2(rw_ln_b[i]), "ones_bd": ones_bd, "w_out": w_out[i].astype(BF16)}
        ffn_p = {"norm2": row2(norm2[i]), "w_gate": w_gate_bf, "w_up": w_up_bf, "w_down": w_down_bf,
                 "ffn_conv": ffn_conv[i], "ffn_conv_b": row2(ffn_conv_b[i])}

        rw, lru, mla = _in_proj_call(x_all, n, mods[i], row2(norm1[i]), w_cat, n_lat_tiles)

        r, v, kk, g, cv, lw, kd, bd = _rwkv_prep_call(rw, v_first, rw_p, seg)
        if i == 0:
            v_first = v
        y_f, y_b = _rwkv_scan_call(r, v, kk, lw, kd, bd, consts, n_lat_tiles)

        h_f = _lru_call(False, lru, lru_p, seg)
        h_b = _lru_call(True, lru, lru_p, seg)

        q_h, k_h, v_h = _mla_proj_call(mla, tabs, mla_p)
        o_mla = _attn_call(q_h, k_h, v_h, None, Q_TILE, 0, n_lat // Q_TILE, 0, kv_chunk, n // kv_chunk)
        if not last:
            o_mla = _attn_call(q_h, k_h, v_h, o_mla, TILE, n_lat_tiles, n_ctx // TILE, n_lat, TILE, n_ctx // TILE)

        n_tiles = n_lat_tiles if last else n // TILE
        x_mid = _out_proj_call(x_all, n, mods[i], y_f, y_b, cv, g, h_f, h_b, lru, o_mla, out_p, n_tiles, n_lat_tiles)

        if last:
            out = _ffn_call(x_mid, None, mods[i], ffn_p, i, FFN_TILE, 0, n_lat // FFN_TILE, 0, n_lat, 0,
                            row2(final_norm), n_lat)
        else:
            x_new = _ffn_call(x_mid, None, mods[i], ffn_p, i, FFN_TILE, 0, n_lat // FFN_TILE, 0, n_lat, 0, None, n)
            x_all = _ffn_call(x_mid, x_new, mods[i], ffn_p, i, TILE, n_lat_tiles, n_ctx // TILE, n_lat, n, 1, None, n)
    return out[None]
```

```python
import functools

import numpy as np
import jax
import jax.numpy as jnp
from jax import lax
from jax.experimental import pallas as pl
from jax.experimental.pallas import tpu as pltpu

F32 = jnp.float32
BF16 = jnp.bfloat16

D_MODEL = 2048
NORM_EPS = 1e-6
GN_EPS = 64e-5
DECAY_SCALE = 0.606531
LRU_C = 8.0
HEADS = 8
HEAD_DIM = 64
WIDTH = HEADS * HEAD_DIM
LORA = 32
GATE_LORA = 96
RW_COLS = 1792
LRU_COLS = 1024
MLA_COLS = 896
Q_RANK = 512
KV_RANK = 256
NOPE = 128
ROPE = 64
V_DIM = 128
V_SLOT = 256
QK_DIM = 256
MLA_WIDTH = HEADS * V_DIM
MLA_SCALE = (NOPE + ROPE) ** -0.5
Q_SCALE = MLA_SCALE * 1.4426950408889634
ROPE_THETA = 10000.0
GRID_W = 64
D_FF = 5632

TILE = 256
CHUNK = 64
SCAN_UNROLL = 2
GROUP = 4
GW = GROUP * HEAD_DIM
HALO = 8
FFN_TILE = 1024
FFN_HALO = 16
FFN_CHUNK = 512
Q_TILE = 1024
VMEM_LIMIT = 56 * 1024 * 1024


def _params(n_axes, vmem=VMEM_LIMIT):
    return pltpu.CompilerParams(dimension_semantics=("arbitrary",) * n_axes, vmem_limit_bytes=vmem)


def _const_spec(shape):
    nd = len(shape)
    return pl.BlockSpec(shape, lambda *_: (0,) * nd, pipeline_mode=pl.Buffered(1))


def _sigmoid(x):
    return 1.0 / (1.0 + jnp.exp(-x))


def _silu(x):
    return x * _sigmoid(x)


def _gelu_tanh(x):
    return 0.5 * x * (1.0 + jnp.tanh(0.7978845608028654 * (x + 0.044715 * (x * x * x))))


def _rms(x, g):
    return x * lax.rsqrt(jnp.mean(x * x, axis=-1, keepdims=True) + NORM_EPS) * g


def _dot_exact01(x, w01, left=False):
    out = None
    rem = x
    for _ in range(3):
        part = rem.astype(BF16)
        rem = rem - part.astype(F32)
        term = jnp.dot(w01, part, preferred_element_type=F32) if left else jnp.dot(part, w01, preferred_element_type=F32)
        out = term if out is None else out + term
    return out


def _row_iota(shape):
    return lax.broadcasted_iota(jnp.int32, shape, 0)


def _shift_down(cur, prev_rows, k):
    out = pltpu.roll(cur, k, 0)
    rows = _row_iota(cur.shape)
    for i in range(k):
        out = jnp.where(rows == i, prev_rows[i:i + 1, :], out)
    return out


def _shift_up(cur, next_row):
    n = cur.shape[0]
    out = pltpu.roll(cur, n - 1, 0)
    return jnp.where(_row_iota(cur.shape) == n - 1, next_row, out)


def _halo_valid(j, tile, seg_starts, seg_ends):
    first = j * tile
    last = first + tile
    lvalid = jnp.logical_and(first != seg_starts[0], first != seg_starts[1])
    rvalid = jnp.logical_and(last != seg_ends[0], last != seg_ends[1])
    return lvalid.astype(F32), rvalid.astype(F32)


def _mod_kernel(cc_ref, w_ref, b_ref, o_ref):
    s = _silu(cc_ref[...])
    w = w_ref[0]
    b = b_ref[0]
    o_ref[0, 0:1, :] = jnp.sum(s[:, 0:1] * w, axis=0, keepdims=True) + b
    o_ref[0, 1:2, :] = jnp.sum(s[:, 1:2] * w, axis=0, keepdims=True) + b


def _mod_call(cc, ada_w, ada_b):
    depth, d, n6 = ada_w.shape
    tn = 1024
    return pl.pallas_call(
        _mod_kernel,
        grid=(depth, n6 // tn),
        in_specs=[
            pl.BlockSpec((d, 2), lambda i, j: (0, 0)),
            pl.BlockSpec((1, d, tn), lambda i, j: (i, 0, j)),
            pl.BlockSpec((1, 1, tn), lambda i, j: (i, 0, j)),
        ],
        out_specs=pl.BlockSpec((1, 2, tn), lambda i, j: (i, 0, j)),
        out_shape=jax.ShapeDtypeStruct((depth, 2, n6), F32),
        compiler_params=_params(2),
        name="adaln_mod",
    )(cc, ada_w, ada_b.reshape(depth, 1, n6))


def _stream_specs(stream, n_lat_tiles):
    if not isinstance(stream, tuple):
        return [pl.BlockSpec((TILE, D_MODEL), lambda j: (j, 0))], [stream]
    return ([pl.BlockSpec((TILE, D_MODEL), lambda j: (jnp.minimum(j, n_lat_tiles - 1), 0)),
             pl.BlockSpec((TILE, D_MODEL), lambda j: (jnp.maximum(j - n_lat_tiles, 0), 0))], list(stream))


def _stream_tile(refs, n_lat_tiles):
    if len(refs) == 1:
        return refs[0][...]
    return jnp.where(pl.program_id(0) >= n_lat_tiles, refs[1][...], refs[0][...])


def _in_proj_kernel(n_src, n_lat_tiles, *refs):
    (mod_ref, g_ref, w_ref, cc_ref, ss_ref, cs_ref, gq_ref, gkv_ref, wq_ref, wkv_ref,
     rw_ref, lru_ref, q_ref, k_ref, v_ref) = refs[n_src:]
    x = _stream_tile(refs[:n_src], n_lat_tiles)
    h = _rms(x, g_ref[...]) * (1.0 + mod_ref[0, 1:2, :]) + mod_ref[0, 0:1, :]
    hb = h.astype(BF16)
    rw_ref[...] = jnp.dot(hb, w_ref[:, 0:RW_COLS], preferred_element_type=F32)
    lru_ref[...] = jnp.dot(hb, w_ref[:, RW_COLS:RW_COLS + LRU_COLS], preferred_element_type=F32)
    mla = jnp.dot(hb, w_ref[:, RW_COLS + LRU_COLS:], preferred_element_type=F32)
    _mla_project(mla, cc_ref, ss_ref, cs_ref, gq_ref, gkv_ref, wq_ref, wkv_ref, q_ref, k_ref, v_ref)


def _in_proj_call(stream, n, mod, g, w_cat, tabs, mla_p, n_lat_tiles):
    cols = w_cat.shape[1]
    row = lambda c: pl.BlockSpec((TILE, c), lambda j: (j, 0))
    head = lambda c: pl.BlockSpec((HEADS, TILE, c), lambda j: (0, j, 0))
    x_specs, x_args = _stream_specs(stream, n_lat_tiles)
    consts = [mla_p["q_norm"], mla_p["kv_norm"], mla_p["w_q"], mla_p["w_kv"]]
    return pl.pallas_call(
        functools.partial(_in_proj_kernel, len(x_args), n_lat_tiles),
        grid=(n // TILE,),
        in_specs=x_specs + [
            pl.BlockSpec((1, 6, D_MODEL), lambda j: (jnp.minimum(j // n_lat_tiles, 1), 0, 0)),
            _const_spec((1, D_MODEL)),
            _const_spec((D_MODEL, cols)),
            row(2 * ROPE), row(2 * ROPE), row(2 * ROPE),
        ] + [_const_spec(a.shape) for a in consts],
        out_specs=[row(RW_COLS), row(LRU_COLS), head(QK_DIM), head(QK_DIM), head(V_SLOT)],
        out_shape=[jax.ShapeDtypeStruct((n, RW_COLS), F32), jax.ShapeDtypeStruct((n, LRU_COLS), F32),
                   jax.ShapeDtypeStruct((HEADS, n, QK_DIM), BF16), jax.ShapeDtypeStruct((HEADS, n, QK_DIM), BF16),
                   jax.ShapeDtypeStruct((HEADS, n, V_SLOT), BF16)],
        compiler_params=_params(1),
        name="in_proj",
    )(*x_args, mod, g, w_cat, tabs["cc"], tabs["ss"], tabs["cs"], *consts)


def _rwkv_prep_kernel(seg, has_vfirst, *refs):
    if has_vfirst:
        (cur_ref, prev_ref, next_ref, vf_ref, cw_ref, cb_ref, wl_ref, bl_ref, kk_ref, ka_ref, rk_ref, ones_ref,
         r_out, v_out, kk_out, g_out, cv_out, lw_out, kd_out, bd_out) = refs
    else:
        (cur_ref, prev_ref, next_ref, cw_ref, cb_ref, wl_ref, bl_ref, kk_ref, ka_ref, rk_ref, ones_ref,
         r_out, v_out, kk_out, g_out, cv_out, lw_out, kd_out, bd_out) = refs
    lvalid, rvalid = _halo_valid(pl.program_id(0), TILE, seg[0], seg[1])
    cur = cur_ref[...]
    up = _shift_down(cur, prev_ref[HALO - 1:HALO, :] * lvalid, 1)
    dn = _shift_up(cur, next_ref[0:1, :] * rvalid)
    u = cb_ref[...] + up * cw_ref[0:1, :] + cur * cw_ref[1:2, :] + dn * cw_ref[2:3, :]
    r = u[:, 0:WIDTH]
    k = u[:, WIDTH:2 * WIDTH]
    v = u[:, 2 * WIDTH:3 * WIDTH]
    blk = u[:, 3 * WIDTH:RW_COLS]
    lane = lax.broadcasted_iota(jnp.int32, blk.shape, 1)
    act = jnp.where(lane < 2 * LORA, jnp.tanh(blk),
                    jnp.where(jnp.logical_and(lane >= 4 * LORA, lane < 4 * LORA + GATE_LORA), _sigmoid(blk), blk))
    lo = jnp.dot(act.astype(BF16), wl_ref[...], preferred_element_type=F32) + bl_ref[...]
    g = lo[:, 4 * WIDTH:5 * WIDTH]
    if has_vfirst:
        mix = _sigmoid(lo[:, 5 * WIDTH:6 * WIDTH])
        v = v + (vf_ref[...] - v) * mix
    ones = ones_ref[...]
    kk = k * kk_ref[...]
    ss = _dot_exact01(kk * kk, ones)
    kk = kk * lax.rsqrt(jnp.maximum(ss, 1e-24))
    ksum = None
    for d in range(2):
        lw_out[d] = -DECAY_SCALE * _sigmoid(lo[:, d * WIDTH:(d + 1) * WIDTH])
        iclr = _sigmoid(lo[:, (2 + d) * WIDTH:(3 + d) * WIDTH])
        kd = k * (1.0 + (iclr - 1.0) * ka_ref[...])
        kd_out[d] = kd
        bd_out[d] = kk * iclr
        ksum = kd if ksum is None else ksum + kd
    coef = _dot_exact01(r * ksum * rk_ref[...], ones)
    r_out[...] = r
    v_out[...] = v
    kk_out[...] = kk
    g_out[...] = g
    cv_out[...] = coef * v


def _rwkv_prep_call(rw, v_first, p, seg):
    n = rw.shape[0]
    hb = TILE // HALO
    nb = n // HALO
    row = lambda c: pl.BlockSpec((TILE, c), lambda j: (j, 0))
    dir_row = pl.BlockSpec((2, TILE, WIDTH), lambda j: (0, j, 0))
    has_vf = v_first is not None
    in_specs = [
        row(RW_COLS),
        pl.BlockSpec((HALO, RW_COLS), lambda j: (jnp.maximum(j * hb - 1, 0), 0)),
        pl.BlockSpec((HALO, RW_COLS), lambda j: (jnp.minimum((j + 1) * hb, nb - 1), 0)),
    ]
    args = [rw, rw, rw]
    if has_vf:
        in_specs.append(row(WIDTH))
        args.append(v_first)
    consts = [p["rw_conv"], p["rw_conv_b"], p["w_lora"], p["b_lora"], p["k_k"], p["k_a"], p["r_k"], p["ones_bd"]]
    in_specs += [_const_spec(a.shape) for a in consts]
    args += consts
    return pl.pallas_call(
        functools.partial(_rwkv_prep_kernel, seg, has_vf),
        grid=(n // TILE,),
        in_specs=in_specs,
        out_specs=[row(WIDTH)] * 5 + [dir_row] * 3,
        out_shape=[jax.ShapeDtypeStruct((n, WIDTH), F32)] * 5 + [jax.ShapeDtypeStruct((2, n, WIDTH), F32)] * 3,
        compiler_params=_params(1),
        name="rwkv_prep",
    )(*args)


def _bd_stack(x, bd_mask):
    return jnp.concatenate([x] * GROUP, axis=0) * bd_mask


def _mm(a, b):
    return jnp.dot(a.astype(BF16), b.astype(BF16), preferred_element_type=F32)


def _mm_nt(a, b):
    return lax.dot_general(a.astype(BF16), b.astype(BF16), (((1,), (1,)), ((), ())), preferred_element_type=F32)


def _mm_tn(a, b):
    return lax.dot_general(a.astype(BF16), b.astype(BF16), (((0,), (0,)), ((), ())), preferred_element_type=F32)


def _scan_operands(reverse, rows, r_ref, v_ref, kk_ref, lw_ref, kd_ref, bd_ref, tri, strict, incl, bdm, h_ref):
    last = 0 if reverse else CHUNK - 1
    lw = lw_ref[0, rows, :]
    cl = _dot_exact01(lw, tri, left=True)
    tot = cl[last:last + 1, :]
    e_cl = jnp.exp(cl)
    e_cle = jnp.exp(cl - lw)
    e_ncl = jnp.exp(-cl)
    e_tc = jnp.exp(tot - cl)
    e_tot = jnp.exp(tot)
    a_t = -kk_ref[rows, :] * e_cle
    r_t = r_ref[rows, :] * e_cl
    kd = kd_ref[0, rows, :]
    bd = bd_ref[0, rows, :]
    b_t = bd * e_ncl
    k_t = kd * e_ncl
    b_h = bd * e_tc
    k_h = kd * e_tc
    v = v_ref[rows, :]
    probs = []
    for gi in range(HEADS // GROUP):
        ln = slice(gi * GW, (gi + 1) * GW)
        r_s = _bd_stack(r_t[:, ln], bdm)
        probs.append(dict(
            r_s=r_s, r_sb=r_s.astype(BF16),
            a_s=_bd_stack(a_t[:, ln], bdm).astype(BF16),
            b_s=_bd_stack(b_t[:, ln], bdm).astype(BF16),
            k_s=_bd_stack(k_t[:, ln], bdm).astype(BF16),
            bh_s=_bd_stack(b_h[:, ln], bdm).astype(BF16),
            kh_s=_bd_stack(k_h[:, ln], bdm).astype(BF16),
            v_s=_bd_stack(v[:, ln], bdm).astype(BF16),
            e_tot=e_tot[:, ln], strict=strict, incl=incl, h_ref=h_ref, gi=gi))
    return probs


def _scan_solve(probs, eye):
    for p in probs:
        p["a_ab"] = _mm_nt(p["a_s"], p["b_s"]) * p["strict"]
    for p in probs:
        p["a_ak"] = _mm_nt(p["a_s"], p["k_s"]) * p["strict"]
    for p in probs:
        p["a_rb"] = (_mm_nt(p["r_sb"], p["b_s"]) * p["incl"]).astype(BF16)
    for p in probs:
        p["a_rk"] = _mm_nt(p["r_sb"], p["k_s"]) * p["incl"]
    for p in probs:
        p["t"] = eye + p["a_ab"]
        p["pw"] = p["a_ab"]
        p["x"] = _mm(p["a_ak"], p["v_s"])
    for _ in range(5):
        for p in probs:
            p["pw"] = _mm(p["pw"], p["pw"])
        for p in probs:
            p["t"] = p["t"] + _mm(p["t"], p["pw"])
    for p in probs:
        p["t"] = p["t"].astype(BF16)
        p["abar"] = _mm(p["t"], p["a_s"]).astype(BF16)
    for p in probs:
        p["u0"] = _mm(p["t"], p["x"]).astype(BF16)
    for p in probs:
        p["m"] = eye * p["e_tot"] + _mm_tn(p["bh_s"], p["abar"])
    for p in probs:
        p["g"] = _mm_tn(p["bh_s"], p["u0"]) + _mm_tn(p["kh_s"], p["v_s"])
    for p in probs:
        p["rbar"] = p["r_s"] + _mm(p["a_rb"], p["abar"])
    for p in probs:
        p["y0"] = _mm(p["a_rb"], p["u0"]) + _mm(p["a_rk"], p["v_s"])
    ys = []
    for p in probs:
        h = p["h_ref"][p["gi"]].astype(BF16)
        y_bd = _mm(p["rbar"], h) + p["y0"]
        p["h_ref"][p["gi"]] = _mm(p["m"], h) + p["g"]
        y = y_bd[0:CHUNK, :]
        for hh in range(1, GROUP):
            y = y + y_bd[hh * CHUNK:(hh + 1) * CHUNK, :]
        ys.append(y)
    return ys


def _rwkv_scan_kernel(rf_ref, vf_ref, kkf_ref, lwf_ref, kdf_ref, bdf_ref, rb_ref, vb_ref, kkb_ref, lwb_ref, kdb_ref, bdb_ref,
                      trif_ref, trib_ref, strictf_ref, strictb_ref, inclf_ref, inclb_ref, bdm_ref, eye_ref,
                      yf_ref, yb_ref, hf_ref, hb_ref):
    @pl.when(pl.program_id(0) == 0)
    def _():
        hf_ref[...] = jnp.zeros_like(hf_ref)
        hb_ref[...] = jnp.zeros_like(hb_ref)

    n_chunks = TILE // CHUNK
    bdm = bdm_ref[...]
    eye = eye_ref[...]

    n_groups = HEADS // GROUP

    def chunk_body(pi, carry):
        probs, rows = [], []
        for ci in [SCAN_UNROLL * pi + u for u in range(SCAN_UNROLL)]:
            rows_f = pl.ds(pl.multiple_of(ci * CHUNK, CHUNK), CHUNK)
            rows_b = pl.ds(pl.multiple_of((n_chunks - 1 - ci) * CHUNK, CHUNK), CHUNK)
            probs += _scan_operands(False, rows_f, rf_ref, vf_ref, kkf_ref, lwf_ref, kdf_ref, bdf_ref,
                                    trif_ref[...], strictf_ref[...], inclf_ref[...], bdm, hf_ref)
            probs += _scan_operands(True, rows_b, rb_ref, vb_ref, kkb_ref, lwb_ref, kdb_ref, bdb_ref,
                                    trib_ref[...], strictb_ref[...], inclb_ref[...], bdm, hb_ref)
            rows.append((rows_f, rows_b))
        ys = _scan_solve(probs, eye)
        for k, (rows_f, rows_b) in enumerate(rows):
            base = 2 * n_groups * k
            yf_ref[rows_f, :] = jnp.concatenate(ys[base:base + n_groups], axis=1)
            yb_ref[rows_b, :] = jnp.concatenate(ys[base + n_groups:base + 2 * n_groups], axis=1)
        return carry

    lax.fori_loop(0, n_chunks // SCAN_UNROLL, chunk_body, 0)


def _scan_tile(reverse, n_lat_tiles):
    if reverse:
        return lambda j: jnp.where(j == 0, n_lat_tiles, n_lat_tiles - j)
    return lambda j: jnp.where(j == 0, n_lat_tiles, j - 1)


def _rwkv_scan_call(r, v, kk, lw, kd, bd, consts, n_lat_tiles):
    n = r.shape[0]
    specs = []
    for d, reverse in enumerate((False, True)):
        tile = _scan_tile(reverse, n_lat_tiles)
        row = pl.BlockSpec((TILE, WIDTH), lambda j, tile=tile: (tile(j), 0))
        dir_row = pl.BlockSpec((1, TILE, WIDTH), lambda j, tile=tile, d=d: (d, tile(j), 0))
        specs.append((row, dir_row))
    (row_f, dir_f), (row_b, dir_b) = specs
    cs = [consts["tri_fwd"], consts["tri_rev"], consts["strict_fwd"], consts["strict_rev"],
          consts["incl_fwd"], consts["incl_rev"], consts["bd_mask"], consts["eye"]]
    state = pltpu.VMEM((HEADS // GROUP, GW, GW), F32)
    return pl.pallas_call(
        _rwkv_scan_kernel,
        grid=(n // TILE,),
        in_specs=[row_f] * 3 + [dir_f] * 3 + [row_b] * 3 + [dir_b] * 3 + [_const_spec(a.shape) for a in cs],
        out_specs=[row_f, row_b],
        out_shape=[jax.ShapeDtypeStruct((n, WIDTH), F32)] * 2,
        scratch_shapes=[state, state],
        compiler_params=_params(1),
        name="rwkv_scan",
    )(r, v, kk, lw, kd, bd, r, v, kk, lw, kd, bd, *cs)


def _lru_kernel(reverse, seg, cur_ref, prev_ref, next_ref, cw_ref, cb_ref, wg_ref, bg_ref, lam_ref,
                hs_ref, a_scr, b_scr, h_scr):
    j = pl.program_id(0)

    @pl.when(j == 0)
    def _():
        h_scr[...] = jnp.zeros_like(h_scr)

    tile_idx = cur_tile_index(reverse, seg, j)
    lvalid, rvalid = _halo_valid(tile_idx, TILE, seg[0], seg[1])
    cur = cur_ref[...]
    prev = prev_ref[...] * lvalid
    x2 = _shift_down(cur, prev[HALO - 2:HALO, :], 2)
    x1 = _shift_down(cur, prev[HALO - 1:HALO, :], 1)
    xn = _shift_up(cur, next_ref[0:1, :] * rvalid)
    xb = cb_ref[...] + x2 * cw_ref[0:1, :] + x1 * cw_ref[1:2, :] + cur * cw_ref[2:3, :] + xn * cw_ref[3:4, :]
    gates = jnp.dot(xb.astype(BF16), wg_ref[0], preferred_element_type=F32) + bg_ref[0]
    gate_r = _sigmoid(gates[:, 0:WIDTH])
    gate_i = _sigmoid(gates[:, WIDTH:2 * WIDTH])
    lam = lam_ref[0]
    softplus = jnp.maximum(-lam, 0.0) + jnp.log(1.0 + jnp.exp(-jnp.abs(lam)))
    log_a = -LRU_C * gate_r * softplus
    a = jnp.exp(log_a)
    b = jnp.sqrt(1.0 - a * a) * gate_i * xb

    in_group = _row_iota((TILE, WIDTH)) % 8
    for s in (1, 2, 4):
        if reverse:
            a_sh = pltpu.roll(a, TILE - s, 0)
            b_sh = pltpu.roll(b, TILE - s, 0)
            ok = in_group < 8 - s
        else:
            a_sh = pltpu.roll(a, s, 0)
            b_sh = pltpu.roll(b, s, 0)
            ok = in_group >= s
        b = jnp.where(ok, a * b_sh + b, b)
        a = jnp.where(ok, a * a_sh, a)
    a_scr[...] = a
    b_scr[...] = b

    n_groups = TILE // 8

    def group_body(gi, h):
        g = (n_groups - 1 - gi) if reverse else gi
        rows = pl.ds(pl.multiple_of(g * 8, 8), 8)
        hs = a_scr[rows, :] * h + b_scr[rows, :]
        hs_ref[rows, :] = hs
        return hs[0:1, :] if reverse else hs[7:8, :]

    h_scr[...] = lax.fori_loop(0, n_groups, group_body, h_scr[...], unroll=4)


def cur_tile_index(reverse, seg, j):
    n_lat_tiles = seg[1][0] // TILE
    return _scan_tile(reverse, n_lat_tiles)(j)


def _lru_call(reverse, lru, p, seg):
    n = lru.shape[0]
    d = 1 if reverse else 0
    hb = TILE // HALO
    nb = n // HALO
    n_lat_tiles = seg[1][0] // TILE
    tile = _scan_tile(reverse, n_lat_tiles)
    dsel = lambda shape: pl.BlockSpec((1,) + shape, lambda j: (d, 0, 0), pipeline_mode=pl.Buffered(1))
    return pl.pallas_call(
        functools.partial(_lru_kernel, reverse, seg),
        grid=(n // TILE,),
        in_specs=[
            pl.BlockSpec((TILE, WIDTH), lambda j: (tile(j), 0)),
            pl.BlockSpec((HALO, WIDTH), lambda j: (jnp.maximum(tile(j) * hb - 1, 0), 0)),
            pl.BlockSpec((HALO, WIDTH), lambda j: (jnp.minimum((tile(j) + 1) * hb, nb - 1), 0)),
            _const_spec(p["lru_conv"].shape),
            _const_spec(p["lru_conv_b"].shape),
            dsel((WIDTH, 2 * WIDTH)),
            dsel((1, 2 * WIDTH)),
            dsel((1, WIDTH)),
        ],
        out_specs=pl.BlockSpec((TILE, WIDTH), lambda j: (tile(j), 0)),
        out_shape=jax.ShapeDtypeStruct((n, WIDTH), F32),
        scratch_shapes=[pltpu.VMEM((TILE, WIDTH), F32), pltpu.VMEM((TILE, WIDTH), F32), pltpu.VMEM((1, WIDTH), F32)],
        compiler_params=_params(1),
        name="lru_rev" if reverse else "lru_fwd",
    )(lru, lru, lru, p["lru_conv"], p["lru_conv_b"], p["lru_wg"], p["lru_bg"], p["lru_lam"])


def _mla_project(cols, cc_ref, ss_ref, cs_ref, gq_ref, gkv_ref, wq_ref, wkv_ref, q_ref, k_ref, v_ref):
    qn =_rms(cols[:, 0:Q_RANK], gq_ref[...]).astype(BF16)
    q = jnp.dot(qn, wq_ref[...], preferred_element_type=F32)
    kvn = _rms(cols[:, Q_RANK:Q_RANK + KV_RANK], gkv_ref[...]).astype(BF16)
    kv = jnp.dot(kvn, wkv_ref[...], preferred_element_type=F32)
    kr = cols[:, Q_RANK + KV_RANK:MLA_COLS] * cs_ref[...]
    kr = (kr + pltpu.roll(kr, ROPE, 1)).astype(BF16)
    lane = lax.broadcasted_iota(jnp.int32, (TILE, 2 * ROPE), 1)
    ones_col = jnp.where(lax.broadcasted_iota(jnp.int32, (TILE, V_SLOT - V_DIM), 1) == 0, 1.0, 0.0).astype(BF16)
    cc = cc_ref[...]
    ss = ss_ref[...]
    for hp in range(HEADS // 2):
        sl = slice(HEADS * NOPE + hp * 2 * ROPE, HEADS * NOPE + (hp + 1) * 2 * ROPE)
        sw = slice(HEADS * NOPE + HEADS * ROPE + hp * 2 * ROPE, HEADS * NOPE + HEADS * ROPE + (hp + 1) * 2 * ROPE)
        roped = (q[:, sl] * cc + q[:, sw] * ss) * Q_SCALE
        for e in range(2):
            h = 2 * hp + e
            q_ref[h, :, 0:NOPE] = (q[:, h * NOPE:(h + 1) * NOPE] * Q_SCALE).astype(BF16)
            keep = (lane < ROPE) if e == 0 else (lane >= ROPE)
            q_ref[h, :, NOPE:QK_DIM] = jnp.where(keep, roped, 0.0).astype(BF16)
            k_ref[h, :, 0:NOPE] = kv[:, h * 2 * NOPE:h * 2 * NOPE + NOPE].astype(BF16)
            k_ref[h, :, NOPE:QK_DIM] = kr
            v_ref[h, :, 0:V_DIM] = kv[:, h * 2 * NOPE + NOPE:(h + 1) * 2 * NOPE].astype(BF16)
            v_ref[h, :, V_DIM:V_SLOT] = ones_col


def _attn_kernel(kv_start, kv_chunk, n_kv, *refs):
    q_ref, k_ref, v_ref = refs[0:3]
    (o_ref, sa0, sa1, sb0, sb1, pb_scr, alpha_b, m_a, acc_a, m_b, acc_b) = refs[-11:]
    tq = acc_a.shape[0]
    for m_scr, acc_scr in ((m_a, acc_a), (m_b, acc_b)):
        m_scr[...] = jnp.full(m_scr.shape, -jnp.inf, F32)
        acc_scr[...] = jnp.zeros_like(acc_scr)
    q_a = (0, tq)
    q_b = (tq, 2 * tq)
    s_a = (sa0, sa1)
    s_b = (sb0, sb1)

    def chunk_rows(ci):
        return pl.ds(pl.multiple_of(kv_start + ci * kv_chunk, kv_chunk), kv_chunk)

    def scores(q_rows, ci, s_ref):
        q = q_ref[0, q_rows[0]:q_rows[1], :]
        s_ref[...] = lax.dot_general(q, k_ref[0, chunk_rows(ci), :], (((1,), (1,)), ((), ())),
                                     preferred_element_type=F32)

    def softmax(s_ref, m_scr):
        s = s_ref[...]
        m_old = m_scr[...]
        m_new = jnp.maximum(m_old, jnp.max(s, axis=-1, keepdims=True))
        alpha = jnp.exp2(m_old - m_new)
        pr = jnp.exp2((s - m_new).astype(BF16))
        m_scr[...] = m_new
        return pr, alpha

    def values(acc_scr, alpha, pr, ci):
        acc_scr[...] = alpha * acc_scr[...] + jnp.dot(pr, v_ref[0, chunk_rows(ci), :], preferred_element_type=F32)

    def step(ci, cur, first, last):
        if not first:
            values(acc_b, alpha_b[...], pb_scr[...], ci - 1)
        if not last:
            scores(q_a, ci + 1, s_a[1 - cur])
        pr, alpha = softmax(s_a[cur], m_a)
        values(acc_a, alpha, pr, ci)
        if not last:
            scores(q_b, ci + 1, s_b[1 - cur])
        pr, alpha = softmax(s_b[cur], m_b)
        pb_scr[...] = pr
        alpha_b[...] = alpha

    scores(q_a, 0, sa0)
    scores(q_b, 0, sb0)
    step(0, 0, True, n_kv == 1)
    if n_kv > 1:
        n_mid = n_kv - 2
        n_pairs = n_mid // 2

        def pair(t, carry):
            step(1 + 2 * t, 1, False, False)
            step(2 + 2 * t, 0, False, False)
            return carry

        lax.fori_loop(0, n_pairs, pair, 0)
        if n_mid % 2 == 1:
            step(n_kv - 2, (n_kv - 2) % 2, False, False)
        step(n_kv - 1, (n_kv - 1) % 2, False, True)
    values(acc_b, alpha_b[...], pb_scr[...], n_kv - 1)
    o_ref[0:tq, :] = (acc_a[:, 0:V_DIM] / acc_a[:, V_DIM:V_DIM + 1]).astype(o_ref.dtype)
    o_ref[tq:2 * tq, :] = (acc_b[:, 0:V_DIM] / acc_b[:, V_DIM:V_DIM + 1]).astype(o_ref.dtype)


def _attn_call(q, k, v, o_prev, q_tile, q_block0, n_q, kv_start, kv_chunk, n_kv):
    n = q.shape[1]
    tq = q_tile // 2
    in_specs = [
        pl.BlockSpec((1, q_tile, QK_DIM), lambda h, i: (h, q_block0 + i, 0)),
        pl.BlockSpec((1, n, QK_DIM), lambda h, i: (h, 0, 0)),
        pl.BlockSpec((1, n, V_SLOT), lambda h, i: (h, 0, 0)),
    ]
    args = [q, k, v]
    aliases = {}
    if o_prev is not None:
        in_specs.append(pl.BlockSpec(memory_space=pl.ANY))
        args.append(o_prev)
        aliases = {3: 0}
    return pl.pallas_call(
        functools.partial(_attn_kernel, kv_start, kv_chunk, n_kv),
        grid=(HEADS, n_q),
        in_specs=in_specs,
        out_specs=pl.BlockSpec((q_tile, V_DIM), lambda h, i: (q_block0 + i, h)),
        out_shape=jax.ShapeDtypeStruct((n, MLA_WIDTH), BF16),
        scratch_shapes=[pltpu.VMEM((tq, kv_chunk), F32)] * 4
                       + [pltpu.VMEM((tq, kv_chunk), BF16), pltpu.VMEM((tq, 1), F32)]
                       + [pltpu.VMEM((tq, 1), F32), pltpu.VMEM((tq, V_SLOT), F32)] * 2,
        input_output_aliases=aliases,
        compiler_params=_params(2),
        name="mla_attn",
    )(*args)


def _out_proj_kernel(n_src, n_lat_tiles, *refs):
    (mod_ref, yf_ref, yb_ref, cv_ref, g_ref, hf_ref, hb_ref, gate_ref, om_ref,
     lng_ref, lnb_ref, ones_ref, w_ref, o_ref) = refs[n_src:]
    ones = ones_ref[...]
    y = yf_ref[...] + yb_ref[...]
    mu = _dot_exact01(y, ones) * (1.0 / HEAD_DIM)
    dlt = y - mu
    var = _dot_exact01(dlt * dlt, ones) * (1.0 / HEAD_DIM)
    yn = dlt * lax.rsqrt(var + GN_EPS) * lng_ref[...] + lnb_ref[...]
    o_rw = ((yn + cv_ref[...]) * g_ref[...]).astype(BF16)
    o_lru = ((hf_ref[...] + hb_ref[...]) * _gelu_tanh(gate_ref[...])).astype(BF16)
    acc = jnp.dot(o_rw, w_ref[0:WIDTH, :], preferred_element_type=F32)
    acc += jnp.dot(o_lru, w_ref[WIDTH:2 * WIDTH, :], preferred_element_type=F32)
    acc += jnp.dot(om_ref[...], w_ref[2 * WIDTH:, :], preferred_element_type=F32)
    o_ref[...] = _stream_tile(refs[:n_src], n_lat_tiles) + mod_ref[0, 2:3, :] * acc


def _out_proj_call(stream, n, mod, yf, yb, cv, g, hf, hb, lru, o_mla, p, n_tiles, n_lat_tiles):
    row = lambda c: pl.BlockSpec((TILE, c), lambda j: (j, 0))
    consts = [p["ln_g"], p["ln_b"], p["ones_bd"], p["w_out"]]
    x_specs, x_args = _stream_specs(stream, n_lat_tiles)
    return pl.pallas_call(
        functools.partial(_out_proj_kernel, len(x_args), n_lat_tiles),
        grid=(n_tiles,),
        in_specs=x_specs + [pl.BlockSpec((1, 6, D_MODEL), lambda j: (jnp.minimum(j // n_lat_tiles, 1), 0, 0))]
                 + [row(WIDTH)] * 6 + [pl.BlockSpec((TILE, WIDTH), lambda j: (j, 1)), row(MLA_WIDTH)]
                 + [_const_spec(a.shape) for a in consts],
        out_specs=row(D_MODEL),
        out_shape=jax.ShapeDtypeStruct((n, D_MODEL), F32),
        compiler_params=_params(1),
        name="out_proj",
    )(*x_args, mod, yf, yb, cv, g, hf, hb, lru, o_mla, *consts)


def _ffn_kernel(tm, block0, seg_lo, seg_hi, mod_row, final, *refs):
    if final:
        (x_ref, prev_ref, next_ref, mod_ref, g_ref, wg_ref, wu_ref, wd_ref, cw_ref, cb_ref, fin_ref,
         o_ref, h_scr) = refs
    else:
        (x_ref, prev_ref, next_ref, mod_ref, g_ref, wg_ref, wu_ref, wd_ref, cw_ref, cb_ref,
         o_ref, h_scr) = refs
    c = pl.program_id(1)
    ext = tm + 2 * FFN_HALO

    @pl.when(c == 0)
    def _():
        sh = mod_ref[mod_row, 3:4, :]
        sc = 1.0 + mod_ref[mod_row, 4:5, :]
        g = g_ref[...]
        h_scr[0:FFN_HALO, :] = (_rms(prev_ref[...], g) * sc + sh).astype(BF16)
        h_scr[FFN_HALO:FFN_HALO + tm, :] = (_rms(x_ref[...], g) * sc + sh).astype(BF16)
        h_scr[FFN_HALO + tm:ext, :] = (_rms(next_ref[...], g) * sc + sh).astype(BF16)
        o_ref[...] = jnp.zeros_like(o_ref)

    ge = jnp.dot(h_scr[...], wg_ref[...], preferred_element_type=F32)
    grow = (block0 + pl.program_id(0)) * tm - FFN_HALO + _row_iota(ge.shape)
    ge = jnp.where(jnp.logical_and(grow >= seg_lo, grow < seg_hi), ge, 0.0)
    up_rows = pltpu.roll(ge, 1, 0)[FFN_HALO:FFN_HALO + tm, :]
    dn_rows = pltpu.roll(ge, ext - 1, 0)[FFN_HALO:FFN_HALO + tm, :]
    gate = cb_ref[...] + up_rows * cw_ref[0:1, :] + ge[FFN_HALO:FFN_HALO + tm, :] * cw_ref[1:2, :] + dn_rows * cw_ref[2:3, :]
    up = jnp.dot(h_scr[FFN_HALO:FFN_HALO + tm, :], wu_ref[...], preferred_element_type=F32)
    act = (_silu(gate) * up).astype(BF16)
    o_ref[...] += jnp.dot(act, wd_ref[...], preferred_element_type=F32)

    @pl.when(c == pl.num_programs(1) - 1)
    def _():
        out = x_ref[...] + mod_ref[mod_row, 5:6, :] * o_ref[...]
        if final:
            out = _rms(out, fin_ref[...])
        o_ref[...] = out


def _ffn_call(x_all, o_prev, mod, p, layer, tm, block0, n_blocks, seg_lo, seg_hi, mod_row, final_g, out_rows):
    n = x_all.shape[0]
    hb = tm // FFN_HALO
    nb = n // FFN_HALO
    n_chunks = D_FF // FFN_CHUNK
    final = final_g is not None
    in_specs = [
        pl.BlockSpec((tm, D_MODEL), lambda j, c: (block0 + j, 0), pipeline_mode=pl.Buffered(1)),
        pl.BlockSpec((FFN_HALO, D_MODEL), lambda j, c: (jnp.maximum((block0 + j) * hb - 1, 0), 0)),
        pl.BlockSpec((FFN_HALO, D_MODEL), lambda j, c: (jnp.minimum((block0 + j + 1) * hb, nb - 1), 0)),
        pl.BlockSpec((2, 6, D_MODEL), lambda j, c: (0, 0, 0)),
        pl.BlockSpec((1, D_MODEL), lambda j, c: (0, 0)),
        pl.BlockSpec((None, D_MODEL, FFN_CHUNK), lambda j, c: (layer, 0, c)),
        pl.BlockSpec((None, D_MODEL, FFN_CHUNK), lambda j, c: (layer, 0, c)),
        pl.BlockSpec((None, FFN_CHUNK, D_MODEL), lambda j, c: (layer, c, 0)),
        pl.BlockSpec((3, FFN_CHUNK), lambda j, c: (0, c)),
        pl.BlockSpec((1, FFN_CHUNK), lambda j, c: (0, c)),
    ]
    args = [x_all, x_all, x_all, mod, p["norm2"], p["w_gate"], p["w_up"], p["w_down"], p["ffn_conv"], p["ffn_conv_b"]]
    if final:
        in_specs.append(pl.BlockSpec((1, D_MODEL), lambda j, c: (0, 0)))
        args.append(final_g)
    aliases = {}
    if o_prev is not None:
        in_specs.append(pl.BlockSpec(memory_space=pl.ANY))
        args.append(o_prev)
        aliases = {len(args) - 1: 0}
    kern = functools.partial(_ffn_kernel, tm, block0, seg_lo, seg_hi, mod_row, final)
    if o_prev is not None:
        kern = _drop_ref(kern, len(args) - 1)
    return pl.pallas_call(
        kern,
        grid=(n_blocks, n_chunks),
        in_specs=in_specs,
        out_specs=pl.BlockSpec((tm, D_MODEL), lambda j, c: (block0 + j, 0)),
        out_shape=jax.ShapeDtypeStruct((out_rows, D_MODEL), F32),
        scratch_shapes=[pltpu.VMEM((tm + 2 * FFN_HALO, D_MODEL), BF16)],
        input_output_aliases=aliases,
        compiler_params=_params(2),
        name="conv_ffn",
    )(*args)


def _cast_kernel(x_ref, o_ref):
    o_ref[...] = x_ref[...].astype(o_ref.dtype)


def _cast_bf16(w):
    depth, r, c = w.shape
    rb = 1 << (((2 * 1024 * 1024) // c).bit_length() - 1)
    assert (depth * r) % rb == 0 and rb % 16 == 0
    out = pl.pallas_call(
        _cast_kernel,
        grid=(depth * r // rb,),
        in_specs=[pl.BlockSpec((rb, c), lambda j: (j, 0))],
        out_specs=pl.BlockSpec((rb, c), lambda j: (j, 0)),
        out_shape=jax.ShapeDtypeStruct((depth * r, c), BF16),
        compiler_params=_params(1),
        name="cast_bf16",
    )(w.reshape(depth * r, c))
    return out.reshape(depth, r, c)


def _drop_ref(kern, idx):
    def wrapped(*refs):
        return kern(*(refs[:idx] + refs[idx + 1:]))
    return wrapped


_ROPE_PERM = np.concatenate([np.arange(16, 32), np.arange(0, 16), np.arange(48, 64), np.arange(32, 48)])


def _block_diag(blocks):
    h, n, m = blocks.shape
    eye = jnp.eye(h, dtype=blocks.dtype)
    return (eye[:, None, :, None] * blocks[:, :, None, :]).reshape(h * n, h * m)


def _scan_consts():
    idx = np.arange(CHUNK)
    lower = (idx[None, :] <= idx[:, None]).astype(np.float32)
    upper = (idx[None, :] >= idx[:, None]).astype(np.float32)
    eye_g = np.eye(GROUP, dtype=np.float32)
    bd = lambda m: np.kron(eye_g, m)
    return {
        "tri_fwd": jnp.asarray(lower, BF16), "tri_rev": jnp.asarray(upper, BF16),
        "incl_fwd": jnp.asarray(bd(lower)), "incl_rev": jnp.asarray(bd(upper)),
        "strict_fwd": jnp.asarray(bd(lower - np.eye(CHUNK, dtype=np.float32))),
        "strict_rev": jnp.asarray(bd(upper - np.eye(CHUNK, dtype=np.float32))),
        "bd_mask": jnp.asarray(bd(np.ones((CHUNK, HEAD_DIM), np.float32))),
        "eye": jnp.asarray(np.eye(GW, dtype=np.float32)),
    }


def _rope_tables(n_lat, n_ctx):
    n_freq = ROPE // 4
    rows = n_lat // GRID_W
    row = jnp.repeat(jnp.arange(rows, dtype=F32), GRID_W)
    col = jnp.tile(jnp.arange(GRID_W, dtype=F32), rows)
    inv_freq = ROPE_THETA ** (-jnp.arange(n_freq, dtype=F32) / n_freq)
    ar, ac = row[:, None] * inv_freq, col[:, None] * inv_freq
    cos = jnp.concatenate([jnp.cos(ar), jnp.cos(ar), jnp.cos(ac), jnp.cos(ac)], axis=-1)
    sin = jnp.concatenate([-jnp.sin(ar), jnp.sin(ar), -jnp.sin(ac), jnp.sin(ac)], axis=-1)
    cos = jnp.concatenate([cos, jnp.ones((n_ctx, ROPE), F32)], axis=0)
    sin = jnp.concatenate([sin, jnp.zeros((n_ctx, ROPE), F32)], axis=0)
    return {"cc": jnp.concatenate([cos, cos], axis=-1), "ss": jnp.concatenate([sin, sin], axis=-1),
            "cs": jnp.concatenate([cos, sin], axis=-1)}


def kernel(x, c, ctx, c_ctx, ada_w, ada_b, norm1, norm2, w_in, w_out, rw_conv, rw_conv_b, rw_w0, rw_w_up, rw_a0, rw_a_up, rw_g_up, rw_k_k, rw_k_a, rw_r_k, rw_ln_g, rw_ln_b, rw_v0, rw_v_down, rw_v_up, lru_conv, lru_conv_b, lru_wa, lru_ba, lru_wx, lru_bx, lru_lambda, mla_q_norm, mla_w_qb, mla_kv_norm, mla_w_kvb, ffn_w_gate, ffn_w_up, ffn_conv, ffn_conv_b, ffn_w_down, final_norm):
    assert x.shape[0] == 1 and ctx.shape[0] == 1
    depth = ada_w.shape[0]
    n_lat, n_ctx = x.shape[1], ctx.shape[1]
    assert n_lat % FFN_TILE == 0 and n_ctx % TILE == 0 and n_lat % GRID_W == 0
    n = n_lat + n_ctx
    n_lat_tiles = n_lat // TILE
    seg = ((0, n_lat), (n_lat, n))

    x_all = (x[0], ctx[0])
    mods =_mod_call(jnp.stack([c[0], c_ctx], axis=1), ada_w, ada_b).reshape(depth, 2, 6, D_MODEL)
    tabs = _rope_tables(n_lat, n_ctx)
    consts = _scan_consts()
    ones_bd = jnp.asarray(np.kron(np.eye(HEADS, dtype=np.float32), np.ones((HEAD_DIM, HEAD_DIM), np.float32)), BF16)
    kv_chunk = 1280 if n % 1280 == 0 else TILE
    row2 = lambda a: a.reshape(1, -1)
    w_gate_bf, w_up_bf, w_down_bf = _cast_bf16(ffn_w_gate), _cast_bf16(ffn_w_up), _cast_bf16(ffn_w_down)

    v_first = None
    out = None
    for i in range(depth):
        last = i == depth - 1
        mla_off = 1760 + LRU_COLS
        rope_cols = mla_off + Q_RANK + KV_RANK + _ROPE_PERM
        vdown = rw_v_down[i - 1] if i > 0 else jnp.zeros((D_MODEL, LORA), F32)
        w_cat = jnp.concatenate([w_in[i][:, :1760], vdown, w_in[i][:, 1760:], w_in[i][:, rope_cols]], axis=1).astype(BF16)

        w_lora = jnp.zeros((RW_COLS - 3 * WIDTH, 6 * WIDTH), F32)
        b_lora = jnp.zeros((6 * WIDTH,), F32)
        for d in range(2):
            w_lora = w_lora.at[d * LORA:(d + 1) * LORA, d * WIDTH:(d + 1) * WIDTH].set(rw_w_up[i][d])
            w_lora = w_lora.at[(2 + d) * LORA:(3 + d) * LORA, (2 + d) * WIDTH:(3 + d) * WIDTH].set(rw_a_up[i][d])
            b_lora = b_lora.at[d * WIDTH:(d + 1) * WIDTH].set(rw_w0[i][d])
            b_lora = b_lora.at[(2 + d) * WIDTH:(3 + d) * WIDTH].set(rw_a0[i][d])
        w_lora = w_lora.at[4 * LORA:4 * LORA + GATE_LORA, 4 * WIDTH:5 * WIDTH].set(rw_g_up[i])
        if i > 0:
            w_lora = w_lora.at[4 * LORA + GATE_LORA:, 5 * WIDTH:].set(rw_v_up[i - 1])
            b_lora = b_lora.at[5 * WIDTH:].set(rw_v0[i - 1])
        ident = jnp.array([[0.0], [1.0], [0.0]], F32) * jnp.ones((1, LORA), F32)
        rw_p = {
            "rw_conv": jnp.concatenate([rw_conv[i], ident], axis=1),
            "rw_conv_b": row2(jnp.concatenate([rw_conv_b[i], jnp.zeros((LORA,), F32)])),
            "w_lora": w_lora.astype(BF16), "b_lora": row2(b_lora),
            "k_k": row2(rw_k_k[i]), "k_a": row2(rw_k_a[i]), "r_k": row2(rw_r_k[i]), "ones_bd": ones_bd,
        }
        lru_p = {
            "lru_conv": lru_conv[i], "lru_conv_b": row2(lru_conv_b[i]),
            "lru_wg": jnp.stack([jnp.concatenate([_block_diag(lru_wa[i][d]), _block_diag(lru_wx[i][d])], axis=1)
                                 for d in range(2)]).astype(BF16),
            "lru_bg": jnp.stack([jnp.concatenate([lru_ba[i][d], lru_bx[i][d]])[None] for d in range(2)]),
            "lru_lam": lru_lambda[i][:, None, :],
        }
        wq = mla_w_qb[i].reshape(Q_RANK, HEADS, NOPE + ROPE)
        mla_p = {
            "q_norm": row2(mla_q_norm[i]), "kv_norm": row2(mla_kv_norm[i]),
            "w_q": jnp.concatenate([wq[:, :, :NOPE].reshape(Q_RANK, -1), wq[:, :, NOPE:].reshape(Q_RANK, -1),
                                    wq[:, :, NOPE + _ROPE_PERM].reshape(Q_RANK, -1)], axis=1).astype(BF16),
            "w_kv": mla_w_kvb[i].astype(BF16),
        }
        out_p = {"ln_g": row2(rw_ln_g[i]), "ln_b": row2(rw_ln_b[i]), "ones_bd": ones_bd, "w_out": w_out[i].astype(BF16)}
        ffn_p = {"norm2": row2(norm2[i]), "w_gate": w_gate_bf, "w_up": w_up_bf, "w_down": w_down_bf,
                 "ffn_conv": ffn_conv[i], "ffn_conv_b": row2(ffn_conv_b[i])}

        rw, lru, q_h, k_h, v_h = _in_proj_call(x_all, n, mods[i], row2(norm1[i]), w_cat, tabs, mla_p, n_lat_tiles)

        r, v, kk, g, cv, lw, kd, bd = _rwkv_prep_call(rw, v_first, rw_p, seg)
        if i == 0:
            v_first = v
        y_f, y_b = _rwkv_scan_call(r, v, kk, lw, kd, bd, consts, n_lat_tiles)

        h_f = _lru_call(False, lru, lru_p, seg)
        h_b = _lru_call(True, lru, lru_p, seg)

        o_mla = _attn_call(q_h, k_h, v_h, None, Q_TILE, 0, n_lat // Q_TILE, 0, kv_chunk, n // kv_chunk)
        if not last:
            o_mla = _attn_call(q_h, k_h, v_h, o_mla, TILE, n_lat_tiles, n_ctx // TILE, n_lat, TILE, n_ctx // TILE)

        n_tiles = n_lat_tiles if last else n // TILE
        x_mid = _out_proj_call(x_all, n, mods[i], y_f, y_b, cv, g, h_f, h_b, lru, o_mla, out_p, n_tiles, n_lat_tiles)

        if last:
            out = _ffn_call(x_mid, None, mods[i], ffn_p, i, FFN_TILE, 0, n_lat // FFN_TILE, 0, n_lat, 0,
                            row2(final_norm), n_lat)
        else:
            x_new = _ffn_call(x_mid, None, mods[i], ffn_p, i, FFN_TILE, 0, n_lat // FFN_TILE, 0, n_lat, 0, None, n)
            x_all = _ffn_call(x_mid, x_new, mods[i], ffn_p, i, TILE, n_lat_tiles, n_ctx // TILE, n_lat, n, 1, None, n)
    return out[None]
```

```python
import functools

import numpy as np
import jax
import jax.numpy as jnp
from jax import lax
from jax.experimental import pallas as pl
from jax.experimental.pallas import tpu as pltpu

F32 = jnp.float32
BF16 = jnp.bfloat16

D_MODEL = 2048
NORM_EPS = 1e-6
GN_EPS = 64e-5
DECAY_SCALE = 0.606531
LRU_C = 8.0
HEADS = 8
HEAD_DIM = 64
WIDTH = HEADS * HEAD_DIM
LORA = 32
GATE_LORA = 96
RW_COLS = 1792
LRU_COLS = 1024
MLA_COLS = 896
Q_RANK = 512
KV_RANK = 256
NOPE = 128
ROPE = 64
V_DIM = 128
V_SLOT = 256
QK_DIM = 256
MLA_WIDTH = HEADS * V_DIM
MLA_SCALE = (NOPE + ROPE) ** -0.5
Q_SCALE = MLA_SCALE * 1.4426950408889634
ROPE_THETA = 10000.0
GRID_W = 64
D_FF = 5632

TILE = 256
CHUNK = 64
SCAN_UNROLL = 2
GROUP = 4
GW = GROUP * HEAD_DIM
HALO = 8
FFN_TILE = 1024
FFN_HALO = 16
FFN_CHUNK = 512
Q_TILE = 1024
VMEM_LIMIT = 56 * 1024 * 1024


def _params(n_axes, vmem=VMEM_LIMIT):
    return pltpu.CompilerParams(dimension_semantics=("arbitrary",) * n_axes, vmem_limit_bytes=vmem)


def _const_spec(shape):
    nd = len(shape)
    return pl.BlockSpec(shape, lambda *_: (0,) * nd, pipeline_mode=pl.Buffered(1))


def _sigmoid(x):
    return 1.0 / (1.0 + jnp.exp(-x))


def _silu(x):
    return x * _sigmoid(x)


def _gelu_tanh(x):
    return 0.5 * x * (1.0 + jnp.tanh(0.7978845608028654 * (x + 0.044715 * (x * x * x))))


def _rms(x, g):
    return x * lax.rsqrt(jnp.mean(x * x, axis=-1, keepdims=True) + NORM_EPS) * g


def _dot_exact01(x, w01, left=False):
    out = None
    rem = x
    for _ in range(3):
        part = rem.astype(BF16)
        rem = rem - part.astype(F32)
        term = jnp.dot(w01, part, preferred_element_type=F32) if left else jnp.dot(part, w01, preferred_element_type=F32)
        out = term if out is None else out + term
    return out


def _row_iota(shape):
    return lax.broadcasted_iota(jnp.int32, shape, 0)


def _shift_down(cur, prev_rows, k):
    out = pltpu.roll(cur, k, 0)
    rows = _row_iota(cur.shape)
    for i in range(k):
        out = jnp.where(rows == i, prev_rows[i:i + 1, :], out)
    return out


def _shift_up(cur, next_row):
    n = cur.shape[0]
    out = pltpu.roll(cur, n - 1, 0)
    return jnp.where(_row_iota(cur.shape) == n - 1, next_row, out)


def _halo_valid(j, tile, seg_starts, seg_ends):
    first = j * tile
    last = first + tile
    lvalid = jnp.logical_and(first != seg_starts[0], first != seg_starts[1])
    rvalid = jnp.logical_and(last != seg_ends[0], last != seg_ends[1])
    return lvalid.astype(F32), rvalid.astype(F32)


def _mod_kernel(cc_ref, w_ref, b_ref, o_ref):
    s = _silu(cc_ref[...])
    w = w_ref[0]
    b = b_ref[0]
    o_ref[0, 0:1, :] = jnp.sum(s[:, 0:1] * w, axis=0, keepdims=True) + b
    o_ref[0, 1:2, :] = jnp.sum(s[:, 1:2] * w, axis=0, keepdims=True) + b


def _mod_call(cc, ada_w, ada_b):
    depth, d, n6 = ada_w.shape
    tn = 1024
    return pl.pallas_call(
        _mod_kernel,
        grid=(depth, n6 // tn),
        in_specs=[
            pl.BlockSpec((d, 2), lambda i, j: (0, 0)),
            pl.BlockSpec((1, d, tn), lambda i, j: (i, 0, j)),
            pl.BlockSpec((1, 1, tn), lambda i, j: (i, 0, j)),
        ],
        out_specs=pl.BlockSpec((1, 2, tn), lambda i, j: (i, 0, j)),
        out_shape=jax.ShapeDtypeStruct((depth, 2, n6), F32),
        compiler_params=_params(2),
        name="adaln_mod",
    )(cc, ada_w, ada_b.reshape(depth, 1, n6))


def _stream_specs(stream, n_lat_tiles):
    if not isinstance(stream, tuple):
        return [pl.BlockSpec((TILE, D_MODEL), lambda j: (j, 0))], [stream]
    return ([pl.BlockSpec((TILE, D_MODEL), lambda j: (jnp.minimum(j, n_lat_tiles - 1), 0)),
             pl.BlockSpec((TILE, D_MODEL), lambda j: (jnp.maximum(j - n_lat_tiles, 0), 0))], list(stream))


def _stream_tile(refs, n_lat_tiles):
    if len(refs) == 1:
        return refs[0][...]
    return jnp.where(pl.program_id(0) >= n_lat_tiles, refs[1][...], refs[0][...])


def _in_proj_kernel(n_src, n_lat_tiles, *refs):
    (mod_ref, g_ref, w_ref, cc_ref, ss_ref, cs_ref, gq_ref, gkv_ref, wq_ref, wkv_ref,
     rw_ref, lru_ref, q_ref, k_ref, v_ref) = refs[n_src:]
    x = _stream_tile(refs[:n_src], n_lat_tiles)
    h = _rms(x, g_ref[...]) * (1.0 + mod_ref[0, 1:2, :]) + mod_ref[0, 0:1, :]
    hb = h.astype(BF16)
    rw_ref[...] = jnp.dot(hb, w_ref[:, 0:RW_COLS], preferred_element_type=F32)
    lru_ref[...] = jnp.dot(hb, w_ref[:, RW_COLS:RW_COLS + LRU_COLS], preferred_element_type=F32)
    mla = jnp.dot(hb, w_ref[:, RW_COLS + LRU_COLS:], preferred_element_type=F32)
    _mla_project(mla, cc_ref, ss_ref, cs_ref, gq_ref, gkv_ref, wq_ref, wkv_ref, q_ref, k_ref, v_ref)


def _in_proj_call(stream, n, mod, g, w_cat, tabs, mla_p, n_lat_tiles):
    cols = w_cat.shape[1]
    row = lambda c: pl.BlockSpec((TILE, c), lambda j: (j, 0))
    head = lambda c: pl.BlockSpec((HEADS, TILE, c), lambda j: (0, j, 0))
    x_specs, x_args = _stream_specs(stream, n_lat_tiles)
    consts = [mla_p["q_norm"], mla_p["kv_norm"], mla_p["w_q"], mla_p["w_kv"]]
    return pl.pallas_call(
        functools.partial(_in_proj_kernel, len(x_args), n_lat_tiles),
        grid=(n // TILE,),
        in_specs=x_specs + [
            pl.BlockSpec((1, 6, D_MODEL), lambda j: (jnp.minimum(j // n_lat_tiles, 1), 0, 0)),
            _const_spec((1, D_MODEL)),
            _const_spec((D_MODEL, cols)),
            row(2 * ROPE), row(2 * ROPE), row(2 * ROPE),
        ] + [_const_spec(a.shape) for a in consts],
        out_specs=[row(RW_COLS), row(LRU_COLS), head(QK_DIM), head(QK_DIM), head(V_SLOT)],
        out_shape=[jax.ShapeDtypeStruct((n, RW_COLS), F32), jax.ShapeDtypeStruct((n, LRU_COLS), F32),
                   jax.ShapeDtypeStruct((HEADS, n, QK_DIM), BF16), jax.ShapeDtypeStruct((HEADS, n, QK_DIM), BF16),
                   jax.ShapeDtypeStruct((HEADS, n, V_SLOT), BF16)],
        compiler_params=_params(1),
        name="in_proj",
    )(*x_args, mod, g, w_cat, tabs["cc"], tabs["ss"], tabs["cs"], *consts)


def _rwkv_prep_kernel(seg, has_vfirst, *refs):
    if has_vfirst:
        (cur_ref, prev_ref, next_ref, vf_ref, cw_ref, cb_ref, wl_ref, bl_ref, kk_ref, ka_ref, rk_ref, ones_ref,
         r_out, v_out, kk_out, g_out, cv_out, lw_out, kd_out, bd_out) = refs
    else:
        (cur_ref, prev_ref, next_ref, cw_ref, cb_ref, wl_ref, bl_ref, kk_ref, ka_ref, rk_ref, ones_ref,
         r_out, v_out, kk_out, g_out, cv_out, lw_out, kd_out, bd_out) = refs
    lvalid, rvalid = _halo_valid(pl.program_id(0), TILE, seg[0], seg[1])
    cur = cur_ref[...]
    up = _shift_down(cur, prev_ref[HALO - 1:HALO, :] * lvalid, 1)
    dn = _shift_up(cur, next_ref[0:1, :] * rvalid)
    u = cb_ref[...] + up * cw_ref[0:1, :] + cur * cw_ref[1:2, :] + dn * cw_ref[2:3, :]
    r = u[:, 0:WIDTH]
    k = u[:, WIDTH:2 * WIDTH]
    v = u[:, 2 * WIDTH:3 * WIDTH]
    blk = u[:, 3 * WIDTH:RW_COLS]
    lane = lax.broadcasted_iota(jnp.int32, blk.shape, 1)
    act = jnp.where(lane < 2 * LORA, jnp.tanh(blk),
                    jnp.where(jnp.logical_and(lane >= 4 * LORA, lane < 4 * LORA + GATE_LORA), _sigmoid(blk), blk))
    lo = jnp.dot(act.astype(BF16), wl_ref[...], preferred_element_type=F32) + bl_ref[...]
    g = lo[:, 4 * WIDTH:5 * WIDTH]
    if has_vfirst:
        mix = _sigmoid(lo[:, 5 * WIDTH:6 * WIDTH])
        v = v + (vf_ref[...] - v) * mix
    ones = ones_ref[...]
    kk = k * kk_ref[...]
    ss = _dot_exact01(kk * kk, ones)
    kk = kk * lax.rsqrt(jnp.maximum(ss, 1e-24))
    ksum = None
    for d in range(2):
        lw_out[d] = -DECAY_SCALE * _sigmoid(lo[:, d * WIDTH:(d + 1) * WIDTH])
        iclr = _sigmoid(lo[:, (2 + d) * WIDTH:(3 + d) * WIDTH])
        kd = k * (1.0 + (iclr - 1.0) * ka_ref[...])
        kd_out[d] = kd
        bd_out[d] = kk * iclr
        ksum = kd if ksum is None else ksum + kd
    coef = _dot_exact01(r * ksum * rk_ref[...], ones)
    r_out[...] = r
    v_out[...] = v
    kk_out[...] = kk
    g_out[...] = g
    cv_out[...] = coef * v


def _rwkv_prep_call(rw, v_first, p, seg):
    n = rw.shape[0]
    hb = TILE // HALO
    nb = n // HALO
    row = lambda c: pl.BlockSpec((TILE, c), lambda j: (j, 0))
    dir_row = pl.BlockSpec((2, TILE, WIDTH), lambda j: (0, j, 0))
    has_vf = v_first is not None
    in_specs = [
        row(RW_COLS),
        pl.BlockSpec((HALO, RW_COLS), lambda j: (jnp.maximum(j * hb - 1, 0), 0)),
        pl.BlockSpec((HALO, RW_COLS), lambda j: (jnp.minimum((j + 1) * hb, nb - 1), 0)),
    ]
    args = [rw, rw, rw]
    if has_vf:
        in_specs.append(row(WIDTH))
        args.append(v_first)
    consts = [p["rw_conv"], p["rw_conv_b"], p["w_lora"], p["b_lora"], p["k_k"], p["k_a"], p["r_k"], p["ones_bd"]]
    in_specs += [_const_spec(a.shape) for a in consts]
    args += consts
    return pl.pallas_call(
        functools.partial(_rwkv_prep_kernel, seg, has_vf),
        grid=(n // TILE,),
        in_specs=in_specs,
        out_specs=[row(WIDTH)] * 5 + [dir_row] * 3,
        out_shape=[jax.ShapeDtypeStruct((n, WIDTH), F32)] * 5 + [jax.ShapeDtypeStruct((2, n, WIDTH), F32)] * 3,
        compiler_params=_params(1),
        name="rwkv_prep",
    )(*args)


def _bd_stack(x, bd_mask):
    return jnp.concatenate([x] * GROUP, axis=0) * bd_mask


def _mm(a, b):
    return jnp.dot(a.astype(BF16), b.astype(BF16), preferred_element_type=F32)


def _mm_nt(a, b):
    return lax.dot_general(a.astype(BF16), b.astype(BF16), (((1,), (1,)), ((), ())), preferred_element_type=F32)


def _mm_tn(a, b):
    return lax.dot_general(a.astype(BF16), b.astype(BF16), (((0,), (0,)), ((), ())), preferred_element_type=F32)


def _scan_operands(reverse, rows, r_ref, v_ref, kk_ref, lw_ref, kd_ref, bd_ref, tri, strict, incl, bdm, h_ref):
    last = 0 if reverse else CHUNK - 1
    lw = lw_ref[0, rows, :]
    cl = _dot_exact01(lw, tri, left=True)
    tot = cl[last:last + 1, :]
    e_cl = jnp.exp(cl)
    e_cle = jnp.exp(cl - lw)
    e_ncl = jnp.exp(-cl)
    e_tc = jnp.exp(tot - cl)
    e_tot = jnp.exp(tot)
    a_t = -kk_ref[rows, :] * e_cle
    r_t = r_ref[rows, :] * e_cl
    kd = kd_ref[0, rows, :]
    bd = bd_ref[0, rows, :]
    b_t = bd * e_ncl
    k_t = kd * e_ncl
    b_h = bd * e_tc
    k_h = kd * e_tc
    v = v_ref[rows, :]
    probs = []
    for gi in range(HEADS // GROUP):
        ln = slice(gi * GW, (gi + 1) * GW)
        r_s = _bd_stack(r_t[:, ln], bdm)
        probs.append(dict(
            r_s=r_s, r_sb=r_s.astype(BF16),
            a_s=_bd_stack(a_t[:, ln], bdm).astype(BF16),
            b_s=_bd_stack(b_t[:, ln], bdm).astype(BF16),
            k_s=_bd_stack(k_t[:, ln], bdm).astype(BF16),
            bh_s=_bd_stack(b_h[:, ln], bdm).astype(BF16),
            kh_s=_bd_stack(k_h[:, ln], bdm).astype(BF16),
            v_s=_bd_stack(v[:, ln], bdm).astype(BF16),
            e_tot=e_tot[:, ln], strict=strict, incl=incl, h_ref=h_ref, gi=gi))
    return probs


def _scan_solve(probs, eye):
    for p in probs:
        p["a_ab"] = _mm_nt(p["a_s"], p["b_s"]) * p["strict"]
    for p in probs:
        p["a_ak"] = _mm_nt(p["a_s"], p["k_s"]) * p["strict"]
    for p in probs:
        p["a_rb"] = (_mm_nt(p["r_sb"], p["b_s"]) * p["incl"]).astype(BF16)
    for p in probs:
        p["a_rk"] = _mm_nt(p["r_sb"], p["k_s"]) * p["incl"]
    for p in probs:
        p["t"] = eye + p["a_ab"]
        p["pw"] = p["a_ab"]
        p["x"] = _mm(p["a_ak"], p["v_s"])
    for _ in range(5):
        for p in probs:
            p["pw"] = _mm(p["pw"], p["pw"])
        for p in probs:
            p["t"] = p["t"] + _mm(p["t"], p["pw"])
    for p in probs:
        p["t"] = p["t"].astype(BF16)
        p["abar"] = _mm(p["t"], p["a_s"]).astype(BF16)
    for p in probs:
        p["u0"] = _mm(p["t"], p["x"]).astype(BF16)
    for p in probs:
        p["m"] = eye * p["e_tot"] + _mm_tn(p["bh_s"], p["abar"])
    for p in probs:
        p["g"] = _mm_tn(p["bh_s"], p["u0"]) + _mm_tn(p["kh_s"], p["v_s"])
    for p in probs:
        p["rbar"] = p["r_s"] + _mm(p["a_rb"], p["abar"])
    for p in probs:
        p["y0"] = _mm(p["a_rb"], p["u0"]) + _mm(p["a_rk"], p["v_s"])
    ys = []
    for p in probs:
        h = p["h_ref"][p["gi"]].astype(BF16)
        y_bd = _mm(p["rbar"], h) + p["y0"]
        p["h_ref"][p["gi"]] = _mm(p["m"], h) + p["g"]
        y = y_bd[0:CHUNK, :]
        for hh in range(1, GROUP):
            y = y + y_bd[hh * CHUNK:(hh + 1) * CHUNK, :]
        ys.append(y)
    return ys


def _rwkv_scan_kernel(rf_ref, vf_ref, kkf_ref, lwf_ref, kdf_ref, bdf_ref, rb_ref, vb_ref, kkb_ref, lwb_ref, kdb_ref, bdb_ref,
                      trif_ref, trib_ref, strictf_ref, strictb_ref, inclf_ref, inclb_ref, bdm_ref, eye_ref,
                      yf_ref, yb_ref, hf_ref, hb_ref):
    @pl.when(pl.program_id(0) == 0)
    def _():
        hf_ref[...] = jnp.zeros_like(hf_ref)
        hb_ref[...] = jnp.zeros_like(hb_ref)

    n_chunks = TILE // CHUNK
    bdm = bdm_ref[...]
    eye = eye_ref[...]

    n_groups = HEADS // GROUP

    def chunk_body(pi, carry):
        probs, rows = [], []
        for ci in [SCAN_UNROLL * pi + u for u in range(SCAN_UNROLL)]:
            rows_f = pl.ds(pl.multiple_of(ci * CHUNK, CHUNK), CHUNK)
            rows_b = pl.ds(pl.multiple_of((n_chunks - 1 - ci) * CHUNK, CHUNK), CHUNK)
            probs += _scan_operands(False, rows_f, rf_ref, vf_ref, kkf_ref, lwf_ref, kdf_ref, bdf_ref,
                                    trif_ref[...], strictf_ref[...], inclf_ref[...], bdm, hf_ref)
            probs += _scan_operands(True, rows_b, rb_ref, vb_ref, kkb_ref, lwb_ref, kdb_ref, bdb_ref,
                                    trib_ref[...], strictb_ref[...], inclb_ref[...], bdm, hb_ref)
            rows.append((rows_f, rows_b))
        ys = _scan_solve(probs, eye)
        for k, (rows_f, rows_b) in enumerate(rows):
            base = 2 * n_groups * k
            yf_ref[rows_f, :] = jnp.concatenate(ys[base:base + n_groups], axis=1)
            yb_ref[rows_b, :] = jnp.concatenate(ys[base + n_groups:base + 2 * n_groups], axis=1)
        return carry

    lax.fori_loop(0, n_chunks // SCAN_UNROLL, chunk_body, 0)


def _scan_tile(reverse, n_lat_tiles):
    if reverse:
        return lambda j: jnp.where(j == 0, n_lat_tiles, n_lat_tiles - j)
    return lambda j: jnp.where(j == 0, n_lat_tiles, j - 1)


def _rwkv_scan_call(r, v, kk, lw, kd, bd, consts, n_lat_tiles):
    n = r.shape[0]
    specs = []
    for d, reverse in enumerate((False, True)):
        tile = _scan_tile(reverse, n_lat_tiles)
        row = pl.BlockSpec((TILE, WIDTH), lambda j, tile=tile: (tile(j), 0))
        dir_row = pl.BlockSpec((1, TILE, WIDTH), lambda j, tile=tile, d=d: (d, tile(j), 0))
        specs.append((row, dir_row))
    (row_f, dir_f), (row_b, dir_b) = specs
    cs = [consts["tri_fwd"], consts["tri_rev"], consts["strict_fwd"], consts["strict_rev"],
          consts["incl_fwd"], consts["incl_rev"], consts["bd_mask"], consts["eye"]]
    state = pltpu.VMEM((HEADS // GROUP, GW, GW), F32)
    return pl.pallas_call(
        _rwkv_scan_kernel,
        grid=(n // TILE,),
        in_specs=[row_f] * 3 + [dir_f] * 3 + [row_b] * 3 + [dir_b] * 3 + [_const_spec(a.shape) for a in cs],
        out_specs=[row_f, row_b],
        out_shape=[jax.ShapeDtypeStruct((n, WIDTH), F32)] * 2,
        scratch_shapes=[state, state],
        compiler_params=_params(1),
        name="rwkv_scan",
    )(r, v, kk, lw, kd, bd, r, v, kk, lw, kd, bd, *cs)


def _lru_kernel(seg, cur_f, prev_f, next_f, cur_b, prev_b, next_b, cw_ref, cb_ref, wg_ref, bg_ref, lam_ref,
                hs_f, hs_b, a_f, b_f, h_f, a_b, b_b, h_b):
    _lru_direction(False, seg, cur_f, prev_f, next_f, cw_ref, cb_ref, wg_ref, bg_ref, lam_ref, hs_f, a_f, b_f, h_f)
    _lru_direction(True, seg, cur_b, prev_b, next_b, cw_ref, cb_ref, wg_ref, bg_ref, lam_ref, hs_b, a_b, b_b, h_b)


def _lru_direction(reverse, seg, cur_ref, prev_ref, next_ref, cw_ref, cb_ref, wg_ref, bg_ref, lam_ref,
                   hs_ref, a_scr, b_scr, h_scr):
    d = 1 if reverse else 0
    j = pl.program_id(0)

    @pl.when(j == 0)
    def _():
        h_scr[...] = jnp.zeros_like(h_scr)

    tile_idx = cur_tile_index(reverse, seg, j)
    lvalid, rvalid = _halo_valid(tile_idx, TILE, seg[0], seg[1])
    cur = cur_ref[...]
    prev = prev_ref[...] * lvalid
    x2 = _shift_down(cur, prev[HALO - 2:HALO, :], 2)
    x1 = _shift_down(cur, prev[HALO - 1:HALO, :], 1)
    xn = _shift_up(cur, next_ref[0:1, :] * rvalid)
    xb = cb_ref[...] + x2 * cw_ref[0:1, :] + x1 * cw_ref[1:2, :] + cur * cw_ref[2:3, :] + xn * cw_ref[3:4, :]
    gates = jnp.dot(xb.astype(BF16), wg_ref[d], preferred_element_type=F32) + bg_ref[d]
    gate_r = _sigmoid(gates[:, 0:WIDTH])
    gate_i = _sigmoid(gates[:, WIDTH:2 * WIDTH])
    lam = lam_ref[d]
    softplus = jnp.maximum(-lam, 0.0) + jnp.log(1.0 + jnp.exp(-jnp.abs(lam)))
    log_a = -LRU_C * gate_r * softplus
    a = jnp.exp(log_a)
    b = jnp.sqrt(1.0 - a * a) * gate_i * xb

    in_group = _row_iota((TILE, WIDTH)) % 8
    for s in (1, 2, 4):
        if reverse:
            a_sh = pltpu.roll(a, TILE - s, 0)
            b_sh = pltpu.roll(b, TILE - s, 0)
            ok = in_group < 8 - s
        else:
            a_sh = pltpu.roll(a, s, 0)
            b_sh = pltpu.roll(b, s, 0)
            ok = in_group >= s
        b = jnp.where(ok, a * b_sh + b, b)
        a = jnp.where(ok, a * a_sh, a)
    a_scr[...] = a
    b_scr[...] = b

    n_groups = TILE // 8

    def group_body(gi, h):
        g = (n_groups - 1 - gi) if reverse else gi
        rows = pl.ds(pl.multiple_of(g * 8, 8), 8)
        hs = a_scr[rows, :] * h + b_scr[rows, :]
        hs_ref[rows, :] = hs
        return hs[0:1, :] if reverse else hs[7:8, :]

    h_scr[...] = lax.fori_loop(0, n_groups, group_body, h_scr[...], unroll=4)


def cur_tile_index(reverse, seg, j):
    n_lat_tiles = seg[1][0] // TILE
    return _scan_tile(reverse, n_lat_tiles)(j)


def _lru_call(lru, p, seg):
    n = lru.shape[0]
    hb = TILE // HALO
    nb = n // HALO
    n_lat_tiles = seg[1][0] // TILE
    tile_specs, out_specs = [], []
    for reverse in (False, True):
        tile = _scan_tile(reverse, n_lat_tiles)
        tile_specs += [
            pl.BlockSpec((TILE, WIDTH), lambda j, tile=tile: (tile(j), 0)),
            pl.BlockSpec((HALO, WIDTH), lambda j, tile=tile: (jnp.maximum(tile(j) * hb - 1, 0), 0)),
            pl.BlockSpec((HALO, WIDTH), lambda j, tile=tile: (jnp.minimum((tile(j) + 1) * hb, nb - 1), 0)),
        ]
        out_specs.append(pl.BlockSpec((TILE, WIDTH), lambda j, tile=tile: (tile(j), 0)))
    consts = [p["lru_conv"], p["lru_conv_b"], p["lru_wg"], p["lru_bg"], p["lru_lam"]]
    state = [pltpu.VMEM((TILE, WIDTH), F32), pltpu.VMEM((TILE, WIDTH), F32), pltpu.VMEM((1, WIDTH), F32)]
    return pl.pallas_call(
        functools.partial(_lru_kernel, seg),
        grid=(n // TILE,),
        in_specs=tile_specs + [_const_spec(a.shape) for a in consts],
        out_specs=out_specs,
        out_shape=[jax.ShapeDtypeStruct((n, WIDTH), F32)] * 2,
        scratch_shapes=state + state,
        compiler_params=_params(1),
        name="lru",
    )(lru, lru, lru, lru, lru, lru, *consts)


def _mla_project(cols, cc_ref, ss_ref, cs_ref, gq_ref, gkv_ref, wq_ref, wkv_ref, q_ref, k_ref, v_ref):
    qn =_rms(cols[:, 0:Q_RANK], gq_ref[...]).astype(BF16)
    q = jnp.dot(qn, wq_ref[...], preferred_element_type=F32)
    kvn = _rms(cols[:, Q_RANK:Q_RANK + KV_RANK], gkv_ref[...]).astype(BF16)
    kv = jnp.dot(kvn, wkv_ref[...], preferred_element_type=F32)
    kr = cols[:, Q_RANK + KV_RANK:MLA_COLS] * cs_ref[...]
    kr = (kr + pltpu.roll(kr, ROPE, 1)).astype(BF16)
    lane = lax.broadcasted_iota(jnp.int32, (TILE, 2 * ROPE), 1)
    ones_col = jnp.where(lax.broadcasted_iota(jnp.int32, (TILE, V_SLOT - V_DIM), 1) == 0, 1.0, 0.0).astype(BF16)
    cc = cc_ref[...]
    ss = ss_ref[...]
    for hp in range(HEADS // 2):
        sl = slice(HEADS * NOPE + hp * 2 * ROPE, HEADS * NOPE + (hp + 1) * 2 * ROPE)
        sw = slice(HEADS * NOPE + HEADS * ROPE + hp * 2 * ROPE, HEADS * NOPE + HEADS * ROPE + (hp + 1) * 2 * ROPE)
        roped = (q[:, sl] * cc + q[:, sw] * ss) * Q_SCALE
        for e in range(2):
            h = 2 * hp + e
            q_ref[h, :, 0:NOPE] = (q[:, h * NOPE:(h + 1) * NOPE] * Q_SCALE).astype(BF16)
            keep = (lane < ROPE) if e == 0 else (lane >= ROPE)
            q_ref[h, :, NOPE:QK_DIM] = jnp.where(keep, roped, 0.0).astype(BF16)
            k_ref[h, :, 0:NOPE] = kv[:, h * 2 * NOPE:h * 2 * NOPE + NOPE].astype(BF16)
            k_ref[h, :, NOPE:QK_DIM] = kr
            v_ref[h, :, 0:V_DIM] = kv[:, h * 2 * NOPE + NOPE:(h + 1) * 2 * NOPE].astype(BF16)
            v_ref[h, :, V_DIM:V_SLOT] = ones_col


def _attn_kernel(kv_start, kv_chunk, n_kv, *refs):
    q_ref, k_ref, v_ref = refs[0:3]
    (o_ref, sa0, sa1, sb0, sb1, pb_scr, alpha_b, m_a, acc_a, m_b, acc_b) = refs[-11:]
    tq = acc_a.shape[0]
    for m_scr, acc_scr in ((m_a, acc_a), (m_b, acc_b)):
        m_scr[...] = jnp.full(m_scr.shape, -jnp.inf, F32)
        acc_scr[...] = jnp.zeros_like(acc_scr)
    q_a = (0, tq)
    q_b = (tq, 2 * tq)
    s_a = (sa0, sa1)
    s_b = (sb0, sb1)

    def chunk_rows(ci):
        return pl.ds(pl.multiple_of(kv_start + ci * kv_chunk, kv_chunk), kv_chunk)

    def scores(q_rows, ci, s_ref):
        q = q_ref[0, q_rows[0]:q_rows[1], :]
        s_ref[...] = lax.dot_general(q, k_ref[0, chunk_rows(ci), :], (((1,), (1,)), ((), ())),
                                     preferred_element_type=F32)

    def softmax(s_ref, m_scr):
        s = s_ref[...]
        m_old = m_scr[...]
        m_new = jnp.maximum(m_old, jnp.max(s, axis=-1, keepdims=True))
        alpha = jnp.exp2(m_old - m_new)
        pr = jnp.exp2((s - m_new).astype(BF16))
        m_scr[...] = m_new
        return pr, alpha

    def values(acc_scr, alpha, pr, ci):
        acc_scr[...] = alpha * acc_scr[...] + jnp.dot(pr, v_ref[0, chunk_rows(ci), :], preferred_element_type=F32)

    def step(ci, cur, first, last):
        if not first:
            values(acc_b, alpha_b[...], pb_scr[...], ci - 1)
        if not last:
            scores(q_a, ci + 1, s_a[1 - cur])
            scores(q_b, ci + 1, s_b[1 - cur])
        pr, alpha = softmax(s_a[cur], m_a)
        values(acc_a, alpha, pr, ci)
        pr, alpha = softmax(s_b[cur], m_b)
        pb_scr[...] = pr
        alpha_b[...] = alpha

    scores(q_a, 0, sa0)
    scores(q_b, 0, sb0)
    step(0, 0, True, n_kv == 1)
    if n_kv > 1:
        n_mid = n_kv - 2
        n_pairs = n_mid // 2

        def pair(t, carry):
            step(1 + 2 * t, 1, False, False)
            step(2 + 2 * t, 0, False, False)
            return carry

        lax.fori_loop(0, n_pairs, pair, 0)
        if n_mid % 2 == 1:
            step(n_kv - 2, (n_kv - 2) % 2, False, False)
        step(n_kv - 1, (n_kv - 1) % 2, False, True)
    values(acc_b, alpha_b[...], pb_scr[...], n_kv - 1)
    o_ref[0:tq, :] = (acc_a[:, 0:V_DIM] / acc_a[:, V_DIM:V_DIM + 1]).astype(o_ref.dtype)
    o_ref[tq:2 * tq, :] = (acc_b[:, 0:V_DIM] / acc_b[:, V_DIM:V_DIM + 1]).astype(o_ref.dtype)


def _attn_call(q, k, v, o_prev, q_tile, q_block0, n_q, kv_start, kv_chunk, n_kv):
    n = q.shape[1]
    tq = q_tile // 2
    in_specs = [
        pl.BlockSpec((1, q_tile, QK_DIM), lambda h, i: (h, q_block0 + i, 0)),
        pl.BlockSpec((1, n, QK_DIM), lambda h, i: (h, 0, 0)),
        pl.BlockSpec((1, n, V_SLOT), lambda h, i: (h, 0, 0)),
    ]
    args = [q, k, v]
    aliases = {}
    if o_prev is not None:
        in_specs.append(pl.BlockSpec(memory_space=pl.ANY))
        args.append(o_prev)
        aliases = {3: 0}
    return pl.pallas_call(
        functools.partial(_attn_kernel, kv_start, kv_chunk, n_kv),
        grid=(HEADS, n_q),
        in_specs=in_specs,
        out_specs=pl.BlockSpec((q_tile, V_DIM), lambda h, i: (q_block0 + i, h)),
        out_shape=jax.ShapeDtypeStruct((n, MLA_WIDTH), BF16),
        scratch_shapes=[pltpu.VMEM((tq, kv_chunk), F32)] * 4
                       + [pltpu.VMEM((tq, kv_chunk), BF16), pltpu.VMEM((tq, 1), F32)]
                       + [pltpu.VMEM((tq, 1), F32), pltpu.VMEM((tq, V_SLOT), F32)] * 2,
        input_output_aliases=aliases,
        compiler_params=_params(2),
        name="mla_attn",
    )(*args)


def _out_proj_kernel(n_src, n_lat_tiles, *refs):
    (mod_ref, yf_ref, yb_ref, cv_ref, g_ref, hf_ref, hb_ref, gate_ref, om_ref,
     lng_ref, lnb_ref, ones_ref, w_ref, o_ref) = refs[n_src:]
    ones = ones_ref[...]
    y = yf_ref[...] + yb_ref[...]
    mu = _dot_exact01(y, ones) * (1.0 / HEAD_DIM)
    dlt = y - mu
    var = _dot_exact01(dlt * dlt, ones) * (1.0 / HEAD_DIM)
    yn = dlt * lax.rsqrt(var + GN_EPS) * lng_ref[...] + lnb_ref[...]
    o_rw = ((yn + cv_ref[...]) * g_ref[...]).astype(BF16)
    o_lru = ((hf_ref[...] + hb_ref[...]) * _gelu_tanh(gate_ref[...])).astype(BF16)
    acc = jnp.dot(o_rw, w_ref[0:WIDTH, :], preferred_element_type=F32)
    acc += jnp.dot(o_lru, w_ref[WIDTH:2 * WIDTH, :], preferred_element_type=F32)
    acc += jnp.dot(om_ref[...], w_ref[2 * WIDTH:, :], preferred_element_type=F32)
    o_ref[...] = _stream_tile(refs[:n_src], n_lat_tiles) + mod_ref[0, 2:3, :] * acc


def _out_proj_call(stream, n, mod, yf, yb, cv, g, hf, hb, lru, o_mla, p, n_tiles, n_lat_tiles):
    row = lambda c: pl.BlockSpec((TILE, c), lambda j: (j, 0))
    consts = [p["ln_g"], p["ln_b"], p["ones_bd"], p["w_out"]]
    x_specs, x_args = _stream_specs(stream, n_lat_tiles)
    return pl.pallas_call(
        functools.partial(_out_proj_kernel, len(x_args), n_lat_tiles),
        grid=(n_tiles,),
        in_specs=x_specs + [pl.BlockSpec((1, 6, D_MODEL), lambda j: (jnp.minimum(j // n_lat_tiles, 1), 0, 0))]
                 + [row(WIDTH)] * 6 + [pl.BlockSpec((TILE, WIDTH), lambda j: (j, 1)), row(MLA_WIDTH)]
                 + [_const_spec(a.shape) for a in consts],
        out_specs=row(D_MODEL),
        out_shape=jax.ShapeDtypeStruct((n, D_MODEL), F32),
        compiler_params=_params(1),
        name="out_proj",
    )(*x_args, mod, yf, yb, cv, g, hf, hb, lru, o_mla, *consts)


def _ffn_kernel(tm, block0, seg_lo, seg_hi, mod_row, final, *refs):
    if final:
        (x_ref, prev_ref, next_ref, mod_ref, g_ref, wg_ref, wu_ref, wd_ref, cw_ref, cb_ref, fin_ref,
         o_ref, h_scr) = refs
    else:
        (x_ref, prev_ref, next_ref, mod_ref, g_ref, wg_ref, wu_ref, wd_ref, cw_ref, cb_ref,
         o_ref, h_scr) = refs
    c = pl.program_id(1)
    ext = tm + 2 * FFN_HALO

    @pl.when(c == 0)
    def _():
        sh = mod_ref[mod_row, 3:4, :]
        sc = 1.0 + mod_ref[mod_row, 4:5, :]
        g = g_ref[...]
        h_scr[0:FFN_HALO, :] = (_rms(prev_ref[...], g) * sc + sh).astype(BF16)
        h_scr[FFN_HALO:FFN_HALO + tm, :] = (_rms(x_ref[...], g) * sc + sh).astype(BF16)
        h_scr[FFN_HALO + tm:ext, :] = (_rms(next_ref[...], g) * sc + sh).astype(BF16)
        o_ref[...] = jnp.zeros_like(o_ref)

    ge = jnp.dot(h_scr[...], wg_ref[...], preferred_element_type=F32)
    grow = (block0 + pl.program_id(0)) * tm - FFN_HALO + _row_iota(ge.shape)
    ge = jnp.where(jnp.logical_and(grow >= seg_lo, grow < seg_hi), ge, 0.0)
    up_rows = pltpu.roll(ge, 1, 0)[FFN_HALO:FFN_HALO + tm, :]
    dn_rows = pltpu.roll(ge, ext - 1, 0)[FFN_HALO:FFN_HALO + tm, :]
    gate = cb_ref[...] + up_rows * cw_ref[0:1, :] + ge[FFN_HALO:FFN_HALO + tm, :] * cw_ref[1:2, :] + dn_rows * cw_ref[2:3, :]
    up = jnp.dot(h_scr[FFN_HALO:FFN_HALO + tm, :], wu_ref[...], preferred_element_type=F32)
    act = (_silu(gate) * up).astype(BF16)
    o_ref[...] += jnp.dot(act, wd_ref[...], preferred_element_type=F32)

    @pl.when(c == pl.num_programs(1) - 1)
    def _():
        out = x_ref[...] + mod_ref[mod_row, 5:6, :] * o_ref[...]
        if final:
            out = _rms(out, fin_ref[...])
        o_ref[...] = out


def _ffn_call(x_all, o_prev, mod, p, layer, tm, block0, n_blocks, seg_lo, seg_hi, mod_row, final_g, out_rows):
    n = x_all.shape[0]
    hb = tm // FFN_HALO
    nb = n // FFN_HALO
    n_chunks = D_FF // FFN_CHUNK
    final = final_g is not None
    in_specs = [
        pl.BlockSpec((tm, D_MODEL), lambda j, c: (block0 + j, 0), pipeline_mode=pl.Buffered(1)),
        pl.BlockSpec((FFN_HALO, D_MODEL), lambda j, c: (jnp.maximum((block0 + j) * hb - 1, 0), 0)),
        pl.BlockSpec((FFN_HALO, D_MODEL), lambda j, c: (jnp.minimum((block0 + j + 1) * hb, nb - 1), 0)),
        pl.BlockSpec((2, 6, D_MODEL), lambda j, c: (0, 0, 0)),
        pl.BlockSpec((1, D_MODEL), lambda j, c: (0, 0)),
        pl.BlockSpec((None, D_MODEL, FFN_CHUNK), lambda j, c: (layer, 0, c)),
        pl.BlockSpec((None, D_MODEL, FFN_CHUNK), lambda j, c: (layer, 0, c)),
        pl.BlockSpec((None, FFN_CHUNK, D_MODEL), lambda j, c: (layer, c, 0)),
        pl.BlockSpec((3, FFN_CHUNK), lambda j, c: (0, c)),
        pl.BlockSpec((1, FFN_CHUNK), lambda j, c: (0, c)),
    ]
    args = [x_all, x_all, x_all, mod, p["norm2"], p["w_gate"], p["w_up"], p["w_down"], p["ffn_conv"], p["ffn_conv_b"]]
    if final:
        in_specs.append(pl.BlockSpec((1, D_MODEL), lambda j, c: (0, 0)))
        args.append(final_g)
    aliases = {}
    if o_prev is not None:
        in_specs.append(pl.BlockSpec(memory_space=pl.ANY))
        args.append(o_prev)
        aliases = {len(args) - 1: 0}
    kern = functools.partial(_ffn_kernel, tm, block0, seg_lo, seg_hi, mod_row, final)
    if o_prev is not None:
        kern = _drop_ref(kern, len(args) - 1)
    return pl.pallas_call(
        kern,
        grid=(n_blocks, n_chunks),
        in_specs=in_specs,
        out_specs=pl.BlockSpec((tm, D_MODEL), lambda j, c: (block0 + j, 0)),
        out_shape=jax.ShapeDtypeStruct((out_rows, D_MODEL), F32),
        scratch_shapes=[pltpu.VMEM((tm + 2 * FFN_HALO, D_MODEL), BF16)],
        input_output_aliases=aliases,
        compiler_params=_params(2),
        name="conv_ffn",
    )(*args)


def _cast_kernel(x_ref, o_ref):
    o_ref[...] = x_ref[...].astype(o_ref.dtype)


def _cast_bf16(w):
    depth, r, c = w.shape
    rb = 1 << (((2 * 1024 * 1024) // c).bit_length() - 1)
    assert (depth * r) % rb == 0 and rb % 16 == 0
    out = pl.pallas_call(
        _cast_kernel,
        grid=(depth * r // rb,),
        in_specs=[pl.BlockSpec((rb, c), lambda j: (j, 0))],
        out_specs=pl.BlockSpec((rb, c), lambda j: (j, 0)),
        out_shape=jax.ShapeDtypeStruct((depth * r, c), BF16),
        compiler_params=_params(1),
        name="cast_bf16",
    )(w.reshape(depth * r, c))
    return out.reshape(depth, r, c)


def _drop_ref(kern, idx):
    def wrapped(*refs):
        return kern(*(refs[:idx] + refs[idx + 1:]))
    return wrapped


_ROPE_PERM = np.concatenate([np.arange(16, 32), np.arange(0, 16), np.arange(48, 64), np.arange(32, 48)])


def _block_diag(blocks):
    h, n, m = blocks.shape
    eye = jnp.eye(h, dtype=blocks.dtype)
    return (eye[:, None, :, None] * blocks[:, :, None, :]).reshape(h * n, h * m)


def _scan_consts():
    idx = np.arange(CHUNK)
    lower = (idx[None, :] <= idx[:, None]).astype(np.float32)
    upper = (idx[None, :] >= idx[:, None]).astype(np.float32)
    eye_g = np.eye(GROUP, dtype=np.float32)
    bd = lambda m: np.kron(eye_g, m)
    return {
        "tri_fwd": jnp.asarray(lower, BF16), "tri_rev": jnp.asarray(upper, BF16),
        "incl_fwd": jnp.asarray(bd(lower)), "incl_rev": jnp.asarray(bd(upper)),
        "strict_fwd": jnp.asarray(bd(lower - np.eye(CHUNK, dtype=np.float32))),
        "strict_rev": jnp.asarray(bd(upper - np.eye(CHUNK, dtype=np.float32))),
        "bd_mask": jnp.asarray(bd(np.ones((CHUNK, HEAD_DIM), np.float32))),
        "eye": jnp.asarray(np.eye(GW, dtype=np.float32)),
    }


def _rope_tables(n_lat, n_ctx):
    n_freq = ROPE // 4
    rows = n_lat // GRID_W
    row = jnp.repeat(jnp.arange(rows, dtype=F32), GRID_W)
    col = jnp.tile(jnp.arange(GRID_W, dtype=F32), rows)
    inv_freq = ROPE_THETA ** (-jnp.arange(n_freq, dtype=F32) / n_freq)
    ar, ac = row[:, None] * inv_freq, col[:, None] * inv_freq
    cos = jnp.concatenate([jnp.cos(ar), jnp.cos(ar), jnp.cos(ac), jnp.cos(ac)], axis=-1)
    sin = jnp.concatenate([-jnp.sin(ar), jnp.sin(ar), -jnp.sin(ac), jnp.sin(ac)], axis=-1)
    cos = jnp.concatenate([cos, jnp.ones((n_ctx, ROPE), F32)], axis=0)
    sin = jnp.concatenate([sin, jnp.zeros((n_ctx, ROPE), F32)], axis=0)
    return {"cc": jnp.concatenate([cos, cos], axis=-1), "ss": jnp.concatenate([sin, sin], axis=-1),
            "cs": jnp.concatenate([cos, sin], axis=-1)}


def kernel(x, c, ctx, c_ctx, ada_w, ada_b, norm1, norm2, w_in, w_out, rw_conv, rw_conv_b, rw_w0, rw_w_up, rw_a0, rw_a_up, rw_g_up, rw_k_k, rw_k_a, rw_r_k, rw_ln_g, rw_ln_b, rw_v0, rw_v_down, rw_v_up, lru_conv, lru_conv_b, lru_wa, lru_ba, lru_wx, lru_bx, lru_lambda, mla_q_norm, mla_w_qb, mla_kv_norm, mla_w_kvb, ffn_w_gate, ffn_w_up, ffn_conv, ffn_conv_b, ffn_w_down, final_norm):
    assert x.shape[0] == 1 and ctx.shape[0] == 1
    depth = ada_w.shape[0]
    n_lat, n_ctx = x.shape[1], ctx.shape[1]
    assert n_lat % FFN_TILE == 0 and n_ctx % TILE == 0 and n_lat % GRID_W == 0
    n = n_lat + n_ctx
    n_lat_tiles = n_lat // TILE
    seg = ((0, n_lat), (n_lat, n))

    x_all = (x[0], ctx[0])
    mods =_mod_call(jnp.stack([c[0], c_ctx], axis=1), ada_w, ada_b).reshape(depth, 2, 6, D_MODEL)
    tabs = _rope_tables(n_lat, n_ctx)
    consts = _scan_consts()
    ones_bd = jnp.asarray(np.kron(np.eye(HEADS, dtype=np.float32), np.ones((HEAD_DIM, HEAD_DIM), np.float32)), BF16)
    kv_chunk = 1280 if n % 1280 == 0 else TILE
    row2 = lambda a: a.reshape(1, -1)
    w_gate_bf, w_up_bf, w_down_bf = _cast_bf16(ffn_w_gate), _cast_bf16(ffn_w_up), _cast_bf16(ffn_w_down)

    v_first = None
    out = None
    for i in range(depth):
        last = i == depth - 1
        mla_off = 1760 + LRU_COLS
        rope_cols = mla_off + Q_RANK + KV_RANK + _ROPE_PERM
        vdown = rw_v_down[i - 1] if i > 0 else jnp.zeros((D_MODEL, LORA), F32)
        w_cat = jnp.concatenate([w_in[i][:, :1760], vdown, w_in[i][:, 1760:], w_in[i][:, rope_cols]], axis=1).astype(BF16)

        w_lora = jnp.zeros((RW_COLS - 3 * WIDTH, 6 * WIDTH), F32)
        b_lora = jnp.zeros((6 * WIDTH,), F32)
        for d in range(2):
            w_lora = w_lora.at[d * LORA:(d + 1) * LORA, d * WIDTH:(d + 1) * WIDTH].set(rw_w_up[i][d])
            w_lora = w_lora.at[(2 + d) * LORA:(3 + d) * LORA, (2 + d) * WIDTH:(3 + d) * WIDTH].set(rw_a_up[i][d])
            b_lora = b_lora.at[d * WIDTH:(d + 1) * WIDTH].set(rw_w0[i][d])
            b_lora = b_lora.at[(2 + d) * WIDTH:(3 + d) * WIDTH].set(rw_a0[i][d])
        w_lora = w_lora.at[4 * LORA:4 * LORA + GATE_LORA, 4 * WIDTH:5 * WIDTH].set(rw_g_up[i])
        if i > 0:
            w_lora = w_lora.at[4 * LORA + GATE_LORA:, 5 * WIDTH:].set(rw_v_up[i - 1])
            b_lora = b_lora.at[5 * WIDTH:].set(rw_v0[i - 1])
        ident = jnp.array([[0.0], [1.0], [0.0]], F32) * jnp.ones((1, LORA), F32)
        rw_p = {
            "rw_conv": jnp.concatenate([rw_conv[i], ident], axis=1),
            "rw_conv_b": row2(jnp.concatenate([rw_conv_b[i], jnp.zeros((LORA,), F32)])),
            "w_lora": w_lora.astype(BF16), "b_lora": row2(b_lora),
            "k_k": row2(rw_k_k[i]), "k_a": row2(rw_k_a[i]), "r_k": row2(rw_r_k[i]), "ones_bd": ones_bd,
        }
        lru_p = {
            "lru_conv": lru_conv[i], "lru_conv_b": row2(lru_conv_b[i]),
            "lru_wg": jnp.stack([jnp.concatenate([_block_diag(lru_wa[i][d]), _block_diag(lru_wx[i][d])], axis=1)
                                 for d in range(2)]).astype(BF16),
            "lru_bg": jnp.stack([jnp.concatenate([lru_ba[i][d], lru_bx[i][d]])[None] for d in range(2)]),
            "lru_lam": lru_lambda[i][:, None, :],
        }
        wq = mla_w_qb[i].reshape(Q_RANK, HEADS, NOPE + ROPE)
        mla_p = {
            "q_norm": row2(mla_q_norm[i]), "kv_norm": row2(mla_kv_norm[i]),
            "w_q": jnp.concatenate([wq[:, :, :NOPE].reshape(Q_RANK, -1), wq[:, :, NOPE:].reshape(Q_RANK, -1),
                                    wq[:, :, NOPE + _ROPE_PERM].reshape(Q_RANK, -1)], axis=1).astype(BF16),
            "w_kv": mla_w_kvb[i].astype(BF16),
        }
        out_p = {"ln_g": row2(rw_ln_g[i]), "ln_b": row2(rw_ln_b[i]), "ones_bd": ones_bd, "w_out": w_out[i].astype(BF16)}
        ffn_p = {"norm2": row2(norm2[i]), "w_gate": w_gate_bf, "w_up": w_up_bf, "w_down": w_down_bf,
                 "ffn_conv": ffn_conv[i], "ffn_conv_b": row2(ffn_conv_b[i])}

        rw, lru, q_h, k_h, v_h = _in_proj_call(x_all, n, mods[i], row2(norm1[i]), w_cat, tabs, mla_p, n_lat_tiles)

        r, v, kk, g, cv, lw, kd, bd = _rwkv_prep_call(rw, v_first, rw_p, seg)
        if i == 0:
            v_first = v
        y_f, y_b = _rwkv_scan_call(r, v, kk, lw, kd, bd, consts, n_lat_tiles)

        h_f, h_b = _lru_call(lru, lru_p, seg)

        o_mla = _attn_call(q_h, k_h, v_h, None, Q_TILE, 0, n_lat // Q_TILE, 0, kv_chunk, n // kv_chunk)
        if not last:
            o_mla = _attn_call(q_h, k_h, v_h, o_mla, TILE, n_lat_tiles, n_ctx // TILE, n_lat, TILE, n_ctx // TILE)

        n_tiles = n_lat_tiles if last else n // TILE
        x_mid = _out_proj_call(x_all, n, mods[i], y_f, y_b, cv, g, h_f, h_b, lru, o_mla, out_p, n_tiles, n_lat_tiles)

        if last:
            out = _ffn_call(x_mid, None, mods[i], ffn_p, i, FFN_TILE, 0, n_lat // FFN_TILE, 0, n_lat, 0,
                            row2(final_norm), n_lat)
        else:
            x_new = _ffn_call(x_mid, None, mods[i], ffn_p, i, FFN_TILE, 0, n_lat // FFN_TILE, 0, n_lat, 0, None, n)
            x_all = _ffn_call(x_mid, x_new, mods[i], ffn_p, i, TILE, n_lat_tiles, n_ctx // TILE, n_lat, n, 1, None, n)
    return out[None]
```
